```python
import math
import jax
import jax.numpy as jnp
from jax import lax
import numpy as np

D_MODEL = 1024
BATCH = 2
SEQ = 8192
DEPTH = 2

GRID_W = 64
CTX_LEN = 256
N_MIXERS = 4
GROUP_W = D_MODEL // N_MIXERS
HEAD_DIM = 64
N_HEADS = GROUP_W // HEAD_DIM
CONV_K = 3
W_LORA = 16
A_LORA = 16
G_LORA = 32
RWKV_DECAY_SCALE = math.exp(-0.5)
KV_HEADS = 2
Q_PER_KV = N_HEADS // KV_HEADS
KV_W = KV_HEADS * HEAD_DIM
WINDOW = 128
ATT_BLOCK = 128
ATT_SCALE = HEAD_DIM ** -0.5
ROPE_BASE = 10000.0
MLSTM_CHUNK = 128
N_GATE_COLS = 2 * 2 * N_HEADS
PEER_HEADS = 8
N_KEYS = 128
N_EXPERTS = N_KEYS * N_KEYS
PEER_TOPK = 16
PEER_QDIM = 256
PEER_HALF = PEER_QDIM // 2
PEER_BLOCK = 128
EPS = 1e-6
F32 = jnp.float32
IN_SIZES = (GROUP_W, GROUP_W, GROUP_W,
            GROUP_W, GROUP_W, GROUP_W, W_LORA, A_LORA, G_LORA,
            GROUP_W, KV_W, KV_W,
            GROUP_W, GROUP_W, GROUP_W, GROUP_W, N_GATE_COLS)
D_IN = sum(IN_SIZES)
IN_OFFSETS = tuple(int(o) for o in np.cumsum(IN_SIZES)[:-1])

kernel_name = 'hybrid_parallel_group_peer_dit_block'


def rms_norm(x, g):
    xf = x.astype(F32)
    y = xf * lax.rsqrt(jnp.mean(xf * xf, axis=-1, keepdims=True) + EPS)
    return (y * g.astype(F32)).astype(x.dtype)


def heads(t):
    return t.reshape(t.shape[:-1] + (N_HEADS, HEAD_DIM))


def head_norm_merge(y, g):
    return rms_norm(y, g).reshape(y.shape[:-2] + (GROUP_W,))


def rope_2d(x, row, col):
    quarter = HEAD_DIM // 4
    inv = ROPE_BASE ** (-jnp.arange(quarter, dtype=F32) / quarter)
    xf = x.astype(F32)
    extra = (1,) * (x.ndim - 3)

    def rot(xa, pos):
        ang = pos.astype(F32)[:, None] * inv[None, :]
        ang = ang.reshape((1, ang.shape[0]) + extra + (quarter,))
        cos, sin = jnp.cos(ang), jnp.sin(ang)
        x1, x2 = xa[..., :quarter], xa[..., quarter:]
        return jnp.concatenate([x1 * cos - x2 * sin, x2 * cos + x1 * sin], axis=-1)

    half = HEAD_DIM // 2
    return jnp.concatenate([rot(xf[..., :half], row), rot(xf[..., half:], col)], axis=-1).astype(x.dtype)


def conv_mixer(hx, b_gate, c_gate, w, g):
    u = c_gate * hx
    up = jnp.pad(u, ((0, 0), (1, 1), (0, 0)))
    y = b_gate * (w[0] * up[:, :-2] + w[1] * up[:, 1:-1] + w[2] * up[:, 2:])
    return head_norm_merge(heads(y), g)


def rwkv_stream(r, k, v, xw, xa, xg, w0, w2, a0, a2, g2, k_k, k_a):
    kk = heads((k * k_k).astype(F32))
    kk = kk * lax.rsqrt(jnp.sum(kk * kk, axis=-1, keepdims=True) + EPS)
    g = jax.nn.sigmoid(xg) @ g2
    per_dir = []
    for d in range(2):
        decay = jnp.exp(-RWKV_DECAY_SCALE * jax.nn.sigmoid(w0[d] + jnp.tanh(xw) @ w2[d]))
        a = jax.nn.sigmoid(a0[d] + xa @ a2[d])
        kd = k * (1 + (a - 1) * k_a)
        per_dir.append((heads(decay.astype(F32)), heads(a.astype(F32)), heads(kd.astype(F32))))
    return heads(r.astype(F32)), heads(v.astype(F32)), kk, g, per_dir


def rwkv_scan(S0, r, decay, k, v, kk, a, reverse):
    xs = tuple(jnp.moveaxis(t, 1, 0) for t in (r, decay, k, v, kk, a))

    def step(S, inp):
        r_t, w_t, k_t, v_t, kk_t, a_t = inp
        sa = jnp.einsum('bhvk,bhk->bhv', S, kk_t)
        S = (S * w_t[:, :, None, :] - sa[..., None] * (kk_t * a_t)[:, :, None, :]
             + v_t[..., None] * k_t[:, :, None, :])
        return S, jnp.einsum('bhvk,bhk->bhv', S, r_t)

    S, y = lax.scan(step, S0, xs, reverse=reverse)
    return S, jnp.moveaxis(y, 0, 1)


def rwkv_mixer(lat, ctx, w0, w2, a0, a2, g2, k_k, k_a, r_k, ln_g, need_ctx):
    sc = rwkv_stream(*ctx, w0, w2, a0, a2, g2, k_k, k_a)
    sl = rwkv_stream(*lat, w0, w2, a0, a2, g2, k_k, k_a)
    B = sl[0].shape[0]
    zero = jnp.zeros((B, N_HEADS, HEAD_DIM, HEAD_DIM), F32)
    ys_c, ys_l = [], []
    for d in range(2):
        dec_c, a_c, k_c = sc[4][d]
        S_c, y_c = rwkv_scan(zero, sc[0], dec_c, k_c, sc[1], sc[2], a_c, reverse=(d == 1))
        dec_l, a_l, k_l = sl[4][d]
        _, y_l = rwkv_scan(S_c, sl[0], dec_l, k_l, sl[1], sl[2], a_l, reverse=(d == 1))
        ys_c.append(y_c)
        ys_l.append(y_l)

    def finish(s, ys):
        r, v, _, g, per_dir = s
        y = rms_norm(ys[0] + ys[1], ln_g)
        bonus = (jnp.sum(r * per_dir[0][2] * r_k, axis=-1, keepdims=True)
                 + jnp.sum(r * per_dir[1][2] * r_k, axis=-1, keepdims=True)) * v
        return (y + bonus).reshape(y.shape[:-2] + (GROUP_W,)) * g

    y_lat = finish(sl, ys_l)
    y_ctx = finish(sc, ys_c) if need_ctx else None
    return y_lat, y_ctx


def attn_project(q, k, v, q_g, k_g):
    B, T, _ = q.shape
    q = rms_norm(q.reshape(B, T, KV_HEADS, Q_PER_KV, HEAD_DIM), q_g)
    k = rms_norm(k.reshape(B, T, KV_HEADS, HEAD_DIM), k_g)
    v = v.reshape(B, T, KV_HEADS, HEAD_DIM)
    return q, k, v


def latent_attention(q, k, v, kc, vc, sink):
    B, T = q.shape[:2]
    nb = T // ATT_BLOCK
    qb = q.reshape(B, nb, ATT_BLOCK, KV_HEADS, Q_PER_KV, HEAD_DIM)

    def band(t):
        tp = jnp.pad(t, ((0, 0), (ATT_BLOCK, ATT_BLOCK), (0, 0), (0, 0)))
        tp = tp.reshape(B, nb + 2, ATT_BLOCK, KV_HEADS, HEAD_DIM)
        return jnp.concatenate([tp[:, :-2], tp[:, 1:-1], tp[:, 2:]], axis=2)

    kw, vw = band(k), band(v)
    start = jnp.arange(nb)[:, None] * ATT_BLOCK
    qpos = start + jnp.arange(ATT_BLOCK)[None, :]
    kpos = start - ATT_BLOCK + jnp.arange(3 * ATT_BLOCK)[None, :]
    mask = ((jnp.abs(qpos[:, :, None] - kpos[:, None, :]) <= WINDOW)
            & (kpos[:, None, :] >= 0) & (kpos[:, None, :] < T))
    s_loc = jnp.einsum('bnqhgd,bnkhd->bhgnqk', qb, kw).astype(F32) * ATT_SCALE
    s_loc = jnp.where(mask, s_loc, -jnp.inf)
    s_ctx = jnp.einsum('bnqhgd,bchd->bhgnqc', qb, kc).astype(F32) * ATT_SCALE
    sk = jnp.broadcast_to(sink.astype(F32).reshape(KV_HEADS, Q_PER_KV, 1, 1, 1), s_loc.shape[:-1] + (1,))
    p = jax.nn.softmax(jnp.concatenate([s_loc, s_ctx, sk], axis=-1), axis=-1).astype(v.dtype)
    nw = 3 * ATT_BLOCK
    o = (jnp.einsum('bhgnqk,bnkhd->bnqhgd', p[..., :nw], vw)
         + jnp.einsum('bhgnqc,bchd->bnqhgd', p[..., nw:-1], vc))
    return o.reshape(B, T, N_HEADS, HEAD_DIM)


def ctx_attention(qc, kc, vc, sink):
    s = jnp.einsum('bqhgd,bchd->bhgqc', qc, kc).astype(F32) * ATT_SCALE
    sk = jnp.broadcast_to(sink.astype(F32).reshape(KV_HEADS, Q_PER_KV, 1, 1), s.shape[:-1] + (1,))
    p = jax.nn.softmax(jnp.concatenate([s, sk], axis=-1), axis=-1)[..., :-1].astype(vc.dtype)
    o = jnp.einsum('bhgqc,bchd->bqhgd', p, vc)
    return o.reshape(qc.shape[0], qc.shape[1], N_HEADS, HEAD_DIM)


def mlstm_chunk_scan(state, q, k, v, logi, logf):
    B, H, T, Dh = q.shape
    L = MLSTM_CHUNK
    nc = T // L

    def chunks(t):
        return jnp.moveaxis(t.reshape((B, H, nc, L) + t.shape[3:]), 2, 0)

    xs = (chunks(q), chunks(k), chunks(v), chunks(logi), chunks(logf))
    causal = jnp.tril(jnp.ones((L, L), dtype=bool))

    def step(carry, inp):
        C, n, m = carry
        qc, kc, vc, li, lf = inp
        b = jnp.cumsum(lf, axis=-1)
        d_intra = jnp.where(causal, b[..., :, None] - b[..., None, :] + li[..., None, :], -jnp.inf)
        d_inter = b + m[..., None]
        m_t = jnp.maximum(jnp.max(d_intra, axis=-1), d_inter)
        w_intra = jnp.exp(d_intra - m_t[..., None])
        w_inter = jnp.exp(d_inter - m_t)
        s = jnp.einsum('bhtd,bhsd->bhts', qc, kc) * w_intra
        num = (jnp.einsum('bhts,bhsd->bhtd', s, vc)
               + w_inter[..., None] * jnp.einsum('bhvk,bhtk->bhtv', C, qc))
        den = jnp.sum(s, axis=-1) + w_inter * jnp.einsum('bhk,bhtk->bht', n, qc)
        h = num / jnp.maximum(jnp.abs(den), jnp.exp(-m_t))[..., None]
        b_end = b[..., -1]
        d_end = b_end[..., None] - b + li
        m_new = jnp.maximum(b_end + m, jnp.max(d_end, axis=-1))
        w_end = jnp.exp(d_end - m_new[..., None])
        carry_decay = jnp.exp(b_end + m - m_new)
        C = carry_decay[..., None, None] * C + jnp.einsum('bhs,bhsv,bhsk->bhvk', w_end, vc, kc)
        n = carry_decay[..., None] * n + jnp.einsum('bhs,bhsk->bhk', w_end, kc)
        return (C, n, m_new), h

    state, h = lax.scan(step, state, xs)
    return state, jnp.moveaxis(h, 0, 2).reshape(B, H, T, Dh)


def mlstm_stream(q, k, v, o, gates, i_b, f_b):
    B, T, _ = q.shape

    def th(t):
        return jnp.moveaxis(heads(t.astype(F32)), 2, 1)

    gates = gates.astype(F32).reshape(B, T, 2, 2, N_HEADS) + jnp.stack([i_b, f_b], axis=1).astype(F32)
    gates = jnp.moveaxis(gates, 1, -1)
    logi = gates[:, :, 0]
    logf = jax.nn.log_sigmoid(gates[:, :, 1])
    return th(q), th(k) * (HEAD_DIM ** -0.5), th(v), jax.nn.sigmoid(o), logi, logf


def mlstm_mixer(lat, ctx, i_b, f_b, out_g, need_ctx):
    sc = mlstm_stream(*ctx, i_b, f_b)
    sl = mlstm_stream(*lat, i_b, f_b)
    B = sl[0].shape[0]
    zero = (jnp.zeros((B, N_HEADS, HEAD_DIM, HEAD_DIM), F32),
            jnp.zeros((B, N_HEADS, HEAD_DIM), F32),
            jnp.zeros((B, N_HEADS), F32))

    def flip(t):
        return jnp.flip(t, axis=2)

    def run(state, s, d):
        q, k, v, _, logi, logf = s
        li, lf = logi[:, d], logf[:, d]
        if d == 0:
            return mlstm_chunk_scan(state, q, k, v, li, lf)
        st, h = mlstm_chunk_scan(state, flip(q), flip(k), flip(v), flip(li), flip(lf))
        return st, flip(h)

    h_c, h_l = [], []
    for d in range(2):
        st, hc = run(zero, sc, d)
        _, hl = run(st, sl, d)
        h_c.append(hc)
        h_l.append(hl)

    def finish(s, hs):
        h = jnp.moveaxis(hs[0] + hs[1], 1, 2)
        return s[3] * head_norm_merge(h, out_g)

    y_lat = finish(sl, h_l)
    y_ctx = finish(sc, h_c) if need_ctx else None
    return y_lat, y_ctx


def peer_ffn(h, w_q, sub_keys, u, v):
    B, T, D = h.shape
    M = B * T
    hf = h.reshape(M, D)
    q = (hf @ w_q).astype(F32).reshape(M, PEER_HEADS, 2, PEER_HALF)
    s = jnp.einsum('mhpd,hpkd->mhpk', q, sub_keys.astype(F32))
    sv, si = lax.top_k(s, PEER_TOPK)
    n_cand = PEER_TOPK * PEER_TOPK
    cand = (sv[:, :, 0, :, None] + sv[:, :, 1, None, :]).reshape(M, PEER_HEADS, n_cand)
    cidx = (si[:, :, 0, :, None] * N_KEYS + si[:, :, 1, None, :]).reshape(M, PEER_HEADS, n_cand)
    tv, ti = lax.top_k(cand, PEER_TOPK)
    experts = jnp.take_along_axis(cidx, ti, axis=-1)
    gates = jax.nn.softmax(tv, axis=-1)
    nblk = M // PEER_BLOCK

    def block(args):
        hb, eb, gb = args
        act = jax.nn.gelu(jnp.einsum('mhkd,md->mhk', u[eb], hb).astype(F32))
        return jnp.einsum('mhk,mhkd->md', (gb * act).astype(v.dtype), v[eb])

    out = lax.map(block, (hf.reshape(nblk, PEER_BLOCK, D),
                          experts.reshape(nblk, PEER_BLOCK, PEER_HEADS, PEER_TOPK),
                          gates.reshape(nblk, PEER_BLOCK, PEER_HEADS, PEER_TOPK)))
    return out.reshape(B, T, D).astype(h.dtype)


def setup_inputs(seed: int = 0) -> dict:
    key = jax.random.key(seed)
    ks = jax.random.split(key, 40)
    L, D = DEPTH, D_MODEL

    def nrm(i, shape, scale):
        return scale * jax.random.normal(ks[i], shape, F32)

    return {
        'x': nrm(0, (BATCH, SEQ, D), 1.0),
        'c': nrm(1, (BATCH, D), 1.0),
        'ctx': nrm(2, (BATCH, CTX_LEN, D), 1.0),
        'c_ctx': nrm(3, (D,), 1.0),
        'ada_w': nrm(4, (L, D, 6 * D), 0.5 * D ** -0.5),
        'ada_b': nrm(5, (L, 6 * D), 0.02),
        'norm1_g': 1.0 + nrm(6, (L, D), 0.02),
        'norm2_g': 1.0 + nrm(7, (L, D), 0.02),
        'w_in': nrm(8, (L, D, D_IN), D ** -0.5),
        'w_out': nrm(9, (L, D, D), D ** -0.5),
        'conv_w': nrm(10, (L, CONV_K, GROUP_W), CONV_K ** -0.5),
        'conv_g': 1.0 + nrm(11, (L, N_HEADS, HEAD_DIM), 0.02),
        'rwkv_w0': nrm(12, (L, 2, GROUP_W), 0.5),
        'rwkv_w2': nrm(13, (L, 2, W_LORA, GROUP_W), W_LORA ** -0.5),
        'rwkv_a0': nrm(14, (L, 2, GROUP_W), 0.5),
        'rwkv_a2': nrm(15, (L, 2, A_LORA, GROUP_W), A_LORA ** -0.5),
        'rwkv_g2': nrm(16, (L, G_LORA, GROUP_W), G_LORA ** -0.5),
        'rwkv_kk': 0.85 + nrm(17, (L, GROUP_W), 0.05),
        'rwkv_ka': 1.0 + nrm(18, (L, GROUP_W), 0.05),
        'rwkv_rk': nrm(19, (L, N_HEADS, HEAD_DIM), 0.1),
        'rwkv_ln_g': 1.0 + nrm(20, (L, N_HEADS, HEAD_DIM), 0.02),
        'att_q_g': 1.0 + nrm(21, (L, HEAD_DIM), 0.02),
        'att_k_g': 1.0 + nrm(22, (L, HEAD_DIM), 0.02),
        'att_sink': nrm(23, (L, N_HEADS), 0.5),
        'att_out_g': 1.0 + nrm(24, (L, N_HEADS, HEAD_DIM), 0.02),
        'ml_i_b': nrm(25, (L, 2, N_HEADS), 0.1),
        'ml_f_b': 3.0 + nrm(26, (L, 2, N_HEADS), 0.5),
        'ml_out_g': 1.0 + nrm(27, (L, N_HEADS, HEAD_DIM), 0.02),
        'peer_wq': nrm(28, (L, D, PEER_HEADS * PEER_QDIM), D ** -0.5),
        'peer_keys': nrm(29, (L, PEER_HEADS, 2, N_KEYS, PEER_HALF), PEER_HALF ** -0.5),
        'peer_u': nrm(30, (L, N_EXPERTS, D), D ** -0.5),
        'peer_v': nrm(31, (L, N_EXPERTS, D), PEER_HEADS ** -0.5),
    }


def reference(x, c, ctx, c_ctx, ada_w, ada_b, norm1_g, norm2_g, w_in, w_out, conv_w, conv_g,
              rwkv_w0, rwkv_w2, rwkv_a0, rwkv_a2, rwkv_g2, rwkv_kk, rwkv_ka, rwkv_rk, rwkv_ln_g,
              att_q_g, att_k_g, att_sink, att_out_g, ml_i_b, ml_f_b, ml_out_g,
              peer_wq, peer_keys, peer_u, peer_v):
    B, T, D = x.shape
    ROWS = T // GRID_W
    row = jnp.repeat(jnp.arange(ROWS), GRID_W)
    col = jnp.arange(ROWS * GRID_W) % GRID_W
    for l in range(DEPTH):
        need_ctx = l < DEPTH - 1
        mod = jax.nn.silu(c) @ ada_w[l] + ada_b[l]
        mod_c = jax.nn.silu(c_ctx) @ ada_w[l] + ada_b[l]
        sh1, sc1, gt1, sh2, sc2, gt2 = jnp.split(mod[:, None, :], 6, axis=-1)
        csh1, csc1, cgt1, csh2, csc2, cgt2 = jnp.split(mod_c, 6, axis=-1)

        h = rms_norm(x, norm1_g[l]) * (1 + sc1) + sh1
        hc = rms_norm(ctx, norm1_g[l]) * (1 + csc1) + csh1
        P = jnp.split(h @ w_in[l], IN_OFFSETS, axis=-1)
        Pc = jnp.split(hc @ w_in[l], IN_OFFSETS, axis=-1)

        y_a = conv_mixer(P[0], P[1], P[2], conv_w[l], conv_g[l])
        y_b, yc_b = rwkv_mixer(P[3:9], Pc[3:9], rwkv_w0[l], rwkv_w2[l], rwkv_a0[l], rwkv_a2[l],
                               rwkv_g2[l], rwkv_kk[l], rwkv_ka[l], rwkv_rk[l], rwkv_ln_g[l], need_ctx)
        q, k, v = attn_project(P[9], P[10], P[11], att_q_g[l], att_k_g[l])
        q, k = rope_2d(q, row, col), rope_2d(k, row, col)
        qc, kc, vc = attn_project(Pc[9], Pc[10], Pc[11], att_q_g[l], att_k_g[l])
        y_c = head_norm_merge(latent_attention(q, k, v, kc, vc, att_sink[l]), att_out_g[l])
        y_d, yc_d = mlstm_mixer(P[12:17], Pc[12:17], ml_i_b[l], ml_f_b[l], ml_out_g[l], need_ctx)

        y = jnp.concatenate([t.astype(x.dtype) for t in (y_a, y_b, y_c, y_d)], axis=-1) @ w_out[l]
        x = x + gt1 * y
        h2 = rms_norm(x, norm2_g[l]) * (1 + sc2) + sh2
        x = x + gt2 * peer_ffn(h2, peer_wq[l], peer_keys[l], peer_u[l], peer_v[l])

        if need_ctx:
            yc_a = conv_mixer(Pc[0], Pc[1], Pc[2], conv_w[l], conv_g[l])
            yc_c = head_norm_merge(ctx_attention(qc, kc, vc, att_sink[l]), att_out_g[l])
            yc = jnp.concatenate([t.astype(ctx.dtype) for t in (yc_a, yc_b, yc_c, yc_d)], axis=-1) @ w_out[l]
            ctx = ctx + cgt1 * yc
            hc2 = rms_norm(ctx, norm2_g[l]) * (1 + csc2) + csh2
            ctx = ctx + cgt2 * peer_ffn(hc2, peer_wq[l], peer_keys[l], peer_u[l], peer_v[l])
    return x
```

```python
import functools
import math

import jax
import jax.numpy as jnp
import numpy as np
from jax import lax
from jax.experimental import pallas as pl
from jax.experimental.pallas import tpu as pltpu

D_MODEL = 1024
DEPTH = 2
GRID_W = 64
N_MIXERS = 4
GROUP_W = D_MODEL // N_MIXERS
HEAD_DIM = 64
N_HEADS = GROUP_W // HEAD_DIM
CONV_K = 3
W_LORA = 16
A_LORA = 16
G_LORA = 32
RWKV_DECAY_SCALE = math.exp(-0.5)
KV_HEADS = 2
Q_PER_KV = N_HEADS // KV_HEADS
KV_W = KV_HEADS * HEAD_DIM
WINDOW = 128
ATT_BLOCK = 128
ATT_SCALE = HEAD_DIM ** -0.5
ROPE_BASE = 10000.0
MLSTM_CHUNK = 128
N_GATE_COLS = 2 * 2 * N_HEADS
PEER_HEADS = 8
N_KEYS = 128
N_EXPERTS = N_KEYS * N_KEYS
PEER_TOPK = 16
PEER_QDIM = 256
PEER_HALF = PEER_QDIM // 2
PEER_BLOCK = 128
EPS = 1e-6
F32 = jnp.float32
BF16 = jnp.bfloat16
IN_SIZES = (GROUP_W, GROUP_W, GROUP_W,
            GROUP_W, GROUP_W, GROUP_W, W_LORA, A_LORA, G_LORA,
            GROUP_W, KV_W, KV_W,
            GROUP_W, GROUP_W, GROUP_W, GROUP_W, N_GATE_COLS)
D_IN = sum(IN_SIZES)
IN_OFFSETS = tuple(int(o) for o in np.cumsum(IN_SIZES)[:-1])

LANE = 128
VMEM_LIMIT_BYTES = 56 * 1024 * 1024


def _mm_kernel(x_ref, w_ref, o_ref):
    o_ref[...] = jnp.dot(x_ref[...].astype(BF16), w_ref[...], preferred_element_type=F32)


def _matmul(x, w, tm=512):
    M, K = x.shape
    N = w.shape[1]
    Np = -(-N // LANE) * LANE
    wb = w.astype(BF16)
    if Np != N:
        wb = jnp.pad(wb, ((0, 0), (0, Np - N)))
    tm = min(tm, M)
    assert M % tm == 0
    out = pl.pallas_call(
        _mm_kernel,
        grid=(M // tm,),
        in_specs=[pl.BlockSpec((tm, K), lambda i: (i, 0)),
                  pl.BlockSpec((K, Np), lambda i: (0, 0))],
        out_specs=pl.BlockSpec((tm, Np), lambda i: (i, 0)),
        out_shape=jax.ShapeDtypeStruct((M, Np), F32),
        compiler_params=pltpu.CompilerParams(
            dimension_semantics=("arbitrary",), vmem_limit_bytes=VMEM_LIMIT_BYTES),
        name="matmul",
    )(x, wb)
    return out[:, :N] if Np != N else out


def _mm3(x, w):
    lead = x.shape[:-1]
    return _matmul(x.reshape(-1, x.shape[-1]), w).reshape(lead + (w.shape[1],))


def rms_norm(x, g):
    xf = x.astype(F32)
    y = xf * lax.rsqrt(jnp.mean(xf * xf, axis=-1, keepdims=True) + EPS)
    return (y * g.astype(F32)).astype(x.dtype)


def heads(t):
    return t.reshape(t.shape[:-1] + (N_HEADS, HEAD_DIM))


def head_norm_merge(y, g):
    return rms_norm(y, g).reshape(y.shape[:-2] + (GROUP_W,))


def rope_2d(x, row, col):
    quarter = HEAD_DIM // 4
    inv = ROPE_BASE ** (-jnp.arange(quarter, dtype=F32) / quarter)
    xf = x.astype(F32)
    extra = (1,) * (x.ndim - 3)

    def rot(xa, pos):
        ang = pos.astype(F32)[:, None] * inv[None, :]
        ang = ang.reshape((1, ang.shape[0]) + extra + (quarter,))
        cos, sin = jnp.cos(ang), jnp.sin(ang)
        x1, x2 = xa[..., :quarter], xa[..., quarter:]
        return jnp.concatenate([x1 * cos - x2 * sin, x2 * cos + x1 * sin], axis=-1)

    half = HEAD_DIM // 2
    return jnp.concatenate([rot(xf[..., :half], row), rot(xf[..., half:], col)], axis=-1).astype(x.dtype)


def conv_mixer(hx, b_gate, c_gate, w, g):
    u = c_gate * hx
    up = jnp.pad(u, ((0, 0), (1, 1), (0, 0)))
    y = b_gate * (w[0] * up[:, :-2] + w[1] * up[:, 1:-1] + w[2] * up[:, 2:])
    return head_norm_merge(heads(y), g)


def rwkv_stream(r, k, v, xw, xa, xg, w0, w2, a0, a2, g2, k_k, k_a):
    kk = heads((k * k_k).astype(F32))
    kk = kk * lax.rsqrt(jnp.sum(kk * kk, axis=-1, keepdims=True) + EPS)
    g = jax.nn.sigmoid(xg) @ g2
    per_dir = []
    for d in range(2):
        decay = jnp.exp(-RWKV_DECAY_SCALE * jax.nn.sigmoid(w0[d] + jnp.tanh(xw) @ w2[d]))
        a = jax.nn.sigmoid(a0[d] + xa @ a2[d])
        kd = k * (1 + (a - 1) * k_a)
        per_dir.append((heads(decay.astype(F32)), heads(a.astype(F32)), heads(kd.astype(F32))))
    return heads(r.astype(F32)), heads(v.astype(F32)), kk, g, per_dir


def rwkv_scan(S0, r, decay, k, v, kk, a, reverse):
    xs = tuple(jnp.moveaxis(t, 1, 0) for t in (r, decay, k, v, kk, a))

    def step(S, inp):
        r_t, w_t, k_t, v_t, kk_t, a_t = inp
        sa = jnp.einsum('bhvk,bhk->bhv', S, kk_t)
        S = (S * w_t[:, :, None, :] - sa[..., None] * (kk_t * a_t)[:, :, None, :]
             + v_t[..., None] * k_t[:, :, None, :])
        return S, jnp.einsum('bhvk,bhk->bhv', S, r_t)

    S, y = lax.scan(step, S0, xs, reverse=reverse)
    return S, jnp.moveaxis(y, 0, 1)


def rwkv_mixer(lat, ctx, w0, w2, a0, a2, g2, k_k, k_a, r_k, ln_g, need_ctx):
    sc = rwkv_stream(*ctx, w0, w2, a0, a2, g2, k_k, k_a)
    sl = rwkv_stream(*lat, w0, w2, a0, a2, g2, k_k, k_a)
    B = sl[0].shape[0]
    zero = jnp.zeros((B, N_HEADS, HEAD_DIM, HEAD_DIM), F32)
    ys_c, ys_l = [], []
    for d in range(2):
        dec_c, a_c, k_c = sc[4][d]
        S_c, y_c = rwkv_scan(zero, sc[0], dec_c, k_c, sc[1], sc[2], a_c, reverse=(d == 1))
        dec_l, a_l, k_l = sl[4][d]
        _, y_l = rwkv_scan(S_c, sl[0], dec_l, k_l, sl[1], sl[2], a_l, reverse=(d == 1))
        ys_c.append(y_c)
        ys_l.append(y_l)

    def finish(s, ys):
        r, v, _, g, per_dir = s
        y = rms_norm(ys[0] + ys[1], ln_g)
        bonus = (jnp.sum(r * per_dir[0][2] * r_k, axis=-1, keepdims=True)
                 + jnp.sum(r * per_dir[1][2] * r_k, axis=-1, keepdims=True)) * v
        return (y + bonus).reshape(y.shape[:-2] + (GROUP_W,)) * g

    y_lat = finish(sl, ys_l)
    y_ctx = finish(sc, ys_c) if need_ctx else None
    return y_lat, y_ctx


def attn_project(q, k, v, q_g, k_g):
    B, T, _ = q.shape
    q = rms_norm(q.reshape(B, T, KV_HEADS, Q_PER_KV, HEAD_DIM), q_g)
    k = rms_norm(k.reshape(B, T, KV_HEADS, HEAD_DIM), k_g)
    v = v.reshape(B, T, KV_HEADS, HEAD_DIM)
    return q, k, v


def latent_attention(q, k, v, kc, vc, sink):
    B, T = q.shape[:2]
    nb = T // ATT_BLOCK
    qb = q.reshape(B, nb, ATT_BLOCK, KV_HEADS, Q_PER_KV, HEAD_DIM)

    def band(t):
        tp = jnp.pad(t, ((0, 0), (ATT_BLOCK, ATT_BLOCK), (0, 0), (0, 0)))
        tp = tp.reshape(B, nb + 2, ATT_BLOCK, KV_HEADS, HEAD_DIM)
        return jnp.concatenate([tp[:, :-2], tp[:, 1:-1], tp[:, 2:]], axis=2)

    kw, vw = band(k), band(v)
    start = jnp.arange(nb)[:, None] * ATT_BLOCK
    qpos = start + jnp.arange(ATT_BLOCK)[None, :]
    kpos = start - ATT_BLOCK + jnp.arange(3 * ATT_BLOCK)[None, :]
    mask = ((jnp.abs(qpos[:, :, None] - kpos[:, None, :]) <= WINDOW)
            & (kpos[:, None, :] >= 0) & (kpos[:, None, :] < T))
    s_loc = jnp.einsum('bnqhgd,bnkhd->bhgnqk', qb, kw).astype(F32) * ATT_SCALE
    s_loc = jnp.where(mask, s_loc, -jnp.inf)
    s_ctx = jnp.einsum('bnqhgd,bchd->bhgnqc', qb, kc).astype(F32) * ATT_SCALE
    sk = jnp.broadcast_to(sink.astype(F32).reshape(KV_HEADS, Q_PER_KV, 1, 1, 1), s_loc.shape[:-1] + (1,))
    p = jax.nn.softmax(jnp.concatenate([s_loc, s_ctx, sk], axis=-1), axis=-1).astype(v.dtype)
    nw = 3 * ATT_BLOCK
    o = (jnp.einsum('bhgnqk,bnkhd->bnqhgd', p[..., :nw], vw)
         + jnp.einsum('bhgnqc,bchd->bnqhgd', p[..., nw:-1], vc))
    return o.reshape(B, T, N_HEADS, HEAD_DIM)


def ctx_attention(qc, kc, vc, sink):
    s = jnp.einsum('bqhgd,bchd->bhgqc', qc, kc).astype(F32) * ATT_SCALE
    sk = jnp.broadcast_to(sink.astype(F32).reshape(KV_HEADS, Q_PER_KV, 1, 1), s.shape[:-1] + (1,))
    p = jax.nn.softmax(jnp.concatenate([s, sk], axis=-1), axis=-1)[..., :-1].astype(vc.dtype)
    o = jnp.einsum('bhgqc,bchd->bqhgd', p, vc)
    return o.reshape(qc.shape[0], qc.shape[1], N_HEADS, HEAD_DIM)


def mlstm_chunk_scan(state, q, k, v, logi, logf):
    B, H, T, Dh = q.shape
    L = MLSTM_CHUNK
    nc = T // L

    def chunks(t):
        return jnp.moveaxis(t.reshape((B, H, nc, L) + t.shape[3:]), 2, 0)

    xs = (chunks(q), chunks(k), chunks(v), chunks(logi), chunks(logf))
    causal = jnp.tril(jnp.ones((L, L), dtype=bool))

    def step(carry, inp):
        C, n, m = carry
        qc, kc, vc, li, lf = inp
        b = jnp.cumsum(lf, axis=-1)
        d_intra = jnp.where(causal, b[..., :, None] - b[..., None, :] + li[..., None, :], -jnp.inf)
        d_inter = b + m[..., None]
        m_t = jnp.maximum(jnp.max(d_intra, axis=-1), d_inter)
        w_intra = jnp.exp(d_intra - m_t[..., None])
        w_inter = jnp.exp(d_inter - m_t)
        s = jnp.einsum('bhtd,bhsd->bhts', qc, kc) * w_intra
        num = (jnp.einsum('bhts,bhsd->bhtd', s, vc)
               + w_inter[..., None] * jnp.einsum('bhvk,bhtk->bhtv', C, qc))
        den = jnp.sum(s, axis=-1) + w_inter * jnp.einsum('bhk,bhtk->bht', n, qc)
        h = num / jnp.maximum(jnp.abs(den), jnp.exp(-m_t))[..., None]
        b_end = b[..., -1]
        d_end = b_end[..., None] - b + li
        m_new = jnp.maximum(b_end + m, jnp.max(d_end, axis=-1))
        w_end = jnp.exp(d_end - m_new[..., None])
        carry_decay = jnp.exp(b_end + m - m_new)
        C = carry_decay[..., None, None] * C + jnp.einsum('bhs,bhsv,bhsk->bhvk', w_end, vc, kc)
        n = carry_decay[..., None] * n + jnp.einsum('bhs,bhsk->bhk', w_end, kc)
        return (C, n, m_new), h

    state, h = lax.scan(step, state, xs)
    return state, jnp.moveaxis(h, 0, 2).reshape(B, H, T, Dh)


def mlstm_stream(q, k, v, o, gates, i_b, f_b):
    B, T, _ = q.shape

    def th(t):
        return jnp.moveaxis(heads(t.astype(F32)), 2, 1)

    gates = gates.astype(F32).reshape(B, T, 2, 2, N_HEADS) + jnp.stack([i_b, f_b], axis=1).astype(F32)
    gates = jnp.moveaxis(gates, 1, -1)
    logi = gates[:, :, 0]
    logf = jax.nn.log_sigmoid(gates[:, :, 1])
    return th(q), th(k) * (HEAD_DIM ** -0.5), th(v), jax.nn.sigmoid(o), logi, logf


def mlstm_mixer(lat, ctx, i_b, f_b, out_g, need_ctx):
    sc = mlstm_stream(*ctx, i_b, f_b)
    sl = mlstm_stream(*lat, i_b, f_b)
    B = sl[0].shape[0]
    zero = (jnp.zeros((B, N_HEADS, HEAD_DIM, HEAD_DIM), F32),
            jnp.zeros((B, N_HEADS, HEAD_DIM), F32),
            jnp.zeros((B, N_HEADS), F32))

    def flip(t):
        return jnp.flip(t, axis=2)

    def run(state, s, d):
        q, k, v, _, logi, logf = s
        li, lf = logi[:, d], logf[:, d]
        if d == 0:
            return mlstm_chunk_scan(state, q, k, v, li, lf)
        st, h = mlstm_chunk_scan(state, flip(q), flip(k), flip(v), flip(li), flip(lf))
        return st, flip(h)

    h_c, h_l = [], []
    for d in range(2):
        st, hc = run(zero, sc, d)
        _, hl = run(st, sl, d)
        h_c.append(hc)
        h_l.append(hl)

    def finish(s, hs):
        h = jnp.moveaxis(hs[0] + hs[1], 1, 2)
        return s[3] * head_norm_merge(h, out_g)

    y_lat = finish(sl, h_l)
    y_ctx = finish(sc, h_c) if need_ctx else None
    return y_lat, y_ctx


def peer_ffn(h, w_q, sub_keys, u, v):
    B, T, D = h.shape
    M = B * T
    hf = h.reshape(M, D)
    q = _matmul(hf, w_q).astype(F32).reshape(M, PEER_HEADS, 2, PEER_HALF)
    s = jnp.einsum('mhpd,hpkd->mhpk', q, sub_keys.astype(F32))
    sv, si = lax.top_k(s, PEER_TOPK)
    n_cand = PEER_TOPK * PEER_TOPK
    cand = (sv[:, :, 0, :, None] + sv[:, :, 1, None, :]).reshape(M, PEER_HEADS, n_cand)
    cidx = (si[:, :, 0, :, None] * N_KEYS + si[:, :, 1, None, :]).reshape(M, PEER_HEADS, n_cand)
    tv, ti = lax.top_k(cand, PEER_TOPK)
    experts = jnp.take_along_axis(cidx, ti, axis=-1)
    gates = jax.nn.softmax(tv, axis=-1)
    nblk = M // PEER_BLOCK

    def block(args):
        hb, eb, gb = args
        act = jax.nn.gelu(jnp.einsum('mhkd,md->mhk', u[eb], hb).astype(F32))
        return jnp.einsum('mhk,mhkd->md', (gb * act).astype(v.dtype), v[eb])

    out = lax.map(block, (hf.reshape(nblk, PEER_BLOCK, D),
                          experts.reshape(nblk, PEER_BLOCK, PEER_HEADS, PEER_TOPK),
                          gates.reshape(nblk, PEER_BLOCK, PEER_HEADS, PEER_TOPK)))
    return out.reshape(B, T, D).astype(h.dtype)


def kernel(x, c, ctx, c_ctx, ada_w, ada_b, norm1_g, norm2_g, w_in, w_out, conv_w, conv_g,
           rwkv_w0, rwkv_w2, rwkv_a0, rwkv_a2, rwkv_g2, rwkv_kk, rwkv_ka, rwkv_rk, rwkv_ln_g,
           att_q_g, att_k_g, att_sink, att_out_g, ml_i_b, ml_f_b, ml_out_g,
           peer_wq, peer_keys, peer_u, peer_v):
    B, T, D = x.shape
    ROWS = T // GRID_W
    row = jnp.repeat(jnp.arange(ROWS), GRID_W)
    col = jnp.arange(ROWS * GRID_W) % GRID_W
    for l in range(DEPTH):
        need_ctx = l < DEPTH - 1
        mod = jax.nn.silu(c) @ ada_w[l] + ada_b[l]
        mod_c = jax.nn.silu(c_ctx) @ ada_w[l] + ada_b[l]
        sh1, sc1, gt1, sh2, sc2, gt2 = jnp.split(mod[:, None, :], 6, axis=-1)
        csh1, csc1, cgt1, csh2, csc2, cgt2 = jnp.split(mod_c, 6, axis=-1)

        h = rms_norm(x, norm1_g[l]) * (1 + sc1) + sh1
        hc = rms_norm(ctx, norm1_g[l]) * (1 + csc1) + csh1
        P = jnp.split(_mm3(h, w_in[l]), IN_OFFSETS, axis=-1)
        Pc = jnp.split(_mm3(hc, w_in[l]), IN_OFFSETS, axis=-1)

        y_a = conv_mixer(P[0], P[1], P[2], conv_w[l], conv_g[l])
        y_b, yc_b = rwkv_mixer(P[3:9], Pc[3:9], rwkv_w0[l], rwkv_w2[l], rwkv_a0[l], rwkv_a2[l],
                               rwkv_g2[l], rwkv_kk[l], rwkv_ka[l], rwkv_rk[l], rwkv_ln_g[l], need_ctx)
        q, k, v = attn_project(P[9], P[10], P[11], att_q_g[l], att_k_g[l])
        q, k = rope_2d(q, row, col), rope_2d(k, row, col)
        qc, kc, vc = attn_project(Pc[9], Pc[10], Pc[11], att_q_g[l], att_k_g[l])
        y_c = head_norm_merge(latent_attention(q, k, v, kc, vc, att_sink[l]), att_out_g[l])
        y_d, yc_d = mlstm_mixer(P[12:17], Pc[12:17], ml_i_b[l], ml_f_b[l], ml_out_g[l], need_ctx)

        y = _mm3(jnp.concatenate([t.astype(x.dtype) for t in (y_a, y_b, y_c, y_d)], axis=-1), w_out[l])
        x = x + gt1 * y
        h2 = rms_norm(x, norm2_g[l]) * (1 + sc2) + sh2
        x = x + gt2 * peer_ffn(h2, peer_wq[l], peer_keys[l], peer_u[l], peer_v[l])

        if need_ctx:
            yc_a = conv_mixer(Pc[0], Pc[1], Pc[2], conv_w[l], conv_g[l])
            yc_c = head_norm_merge(ctx_attention(qc, kc, vc, att_sink[l]), att_out_g[l])
            yc = _mm3(jnp.concatenate([t.astype(ctx.dtype) for t in (yc_a, yc_b, yc_c, yc_d)], axis=-1), w_out[l])
            ctx = ctx + cgt1 * yc
            hc2 = rms_norm(ctx, norm2_g[l]) * (1 + csc2) + csh2
            ctx = ctx + cgt2 * peer_ffn(hc2, peer_wq[l], peer_keys[l], peer_u[l], peer_v[l])
    return x
```

```python
import functools
import math

import jax
import jax.numpy as jnp
import numpy as np
from jax import lax
from jax.experimental import pallas as pl
from jax.experimental.pallas import tpu as pltpu

D_MODEL = 1024
DEPTH = 2
GRID_W = 64
N_MIXERS = 4
GROUP_W = D_MODEL // N_MIXERS
HEAD_DIM = 64
N_HEADS = GROUP_W // HEAD_DIM
CONV_K = 3
W_LORA = 16
A_LORA = 16
G_LORA = 32
RWKV_DECAY_SCALE = math.exp(-0.5)
KV_HEADS = 2
Q_PER_KV = N_HEADS // KV_HEADS
KV_W = KV_HEADS * HEAD_DIM
WINDOW = 128
ATT_BLOCK = 128
ATT_SCALE = HEAD_DIM ** -0.5
ROPE_BASE = 10000.0
MLSTM_CHUNK = 128
N_GATE_COLS = 2 * 2 * N_HEADS
PEER_HEADS = 8
N_KEYS = 128
N_EXPERTS = N_KEYS * N_KEYS
PEER_TOPK = 16
PEER_QDIM = 256
PEER_HALF = PEER_QDIM // 2
PEER_BLOCK = 128
EPS = 1e-6
F32 = jnp.float32
BF16 = jnp.bfloat16
IN_SIZES = (GROUP_W, GROUP_W, GROUP_W,
            GROUP_W, GROUP_W, GROUP_W, W_LORA, A_LORA, G_LORA,
            GROUP_W, KV_W, KV_W,
            GROUP_W, GROUP_W, GROUP_W, GROUP_W, N_GATE_COLS)
D_IN = sum(IN_SIZES)
IN_OFFSETS = tuple(int(o) for o in np.cumsum(IN_SIZES)[:-1])

LANE = 128
VMEM_LIMIT_BYTES = 56 * 1024 * 1024


def _mm_kernel(x_ref, w_ref, o_ref):
    o_ref[...] = jnp.dot(x_ref[...].astype(BF16), w_ref[...], preferred_element_type=F32)


def _matmul(x, w, tm=512):
    M, K = x.shape
    N = w.shape[1]
    Np = -(-N // LANE) * LANE
    wb = w.astype(BF16)
    if Np != N:
        wb = jnp.pad(wb, ((0, 0), (0, Np - N)))
    tm = min(tm, M)
    assert M % tm == 0
    out = pl.pallas_call(
        _mm_kernel,
        grid=(M // tm,),
        in_specs=[pl.BlockSpec((tm, K), lambda i: (i, 0)),
                  pl.BlockSpec((K, Np), lambda i: (0, 0))],
        out_specs=pl.BlockSpec((tm, Np), lambda i: (i, 0)),
        out_shape=jax.ShapeDtypeStruct((M, Np), F32),
        compiler_params=pltpu.CompilerParams(
            dimension_semantics=("arbitrary",), vmem_limit_bytes=VMEM_LIMIT_BYTES),
        name="matmul",
    )(x, wb)
    return out[:, :N] if Np != N else out


def _mm3(x, w):
    lead = x.shape[:-1]
    return _matmul(x.reshape(-1, x.shape[-1]), w).reshape(lead + (w.shape[1],))


_PEER_CAND_ROWS = 80
_NEG_INF = float("-inf")


def _gelu_tanh(x):
    c = math.sqrt(2.0 / math.pi)
    return 0.5 * x * (1.0 + jnp.tanh(c * (x + 0.044715 * (x * x * x))))


def _top16_sorted(x, n_rows):
    iota = lax.broadcasted_iota(jnp.int32, x.shape, 0)
    vals = []
    for _ in range(PEER_TOPK):
        mx = jnp.max(x, axis=0, keepdims=True)
        vals.append(mx)
        pos = jnp.min(jnp.where(x == mx, iota, n_rows), axis=0, keepdims=True)
        x = jnp.where(iota == pos, _NEG_INF, x)
    return vals


def _peer_candidates(sv1, sv2):
    row8 = lax.broadcasted_iota(jnp.int32, (8, sv1.shape[1]), 0)
    blocks = [sv1[0:1, :] + sv2[0:8, :], sv1[0:1, :] + sv2[8:16, :]]
    for a in range(1, 8):
        n_valid = PEER_TOPK // (a + 1)
        blk = sv1[a:a + 1, :] + sv2[0:8, :]
        blocks.append(blk if n_valid >= 8 else jnp.where(row8 < n_valid, blk, _NEG_INF))
    blocks.append(sv1[8:16, :] + sv2[0:1, :])
    return jnp.concatenate(blocks, axis=0)


def _peer_kernel(h_ref, wq_ref, keys_ref, u_ref, vt_ref, o_ref,
                 hb_ref, s_ref, sv_ref, e1_ref, e2_ref, thr_ref, act_ref, wg_ref, *, tm, te):
    j = pl.program_id(1)
    n_lg = tm // LANE
    n_ib = te // N_KEYS

    @pl.when(j == 0)
    def _prepare():
        hb = h_ref[...].astype(BF16)
        hb_ref[...] = hb
        qb = jnp.dot(hb, wq_ref[...], preferred_element_type=F32).astype(BF16)
        for hp in range(2 * PEER_HEADS):
            s_ref[hp] = lax.dot_general(keys_ref[hp], qb[:, hp * PEER_HALF:(hp + 1) * PEER_HALF],
                                        (((1,), (1,)), ((), ())), preferred_element_type=F32)

        def top_body(it, carry):
            hp = it // n_lg
            off = pl.multiple_of((it % n_lg) * LANE, LANE)
            vals = _top16_sorted(s_ref[hp, :, pl.ds(off, LANE)], N_KEYS)
            sv_ref[hp, :, pl.ds(off, LANE)] = jnp.concatenate(vals, axis=0)
            return carry

        lax.fori_loop(0, 2 * PEER_HEADS * n_lg, top_body, 0)

        def head_body(it, carry):
            h = it // n_lg
            off = pl.multiple_of((it % n_lg) * LANE, LANE)
            sv1 = sv_ref[2 * h, :, pl.ds(off, LANE)]
            sv2 = sv_ref[2 * h + 1, :, pl.ds(off, LANE)]
            tv = _top16_sorted(_peer_candidates(sv1, sv2), _PEER_CAND_ROWS)
            z = jnp.zeros_like(tv[0])
            for t in tv:
                z = z + jnp.exp(t - tv[0])
            thr_ref[h, :, pl.ds(off, LANE)] = jnp.broadcast_to(tv[PEER_TOPK - 1], (8, LANE))
            e1_ref[h, :, pl.ds(off, LANE)] = jnp.exp(s_ref[2 * h, :, pl.ds(off, LANE)] - sv1[0:1, :])
            e2_ref[h, :, pl.ds(off, LANE)] = jnp.exp(s_ref[2 * h + 1, :, pl.ds(off, LANE)] - sv2[0:1, :]) / z
            return carry

        lax.fori_loop(0, PEER_HEADS * n_lg, head_body, 0)
        o_ref[...] = jnp.zeros_like(o_ref)

    act_ref[...] = lax.dot_general(u_ref[...], hb_ref[...], (((1,), (1,)), ((), ())),
                                   preferred_element_type=F32)

    i0 = pl.multiple_of(j * n_ib, 8)

    def lg_body(lg, carry):
        ls = pl.ds(pl.multiple_of(lg * LANE, LANE), LANE)
        s1t = [s_ref[2 * h, pl.ds(i0, 8), ls] for h in range(PEER_HEADS)]
        e1t = [e1_ref[h, pl.ds(i0, 8), ls] for h in range(PEER_HEADS)]
        for ib in range(n_ib):
            w = jnp.zeros((N_KEYS, LANE), F32)
            for h in range(PEER_HEADS):
                sel = (s_ref[2 * h + 1, :, ls] + s1t[h][ib:ib + 1, :]) >= thr_ref[h, 0:1, ls]
                w = w + jnp.where(sel, e2_ref[h, :, ls] * e1t[h][ib:ib + 1, :], 0.0)
            rs = slice(ib * N_KEYS, (ib + 1) * N_KEYS)
            wg_ref[rs, ls] = (w * _gelu_tanh(act_ref[rs, ls])).astype(BF16)
        return carry

    lax.fori_loop(0, n_lg, lg_body, 0)
    o_ref[...] += jnp.dot(vt_ref[...], wg_ref[...], preferred_element_type=F32)


def _peer_dense(hf, wq_b, keys_b, u_b, vt_b, tm=512, te=1024):
    M, D = hf.shape
    E = u_b.shape[0]
    tm = min(tm, M)
    assert M % tm == 0 and E % te == 0 and tm % LANE == 0 and te == 8 * N_KEYS
    kern = functools.partial(_peer_kernel, tm=tm, te=te)
    return pl.pallas_call(
        kern,
        grid=(M // tm, E // te),
        in_specs=[pl.BlockSpec((tm, D), lambda i, j: (i, 0)),
                  pl.BlockSpec(wq_b.shape, lambda i, j: (0, 0)),
                  pl.BlockSpec(keys_b.shape, lambda i, j: (0, 0, 0)),
                  pl.BlockSpec((te, D), lambda i, j: (j, 0)),
                  pl.BlockSpec((D, te), lambda i, j: (0, j))],
        out_specs=pl.BlockSpec((D, tm), lambda i, j: (0, i)),
        out_shape=jax.ShapeDtypeStruct((D, M), F32),
        scratch_shapes=[pltpu.VMEM((tm, D), BF16),
                        pltpu.VMEM((2 * PEER_HEADS, N_KEYS, tm), F32),
                        pltpu.VMEM((2 * PEER_HEADS, PEER_TOPK, tm), F32),
                        pltpu.VMEM((PEER_HEADS, N_KEYS, tm), F32),
                        pltpu.VMEM((PEER_HEADS, N_KEYS, tm), F32),
                        pltpu.VMEM((PEER_HEADS, 8, tm), F32),
                        pltpu.VMEM((te, tm), F32),
                        pltpu.VMEM((te, tm), BF16)],
        compiler_params=pltpu.CompilerParams(
            dimension_semantics=("arbitrary", "arbitrary"), vmem_limit_bytes=VMEM_LIMIT_BYTES),
        name="peer_dense",
    )(hf, wq_b, keys_b, u_b, vt_b)


_RWKV_UNROLL = 8


def _rwkv_kernel(sh_f_ref, sh_b_ref, d0_ref, d1_ref, y0_ref, y1_ref, s_ref, sa_ref, *, tb, n_batch):
    i = pl.program_id(0)
    n_ch = 2 * n_batch
    n_tiles = tb // _RWKV_UNROLL

    @pl.when(i == 0)
    def _init():
        s_ref[...] = jnp.zeros_like(s_ref)

    lane = lax.broadcasted_iota(jnp.int32, (GROUP_W, GROUP_W), 1)
    sub = lax.broadcasted_iota(jnp.int32, (GROUP_W, GROUP_W), 0)
    seg_ones = jnp.where(lane // HEAD_DIM == sub // HEAD_DIM, 1.0, 0.0).astype(BF16)
    lane_v = lax.broadcasted_iota(jnp.int32, (HEAD_DIM, GROUP_W), 1)
    sub_v = lax.broadcasted_iota(jnp.int32, (HEAD_DIM, GROUP_W), 0)
    eye = (lane_v % HEAD_DIM == sub_v)

    def seg_sum(p):
        return jnp.dot(p.astype(BF16), seg_ones, preferred_element_type=F32)

    def chain_refs(c):
        d, b = divmod(c, n_batch)
        return d, b, (sh_f_ref, d0_ref) if d == 0 else (sh_b_ref, d1_ref)

    def tile_start(d, tt):
        return pl.multiple_of((tt if d == 0 else n_tiles - 1 - tt) * _RWKV_UNROLL, _RWKV_UNROLL)

    def first_kk(c, tt):
        d, b, (sh_ref, _) = chain_refs(c)
        kk8 = sh_ref[0, b, pl.ds(tile_start(d, tt), _RWKV_UNROLL), :]
        r0 = 0 if d == 0 else _RWKV_UNROLL - 1
        return kk8[r0:r0 + 1, :]

    sa_ref[...] = seg_sum(jnp.concatenate([s_ref[c] * first_kk(c, 0) for c in range(n_ch)], axis=0))

    def tile_body(tt, carry):
        rows, t8s, dirs = [], [], []
        for c in range(n_ch):
            d, b, (sh_ref, dr_ref) = chain_refs(c)
            t8 = tile_start(d, tt)
            rows.append([sh_ref[q, b, pl.ds(t8, _RWKV_UNROLL), :] for q in range(3)]
                        + [dr_ref[q, b, pl.ds(t8, _RWKV_UNROLL), :] for q in range(3)])
            t8s.append(t8)
            dirs.append(d)
        kk_next_tile = [first_kk(c, jnp.minimum(tt + 1, n_tiles - 1)) for c in range(n_ch)]
        vdiag = jnp.concatenate([jnp.where(eye, rows[c][2][s:s + 1, :], 0.0)
                                 for c in range(n_ch) for s in range(_RWKV_UNROLL)], axis=0)
        vexp = seg_sum(vdiag)
        S = [s_ref[c] for c in range(n_ch)]
        sa = sa_ref[...]
        ys = [[None] * _RWKV_UNROLL for _ in range(n_ch)]
        for step in range(_RWKV_UNROLL):
            row = [step if d == 0 else _RWKV_UNROLL - 1 - step for d in dirs]

            def r_(c, q, rw=None):
                rw = row[c] if rw is None else rw
                return rows[c][q][rw:rw + 1, :]

            for c in range(n_ch):
                v0 = (c * _RWKV_UNROLL + row[c]) * HEAD_DIM
                S[c] = (S[c] * r_(c, 3) - sa[c * HEAD_DIM:(c + 1) * HEAD_DIM, :] * r_(c, 4)
                        + vexp[v0:v0 + HEAD_DIM, :] * r_(c, 5))
            if step + 1 < _RWKV_UNROLL:
                kk_next = [r_(c, 0, row[c] + (1 if dirs[c] == 0 else -1)) for c in range(n_ch)]
            else:
                kk_next = kk_next_tile
            sa = seg_sum(jnp.concatenate([S[c] * kk_next[c] for c in range(n_ch)], axis=0))
            for c in range(n_ch):
                ys[c][row[c]] = S[c] * r_(c, 1)
        ye = seg_sum(jnp.concatenate([ys[c][s] for c in range(n_ch) for s in range(_RWKV_UNROLL)], axis=0))
        for c in range(n_ch):
            for s in range(_RWKV_UNROLL):
                v0 = (c * _RWKV_UNROLL + s) * HEAD_DIM
                ys[c][s] = jnp.sum(jnp.where(eye, ye[v0:v0 + HEAD_DIM, :], 0.0), axis=0, keepdims=True)
        sa_ref[...] = sa
        for c in range(n_ch):
            s_ref[c] = S[c]
            y_ref = y0_ref if dirs[c] == 0 else y1_ref
            y_ref[c % n_batch, pl.ds(t8s[c], _RWKV_UNROLL), :] = jnp.concatenate(ys[c], axis=0)
        return carry

    lax.fori_loop(0, n_tiles, tile_body, 0)


def _rwkv_scan_pallas(shared, dir0, dir1, n_ctx, tb=256):
    _, B, T, C = shared.shape
    assert T % tb == 0 and n_ctx % tb == 0 and tb % _RWKV_UNROLL == 0 and C == GROUP_W
    nblk, ncb = T // tb, n_ctx // tb

    def fwd3(i):
        return (0, 0, i, 0)

    def bwd_blk(i):
        return jnp.where(i < ncb, ncb - 1 - i, nblk - 1 - (i - ncb))

    def bwd3(i):
        return (0, 0, bwd_blk(i), 0)

    kern = functools.partial(_rwkv_kernel, tb=tb, n_batch=B)
    blk = (3, B, tb, C)
    return pl.pallas_call(
        kern,
        grid=(nblk,),
        in_specs=[pl.BlockSpec(blk, fwd3), pl.BlockSpec(blk, bwd3), pl.BlockSpec(blk, fwd3), pl.BlockSpec(blk, bwd3)],
        out_specs=[pl.BlockSpec((B, tb, C), lambda i: (0, i, 0)),
                   pl.BlockSpec((B, tb, C), lambda i: (0, bwd_blk(i), 0))],
        out_shape=[jax.ShapeDtypeStruct((B, T, C), F32)] * 2,
        scratch_shapes=[pltpu.VMEM((2 * B, HEAD_DIM, C), F32), pltpu.VMEM((2 * B * HEAD_DIM, C), F32)],
        compiler_params=pltpu.CompilerParams(
            dimension_semantics=("arbitrary",), vmem_limit_bytes=VMEM_LIMIT_BYTES),
        name="rwkv_scan",
    )(shared, shared, dir0, dir1)


def rms_norm(x, g):
    xf = x.astype(F32)
    y = xf * lax.rsqrt(jnp.mean(xf * xf, axis=-1, keepdims=True) + EPS)
    return (y * g.astype(F32)).astype(x.dtype)


def heads(t):
    return t.reshape(t.shape[:-1] + (N_HEADS, HEAD_DIM))


def head_norm_merge(y, g):
    return rms_norm(y, g).reshape(y.shape[:-2] + (GROUP_W,))


def rope_2d(x, row, col):
    quarter = HEAD_DIM // 4
    inv = ROPE_BASE ** (-jnp.arange(quarter, dtype=F32) / quarter)
    xf = x.astype(F32)
    extra = (1,) * (x.ndim - 3)

    def rot(xa, pos):
        ang = pos.astype(F32)[:, None] * inv[None, :]
        ang = ang.reshape((1, ang.shape[0]) + extra + (quarter,))
        cos, sin = jnp.cos(ang), jnp.sin(ang)
        x1, x2 = xa[..., :quarter], xa[..., quarter:]
        return jnp.concatenate([x1 * cos - x2 * sin, x2 * cos + x1 * sin], axis=-1)

    half = HEAD_DIM // 2
    return jnp.concatenate([rot(xf[..., :half], row), rot(xf[..., half:], col)], axis=-1).astype(x.dtype)


def conv_mixer(hx, b_gate, c_gate, w, g):
    u = c_gate * hx
    up = jnp.pad(u, ((0, 0), (1, 1), (0, 0)))
    y = b_gate * (w[0] * up[:, :-2] + w[1] * up[:, 1:-1] + w[2] * up[:, 2:])
    return head_norm_merge(heads(y), g)


def rwkv_mixer(lat, ctx, w0, w2, a0, a2, g2, k_k, k_a, r_k, ln_g, need_ctx):
    n_ctx = ctx[0].shape[1]
    r, k, v, xw, xa, xg = (jnp.concatenate([c_, l_], axis=1) for c_, l_ in zip(ctx, lat))
    kk = heads(k * k_k)
    kk = (kk * lax.rsqrt(jnp.sum(kk * kk, axis=-1, keepdims=True) + EPS)).reshape(k.shape)
    g = jax.nn.sigmoid(xg) @ g2
    dirs = []
    for d in range(2):
        decay = jnp.exp(-RWKV_DECAY_SCALE * jax.nn.sigmoid(w0[d] + jnp.tanh(xw) @ w2[d]))
        a = jax.nn.sigmoid(a0[d] + xa @ a2[d])
        dirs.append(jnp.stack([decay, kk * a, k * (1 + (a - 1) * k_a)]))
    y0, y1 = _rwkv_scan_pallas(jnp.stack([kk, r, v]), dirs[0], dirs[1], n_ctx)
    y = rms_norm(heads(y0 + y1), ln_g)
    rh = heads(r)
    bonus = (jnp.sum(rh * heads(dirs[0][2]) * r_k, axis=-1, keepdims=True)
             + jnp.sum(rh * heads(dirs[1][2]) * r_k, axis=-1, keepdims=True)) * heads(v)
    out = (y + bonus).reshape(r.shape) * g
    return out[:, n_ctx:], (out[:, :n_ctx] if need_ctx else None)


def attn_project(q, k, v, q_g, k_g):
    B, T, _ = q.shape
    q = rms_norm(q.reshape(B, T, KV_HEADS, Q_PER_KV, HEAD_DIM), q_g)
    k = rms_norm(k.reshape(B, T, KV_HEADS, HEAD_DIM), k_g)
    v = v.reshape(B, T, KV_HEADS, HEAD_DIM)
    return q, k, v


def latent_attention(q, k, v, kc, vc, sink):
    B, T = q.shape[:2]
    nb = T // ATT_BLOCK
    qb = q.reshape(B, nb, ATT_BLOCK, KV_HEADS, Q_PER_KV, HEAD_DIM)

    def band(t):
        tp = jnp.pad(t, ((0, 0), (ATT_BLOCK, ATT_BLOCK), (0, 0), (0, 0)))
        tp = tp.reshape(B, nb + 2, ATT_BLOCK, KV_HEADS, HEAD_DIM)
        return jnp.concatenate([tp[:, :-2], tp[:, 1:-1], tp[:, 2:]], axis=2)

    kw, vw = band(k), band(v)
    start = jnp.arange(nb)[:, None] * ATT_BLOCK
    qpos = start + jnp.arange(ATT_BLOCK)[None, :]
    kpos = start - ATT_BLOCK + jnp.arange(3 * ATT_BLOCK)[None, :]
    mask = ((jnp.abs(qpos[:, :, None] - kpos[:, None, :]) <= WINDOW)
            & (kpos[:, None, :] >= 0) & (kpos[:, None, :] < T))
    s_loc = jnp.einsum('bnqhgd,bnkhd->bhgnqk', qb, kw).astype(F32) * ATT_SCALE
    s_loc = jnp.where(mask, s_loc, -jnp.inf)
    s_ctx = jnp.einsum('bnqhgd,bchd->bhgnqc', qb, kc).astype(F32) * ATT_SCALE
    sk = jnp.broadcast_to(sink.astype(F32).reshape(KV_HEADS, Q_PER_KV, 1, 1, 1), s_loc.shape[:-1] + (1,))
    p = jax.nn.softmax(jnp.concatenate([s_loc, s_ctx, sk], axis=-1), axis=-1).astype(v.dtype)
    nw = 3 * ATT_BLOCK
    o = (jnp.einsum('bhgnqk,bnkhd->bnqhgd', p[..., :nw], vw)
         + jnp.einsum('bhgnqc,bchd->bnqhgd', p[..., nw:-1], vc))
    return o.reshape(B, T, N_HEADS, HEAD_DIM)


def ctx_attention(qc, kc, vc, sink):
    s = jnp.einsum('bqhgd,bchd->bhgqc', qc, kc).astype(F32) * ATT_SCALE
    sk = jnp.broadcast_to(sink.astype(F32).reshape(KV_HEADS, Q_PER_KV, 1, 1), s.shape[:-1] + (1,))
    p = jax.nn.softmax(jnp.concatenate([s, sk], axis=-1), axis=-1)[..., :-1].astype(vc.dtype)
    o = jnp.einsum('bhgqc,bchd->bqhgd', p, vc)
    return o.reshape(qc.shape[0], qc.shape[1], N_HEADS, HEAD_DIM)


def mlstm_chunk_scan(state, q, k, v, logi, logf):
    B, H, T, Dh = q.shape
    L = MLSTM_CHUNK
    nc = T // L

    def chunks(t):
        return jnp.moveaxis(t.reshape((B, H, nc, L) + t.shape[3:]), 2, 0)

    xs = (chunks(q), chunks(k), chunks(v), chunks(logi), chunks(logf))
    causal = jnp.tril(jnp.ones((L, L), dtype=bool))

    def step(carry, inp):
        C, n, m = carry
        qc, kc, vc, li, lf = inp
        b = jnp.cumsum(lf, axis=-1)
        d_intra = jnp.where(causal, b[..., :, None] - b[..., None, :] + li[..., None, :], -jnp.inf)
        d_inter = b + m[..., None]
        m_t = jnp.maximum(jnp.max(d_intra, axis=-1), d_inter)
        w_intra = jnp.exp(d_intra - m_t[..., None])
        w_inter = jnp.exp(d_inter - m_t)
        s = jnp.einsum('bhtd,bhsd->bhts', qc, kc) * w_intra
        num = (jnp.einsum('bhts,bhsd->bhtd', s, vc)
               + w_inter[..., None] * jnp.einsum('bhvk,bhtk->bhtv', C, qc))
        den = jnp.sum(s, axis=-1) + w_inter * jnp.einsum('bhk,bhtk->bht', n, qc)
        h = num / jnp.maximum(jnp.abs(den), jnp.exp(-m_t))[..., None]
        b_end = b[..., -1]
        d_end = b_end[..., None] - b + li
        m_new = jnp.maximum(b_end + m, jnp.max(d_end, axis=-1))
        w_end = jnp.exp(d_end - m_new[..., None])
        carry_decay = jnp.exp(b_end + m - m_new)
        C = carry_decay[..., None, None] * C + jnp.einsum('bhs,bhsv,bhsk->bhvk', w_end, vc, kc)
        n = carry_decay[..., None] * n + jnp.einsum('bhs,bhsk->bhk', w_end, kc)
        return (C, n, m_new), h

    state, h = lax.scan(step, state, xs)
    return state, jnp.moveaxis(h, 0, 2).reshape(B, H, T, Dh)


def mlstm_stream(q, k, v, o, gates, i_b, f_b):
    B, T, _ = q.shape

    def th(t):
        return jnp.moveaxis(heads(t.astype(F32)), 2, 1)

    gates = gates.astype(F32).reshape(B, T, 2, 2, N_HEADS) + jnp.stack([i_b, f_b], axis=1).astype(F32)
    gates = jnp.moveaxis(gates, 1, -1)
    logi = gates[:, :, 0]
    logf = jax.nn.log_sigmoid(gates[:, :, 1])
    return th(q), th(k) * (HEAD_DIM ** -0.5), th(v), jax.nn.sigmoid(o), logi, logf


def mlstm_mixer(lat, ctx, i_b, f_b, out_g, need_ctx):
    sc = mlstm_stream(*ctx, i_b, f_b)
    sl = mlstm_stream(*lat, i_b, f_b)
    B = sl[0].shape[0]
    zero = (jnp.zeros((B, N_HEADS, HEAD_DIM, HEAD_DIM), F32),
            jnp.zeros((B, N_HEADS, HEAD_DIM), F32),
            jnp.zeros((B, N_HEADS), F32))

    def flip(t):
        return jnp.flip(t, axis=2)

    def run(state, s, d):
        q, k, v, _, logi, logf = s
        li, lf = logi[:, d], logf[:, d]
        if d == 0:
            return mlstm_chunk_scan(state, q, k, v, li, lf)
        st, h = mlstm_chunk_scan(state, flip(q), flip(k), flip(v), flip(li), flip(lf))
        return st, flip(h)

    h_c, h_l = [], []
    for d in range(2):
        st, hc = run(zero, sc, d)
        _, hl = run(st, sl, d)
        h_c.append(hc)
        h_l.append(hl)

    def finish(s, hs):
        h = jnp.moveaxis(hs[0] + hs[1], 1, 2)
        return s[3] * head_norm_merge(h, out_g)

    y_lat = finish(sl, h_l)
    y_ctx = finish(sc, h_c) if need_ctx else None
    return y_lat, y_ctx


def kernel(x, c, ctx, c_ctx, ada_w, ada_b, norm1_g, norm2_g, w_in, w_out, conv_w, conv_g,
           rwkv_w0, rwkv_w2, rwkv_a0, rwkv_a2, rwkv_g2, rwkv_kk, rwkv_ka, rwkv_rk, rwkv_ln_g,
           att_q_g, att_k_g, att_sink, att_out_g, ml_i_b, ml_f_b, ml_out_g,
           peer_wq, peer_keys, peer_u, peer_v):
    B, T, D = x.shape
    ROWS = T // GRID_W
    row = jnp.repeat(jnp.arange(ROWS), GRID_W)
    col = jnp.arange(ROWS * GRID_W) % GRID_W
    for l in range(DEPTH):
        need_ctx = l < DEPTH - 1
        mod = jax.nn.silu(c) @ ada_w[l] + ada_b[l]
        mod_c = jax.nn.silu(c_ctx) @ ada_w[l] + ada_b[l]
        sh1, sc1, gt1, sh2, sc2, gt2 = jnp.split(mod[:, None, :], 6, axis=-1)
        csh1, csc1, cgt1, csh2, csc2, cgt2 = jnp.split(mod_c, 6, axis=-1)

        h = rms_norm(x, norm1_g[l]) * (1 + sc1) + sh1
        hc = rms_norm(ctx, norm1_g[l]) * (1 + csc1) + csh1
        P = jnp.split(_mm3(h, w_in[l]), IN_OFFSETS, axis=-1)
        Pc = jnp.split(_mm3(hc, w_in[l]), IN_OFFSETS, axis=-1)

        y_a = conv_mixer(P[0], P[1], P[2], conv_w[l], conv_g[l])
        y_b, yc_b = rwkv_mixer(P[3:9], Pc[3:9], rwkv_w0[l], rwkv_w2[l], rwkv_a0[l], rwkv_a2[l],
                               rwkv_g2[l], rwkv_kk[l], rwkv_ka[l], rwkv_rk[l], rwkv_ln_g[l], need_ctx)
        q, k, v = attn_project(P[9], P[10], P[11], att_q_g[l], att_k_g[l])
        q, k = rope_2d(q, row, col), rope_2d(k, row, col)
        qc, kc, vc = attn_project(Pc[9], Pc[10], Pc[11], att_q_g[l], att_k_g[l])
        y_c = head_norm_merge(latent_attention(q, k, v, kc, vc, att_sink[l]), att_out_g[l])
        y_d, yc_d = mlstm_mixer(P[12:17], Pc[12:17], ml_i_b[l], ml_f_b[l], ml_out_g[l], need_ctx)

        y = _mm3(jnp.concatenate([t.astype(x.dtype) for t in (y_a, y_b, y_c, y_d)], axis=-1), w_out[l])
        x = x + gt1 * y
        h2 = rms_norm(x, norm2_g[l]) * (1 + sc2) + sh2
        tok = [h2.reshape(B * T, D)]
        if need_ctx:
            yc_a = conv_mixer(Pc[0], Pc[1], Pc[2], conv_w[l], conv_g[l])
            yc_c = head_norm_merge(ctx_attention(qc, kc, vc, att_sink[l]), att_out_g[l])
            yc = _mm3(jnp.concatenate([t.astype(ctx.dtype) for t in (yc_a, yc_b, yc_c, yc_d)], axis=-1), w_out[l])
            ctx = ctx + cgt1 * yc
            hc2 = rms_norm(ctx, norm2_g[l]) * (1 + csc2) + csh2
            tok.append(hc2.reshape(-1, D))
        peer_t = _peer_dense(jnp.concatenate(tok, axis=0) if need_ctx else tok[0],
                             peer_wq[l].astype(BF16),
                             peer_keys[l].reshape(2 * PEER_HEADS, N_KEYS, PEER_HALF).astype(BF16),
                             peer_u[l].astype(BF16), peer_v[l].T.astype(BF16))
        x = x + gt2 * peer_t[:, :B * T].T.reshape(B, T, D)
        if need_ctx:
            ctx = ctx + cgt2 * peer_t[:, B * T:].T.reshape(ctx.shape)
    return x
```

```python
import functools
import math

import jax
import jax.numpy as jnp
import numpy as np
from jax import lax
from jax.experimental import pallas as pl
from jax.experimental.pallas import tpu as pltpu

D_MODEL = 1024
DEPTH = 2
GRID_W = 64
N_MIXERS = 4
GROUP_W = D_MODEL // N_MIXERS
HEAD_DIM = 64
N_HEADS = GROUP_W // HEAD_DIM
CONV_K = 3
W_LORA = 16
A_LORA = 16
G_LORA = 32
RWKV_DECAY_SCALE = math.exp(-0.5)
KV_HEADS = 2
Q_PER_KV = N_HEADS // KV_HEADS
KV_W = KV_HEADS * HEAD_DIM
WINDOW = 128
ATT_BLOCK = 128
ATT_SCALE = HEAD_DIM ** -0.5
ROPE_BASE = 10000.0
MLSTM_CHUNK = 128
N_GATE_COLS = 2 * 2 * N_HEADS
PEER_HEADS = 8
N_KEYS = 128
N_EXPERTS = N_KEYS * N_KEYS
PEER_TOPK = 16
PEER_QDIM = 256
PEER_HALF = PEER_QDIM // 2
PEER_BLOCK = 128
EPS = 1e-6
F32 = jnp.float32
BF16 = jnp.bfloat16
IN_SIZES = (GROUP_W, GROUP_W, GROUP_W,
            GROUP_W, GROUP_W, GROUP_W, W_LORA, A_LORA, G_LORA,
            GROUP_W, KV_W, KV_W,
            GROUP_W, GROUP_W, GROUP_W, GROUP_W, N_GATE_COLS)
D_IN = sum(IN_SIZES)
IN_OFFSETS = tuple(int(o) for o in np.cumsum(IN_SIZES)[:-1])

LANE = 128
VMEM_LIMIT_BYTES = 56 * 1024 * 1024


def _mm_kernel(x_ref, w_ref, o_ref):
    o_ref[...] = jnp.dot(x_ref[...].astype(BF16), w_ref[...], preferred_element_type=F32)


def _matmul(x, w, tm=512):
    M, K = x.shape
    N = w.shape[1]
    Np = -(-N // LANE) * LANE
    wb = w.astype(BF16)
    if Np != N:
        wb = jnp.pad(wb, ((0, 0), (0, Np - N)))
    tm = min(tm, M)
    assert M % tm == 0
    out = pl.pallas_call(
        _mm_kernel,
        grid=(M // tm,),
        in_specs=[pl.BlockSpec((tm, K), lambda i: (i, 0)),
                  pl.BlockSpec((K, Np), lambda i: (0, 0))],
        out_specs=pl.BlockSpec((tm, Np), lambda i: (i, 0)),
        out_shape=jax.ShapeDtypeStruct((M, Np), F32),
        compiler_params=pltpu.CompilerParams(
            dimension_semantics=("arbitrary",), vmem_limit_bytes=VMEM_LIMIT_BYTES),
        name="matmul",
    )(x, wb)
    return out[:, :N] if Np != N else out


def _mm3(x, w):
    lead = x.shape[:-1]
    return _matmul(x.reshape(-1, x.shape[-1]), w).reshape(lead + (w.shape[1],))


_PEER_CAND_ROWS = 80
_NEG_INF = float("-inf")


def _gelu_tanh(x):
    c = math.sqrt(2.0 / math.pi)
    return 0.5 * x * (1.0 + jnp.tanh(c * (x + 0.044715 * (x * x * x))))


def _top16_sorted(x, n_rows):
    iota = lax.broadcasted_iota(jnp.int32, x.shape, 0)
    vals = []
    for _ in range(PEER_TOPK):
        mx = jnp.max(x, axis=0, keepdims=True)
        vals.append(mx)
        pos = jnp.min(jnp.where(x == mx, iota, n_rows), axis=0, keepdims=True)
        x = jnp.where(iota == pos, _NEG_INF, x)
    return vals


def _peer_candidates(sv1, sv2):
    row8 = lax.broadcasted_iota(jnp.int32, (8, sv1.shape[1]), 0)
    blocks = [sv1[0:1, :] + sv2[0:8, :], sv1[0:1, :] + sv2[8:16, :]]
    for a in range(1, 8):
        n_valid = PEER_TOPK // (a + 1)
        blk = sv1[a:a + 1, :] + sv2[0:8, :]
        blocks.append(blk if n_valid >= 8 else jnp.where(row8 < n_valid, blk, _NEG_INF))
    blocks.append(sv1[8:16, :] + sv2[0:1, :])
    return jnp.concatenate(blocks, axis=0)


def _peer_kernel(h_ref, wq_ref, keys_ref, u_ref, vt_ref, o_ref,
                 hb_ref, s_ref, sv_ref, e1_ref, e2_ref, thr_ref, act_ref, wg_ref, *, tm, te):
    j = pl.program_id(1)
    n_lg = tm // LANE
    n_pair = n_lg // 2
    n_ib = te // N_KEYS

    @pl.when(j == 0)
    def _prepare():
        hb = h_ref[...].astype(BF16)
        hb_ref[...] = hb
        qb = jnp.dot(hb, wq_ref[...], preferred_element_type=F32).astype(BF16)
        for hp in range(2 * PEER_HEADS):
            s_ref[hp] = lax.dot_general(keys_ref[hp], qb[:, hp * PEER_HALF:(hp + 1) * PEER_HALF],
                                        (((1,), (1,)), ((), ())), preferred_element_type=F32)

        def top_body(it, carry):
            hp = it // n_pair
            for half in range(2):
                off = pl.multiple_of(((it % n_pair) * 2 + half) * LANE, LANE)
                vals = _top16_sorted(s_ref[hp, :, pl.ds(off, LANE)], N_KEYS)
                sv_ref[hp, :, pl.ds(off, LANE)] = jnp.concatenate(vals, axis=0)
            return carry

        lax.fori_loop(0, 2 * PEER_HEADS * n_pair, top_body, 0)

        def head_body(it, carry):
            h = it // n_pair
            for half in range(2):
                off = pl.multiple_of(((it % n_pair) * 2 + half) * LANE, LANE)
                sv1 = sv_ref[2 * h, :, pl.ds(off, LANE)]
                sv2 = sv_ref[2 * h + 1, :, pl.ds(off, LANE)]
                tv = _top16_sorted(_peer_candidates(sv1, sv2), _PEER_CAND_ROWS)
                z = jnp.zeros_like(tv[0])
                for t in tv:
                    z = z + jnp.exp(t - tv[0])
                thr_ref[h, :, pl.ds(off, LANE)] = jnp.broadcast_to(tv[PEER_TOPK - 1], (8, LANE))
                e1_ref[h, :, pl.ds(off, LANE)] = jnp.exp(s_ref[2 * h, :, pl.ds(off, LANE)] - sv1[0:1, :])
                e2_ref[h, :, pl.ds(off, LANE)] = jnp.exp(s_ref[2 * h + 1, :, pl.ds(off, LANE)] - sv2[0:1, :]) / z
            return carry

        lax.fori_loop(0, PEER_HEADS * n_pair, head_body, 0)
        o_ref[...] = jnp.zeros_like(o_ref)

    act_ref[...] = lax.dot_general(u_ref[...], hb_ref[...], (((1,), (1,)), ((), ())),
                                   preferred_element_type=F32)

    i0 = pl.multiple_of(j * n_ib, 8)

    def lg_body(lg, carry):
        ls = pl.ds(pl.multiple_of(lg * LANE, LANE), LANE)
        s1t = [s_ref[2 * h, pl.ds(i0, 8), ls] for h in range(PEER_HEADS)]
        e1t = [e1_ref[h, pl.ds(i0, 8), ls] for h in range(PEER_HEADS)]
        for ib in range(n_ib):
            w = jnp.zeros((N_KEYS, LANE), F32)
            for h in range(PEER_HEADS):
                sel = (s_ref[2 * h + 1, :, ls] + s1t[h][ib:ib + 1, :]) >= thr_ref[h, 0:1, ls]
                w = w + jnp.where(sel, e2_ref[h, :, ls] * e1t[h][ib:ib + 1, :], 0.0)
            rs = slice(ib * N_KEYS, (ib + 1) * N_KEYS)
            wg_ref[rs, ls] = (w * _gelu_tanh(act_ref[rs, ls])).astype(BF16)
        return carry

    lax.fori_loop(0, n_lg, lg_body, 0)
    o_ref[...] += jnp.dot(vt_ref[...], wg_ref[...], preferred_element_type=F32)


def _peer_dense(hf, wq_b, keys_b, u_b, vt_b, tm=512, te=1024):
    M, D = hf.shape
    E = u_b.shape[0]
    tm = min(tm, M)
    assert M % tm == 0 and E % te == 0 and tm % (2 * LANE) == 0 and te == 8 * N_KEYS
    kern = functools.partial(_peer_kernel, tm=tm, te=te)
    return pl.pallas_call(
        kern,
        grid=(M // tm, E // te),
        in_specs=[pl.BlockSpec((tm, D), lambda i, j: (i, 0)),
                  pl.BlockSpec(wq_b.shape, lambda i, j: (0, 0)),
                  pl.BlockSpec(keys_b.shape, lambda i, j: (0, 0, 0)),
                  pl.BlockSpec((te, D), lambda i, j: (j, 0)),
                  pl.BlockSpec((D, te), lambda i, j: (0, j))],
        out_specs=pl.BlockSpec((D, tm), lambda i, j: (0, i)),
        out_shape=jax.ShapeDtypeStruct((D, M), F32),
        scratch_shapes=[pltpu.VMEM((tm, D), BF16),
                        pltpu.VMEM((2 * PEER_HEADS, N_KEYS, tm), F32),
                        pltpu.VMEM((2 * PEER_HEADS, PEER_TOPK, tm), F32),
                        pltpu.VMEM((PEER_HEADS, N_KEYS, tm), F32),
                        pltpu.VMEM((PEER_HEADS, N_KEYS, tm), F32),
                        pltpu.VMEM((PEER_HEADS, 8, tm), F32),
                        pltpu.VMEM((te, tm), F32),
                        pltpu.VMEM((te, tm), BF16)],
        compiler_params=pltpu.CompilerParams(
            dimension_semantics=("arbitrary", "arbitrary"), vmem_limit_bytes=VMEM_LIMIT_BYTES),
        name="peer_dense",
    )(hf, wq_b, keys_b, u_b, vt_b)


_RWKV_UNROLL = 8


def _rwkv_kernel(sh_f_ref, sh_b_ref, d0_ref, d1_ref, y0_ref, y1_ref, s_ref, sa_ref, *, tb, n_batch):
    i = pl.program_id(0)
    n_ch = 2 * n_batch
    n_tiles = tb // _RWKV_UNROLL

    @pl.when(i == 0)
    def _init():
        s_ref[...] = jnp.zeros_like(s_ref)

    lane = lax.broadcasted_iota(jnp.int32, (GROUP_W, GROUP_W), 1)
    sub = lax.broadcasted_iota(jnp.int32, (GROUP_W, GROUP_W), 0)
    seg_ones = jnp.where(lane // HEAD_DIM == sub // HEAD_DIM, 1.0, 0.0).astype(BF16)
    lane_v = lax.broadcasted_iota(jnp.int32, (HEAD_DIM, GROUP_W), 1)
    sub_v = lax.broadcasted_iota(jnp.int32, (HEAD_DIM, GROUP_W), 0)
    eye = (lane_v % HEAD_DIM == sub_v)

    def seg_sum(p):
        return jnp.dot(p.astype(BF16), seg_ones, preferred_element_type=F32)

    def chain_refs(c):
        d, b = divmod(c, n_batch)
        return d, b, (sh_f_ref, d0_ref) if d == 0 else (sh_b_ref, d1_ref)

    def tile_start(d, tt):
        return pl.multiple_of((tt if d == 0 else n_tiles - 1 - tt) * _RWKV_UNROLL, _RWKV_UNROLL)

    def first_kk(c, tt):
        d, b, (sh_ref, _) = chain_refs(c)
        kk8 = sh_ref[0, b, pl.ds(tile_start(d, tt), _RWKV_UNROLL), :]
        r0 = 0 if d == 0 else _RWKV_UNROLL - 1
        return kk8[r0:r0 + 1, :]

    sa_ref[...] = seg_sum(jnp.concatenate([s_ref[c] * first_kk(c, 0) for c in range(n_ch)], axis=0))

    def tile_body(tt, carry):
        rows, t8s, dirs = [], [], []
        for c in range(n_ch):
            d, b, (sh_ref, dr_ref) = chain_refs(c)
            t8 = tile_start(d, tt)
            rows.append([sh_ref[q, b, pl.ds(t8, _RWKV_UNROLL), :] for q in range(3)]
                        + [dr_ref[q, b, pl.ds(t8, _RWKV_UNROLL), :] for q in range(3)])
            t8s.append(t8)
            dirs.append(d)
        kk_next_tile = [first_kk(c, jnp.minimum(tt + 1, n_tiles - 1)) for c in range(n_ch)]
        vdiag = jnp.concatenate([jnp.where(eye, rows[c][2][s:s + 1, :], 0.0)
                                 for c in range(n_ch) for s in range(_RWKV_UNROLL)], axis=0)
        vexp = seg_sum(vdiag)
        S = [s_ref[c] for c in range(n_ch)]
        sa = sa_ref[...]
        ys = [[None] * _RWKV_UNROLL for _ in range(n_ch)]
        for step in range(_RWKV_UNROLL):
            row = [step if d == 0 else _RWKV_UNROLL - 1 - step for d in dirs]

            def r_(c, q, rw=None):
                rw = row[c] if rw is None else rw
                return rows[c][q][rw:rw + 1, :]

            for c in range(n_ch):
                v0 = (c * _RWKV_UNROLL + row[c]) * HEAD_DIM
                S[c] = (S[c] * r_(c, 3) - sa[c * HEAD_DIM:(c + 1) * HEAD_DIM, :] * r_(c, 4)
                        + vexp[v0:v0 + HEAD_DIM, :] * r_(c, 5))
            if step + 1 < _RWKV_UNROLL:
                kk_next = [r_(c, 0, row[c] + (1 if dirs[c] == 0 else -1)) for c in range(n_ch)]
            else:
                kk_next = kk_next_tile
            sa = seg_sum(jnp.concatenate([S[c] * kk_next[c] for c in range(n_ch)], axis=0))
            for c in range(n_ch):
                ys[c][row[c]] = S[c] * r_(c, 1)
        ye = seg_sum(jnp.concatenate([ys[c][s] for c in range(n_ch) for s in range(_RWKV_UNROLL)], axis=0))
        for c in range(n_ch):
            for s in range(_RWKV_UNROLL):
                v0 = (c * _RWKV_UNROLL + s) * HEAD_DIM
                ys[c][s] = jnp.sum(jnp.where(eye, ye[v0:v0 + HEAD_DIM, :], 0.0), axis=0, keepdims=True)
        sa_ref[...] = sa
        for c in range(n_ch):
            s_ref[c] = S[c]
            y_ref = y0_ref if dirs[c] == 0 else y1_ref
            y_ref[c % n_batch, pl.ds(t8s[c], _RWKV_UNROLL), :] = jnp.concatenate(ys[c], axis=0)
        return carry

    lax.fori_loop(0, n_tiles, tile_body, 0)


def _rwkv_scan_pallas(shared, dir0, dir1, n_ctx, tb=256):
    _, B, T, C = shared.shape
    assert T % tb == 0 and n_ctx % tb == 0 and tb % _RWKV_UNROLL == 0 and C == GROUP_W
    nblk, ncb = T // tb, n_ctx // tb

    def fwd3(i):
        return (0, 0, i, 0)

    def bwd_blk(i):
        return jnp.where(i < ncb, ncb - 1 - i, nblk - 1 - (i - ncb))

    def bwd3(i):
        return (0, 0, bwd_blk(i), 0)

    kern = functools.partial(_rwkv_kernel, tb=tb, n_batch=B)
    blk = (3, B, tb, C)
    return pl.pallas_call(
        kern,
        grid=(nblk,),
        in_specs=[pl.BlockSpec(blk, fwd3), pl.BlockSpec(blk, bwd3), pl.BlockSpec(blk, fwd3), pl.BlockSpec(blk, bwd3)],
        out_specs=[pl.BlockSpec((B, tb, C), lambda i: (0, i, 0)),
                   pl.BlockSpec((B, tb, C), lambda i: (0, bwd_blk(i), 0))],
        out_shape=[jax.ShapeDtypeStruct((B, T, C), F32)] * 2,
        scratch_shapes=[pltpu.VMEM((2 * B, HEAD_DIM, C), F32), pltpu.VMEM((2 * B * HEAD_DIM, C), F32)],
        compiler_params=pltpu.CompilerParams(
            dimension_semantics=("arbitrary",), vmem_limit_bytes=VMEM_LIMIT_BYTES),
        name="rwkv_scan",
    )(shared, shared, dir0, dir1)


def _mlstm_kernel(q_ref, k_ref, v_ref, row_ref, col_ref, h_ref, c_ref, n_ref, m_ref):
    g = pl.program_id(0)
    L = q_ref.shape[2]

    @pl.when(pl.program_id(1) == 0)
    def _init():
        c_ref[...] = jnp.zeros_like(c_ref)
        n_ref[...] = jnp.zeros_like(n_ref)
        m_ref[...] = jnp.zeros_like(m_ref)

    backward = (g % 2) == 1
    sgn = 1 - 2 * (g % 2)
    tt = lax.broadcasted_iota(jnp.int32, (L, L), 0)
    ss = lax.broadcasted_iota(jnp.int32, (L, L), 1)
    causal = (tt - ss) * sgn >= 0
    nt_dims = (((1,), (1,)), ((), ()))
    for h in range(N_HEADS):
        q = q_ref[0, h]
        k = k_ref[0, h]
        v = v_ref[0, h]
        qb, kb, vb = q.astype(BF16), k.astype(BF16), v.astype(BF16)
        brow = row_ref[0, h, 0, 0:1, :]
        lirow = row_ref[0, h, 0, 1:2, :]
        bcol = col_ref[0, h, 0, :, 0:1]
        licol = col_ref[0, h, 0, :, 1:2]
        m_prev = m_ref[h][:, 0:1]
        C = c_ref[h]
        n = n_ref[h]
        d_intra = jnp.where(causal, bcol - brow + lirow, _NEG_INF)
        d_inter = bcol + m_prev
        m_t = jnp.maximum(jnp.max(d_intra, axis=1, keepdims=True), d_inter)
        w_intra = jnp.exp(d_intra - m_t)
        w_inter = jnp.exp(d_inter - m_t)
        s = lax.dot_general(qb, kb, nt_dims, preferred_element_type=F32) * w_intra
        num = (jnp.dot(s.astype(BF16), vb, preferred_element_type=F32)
               + w_inter * lax.dot_general(qb, C.astype(BF16), nt_dims, preferred_element_type=F32))
        den = jnp.sum(s, axis=1, keepdims=True) + w_inter * jnp.sum(q * n, axis=1, keepdims=True)
        h_ref[0, h] = num / jnp.maximum(jnp.abs(den), jnp.exp(-m_t))
        b_end = jnp.where(backward, brow[:, 0:1], brow[:, L - 1:L])
        d_end = b_end - bcol + licol
        m_new = jnp.maximum(b_end + m_prev, jnp.max(d_end, axis=0, keepdims=True))
        w_end = jnp.exp(d_end - m_new)
        decay = jnp.exp(b_end + m_prev - m_new)
        wv_t = (w_end * v).T.astype(BF16)
        c_ref[h] = decay * C + jnp.dot(wv_t, kb, preferred_element_type=F32)
        n_ref[h] = decay * n + jnp.sum(w_end * k, axis=0, keepdims=True)
        m_ref[h] = jnp.broadcast_to(m_new, (1, LANE))


def _mlstm_scan_pallas(qh, kh, vh, rows, cols, n_ctx):
    B, H, T, Dh = qh.shape
    L = MLSTM_CHUNK
    nc, ncb = T // L, n_ctx // L
    assert T % L == 0 and n_ctx % L == 0 and H == N_HEADS

    def chunk(g, c):
        rev = jnp.where(c < ncb, ncb - 1 - c, nc - 1 - (c - ncb))
        return jnp.where(g % 2 == 0, c, rev)

    qkv_spec = pl.BlockSpec((1, H, L, Dh), lambda g, c: (g // 2, 0, chunk(g, c), 0))
    return pl.pallas_call(
        _mlstm_kernel,
        grid=(2 * B, nc),
        in_specs=[qkv_spec, qkv_spec, qkv_spec,
                  pl.BlockSpec((1, H, 1, 2, L), lambda g, c: (g, 0, chunk(g, c), 0, 0)),
                  pl.BlockSpec((1, H, 1, L, 2), lambda g, c: (g, 0, chunk(g, c), 0, 0))],
        out_specs=pl.BlockSpec((1, H, L, Dh), lambda g, c: (g, 0, chunk(g, c), 0)),
        out_shape=jax.ShapeDtypeStruct((2 * B, H, T, Dh), F32),
        scratch_shapes=[pltpu.VMEM((H, Dh, Dh), F32), pltpu.VMEM((H, 1, Dh), F32), pltpu.VMEM((H, 1, LANE), F32)],
        compiler_params=pltpu.CompilerParams(
            dimension_semantics=("arbitrary", "arbitrary"), vmem_limit_bytes=VMEM_LIMIT_BYTES),
        name="mlstm_scan",
    )(qh, kh, vh, rows, cols)


def _attn_kernel(*refs, n_band, t_total):
    q_ref = refs[0]
    band = refs[1:1 + 2 * n_band]
    kc_ref, vc_ref, sink_ref, o_ref = refs[1 + 2 * n_band:]
    n = pl.program_id(1)
    nt_dims = (((1,), (1,)), ((), ()))
    if n_band:
        qpos = n * ATT_BLOCK + lax.broadcasted_iota(jnp.int32, (ATT_BLOCK, n_band * ATT_BLOCK), 0)
        kpos = (n - 1) * ATT_BLOCK + lax.broadcasted_iota(jnp.int32, (ATT_BLOCK, n_band * ATT_BLOCK), 1)
        mask = (jnp.abs(qpos - kpos) <= WINDOW) & (kpos >= 0) & (kpos < t_total)
    for kvh in range(KV_HEADS):
        kc = kc_ref[0, kvh].astype(BF16)
        vc = vc_ref[0, kvh].astype(BF16)
        if n_band:
            kw = jnp.concatenate([band[j][0, kvh] for j in range(n_band)], axis=0).astype(BF16)
            vw = jnp.concatenate([band[n_band + j][0, kvh] for j in range(n_band)], axis=0).astype(BF16)
        for g in range(Q_PER_KV):
            h = kvh * Q_PER_KV + g
            q = q_ref[0, h].astype(BF16)
            sink = sink_ref[h][:, 0:1]
            s_ctx = lax.dot_general(q, kc, nt_dims, preferred_element_type=F32) * ATT_SCALE
            m = jnp.maximum(jnp.max(s_ctx, axis=1, keepdims=True), sink)
            if n_band:
                s_loc = lax.dot_general(q, kw, nt_dims, preferred_element_type=F32) * ATT_SCALE
                s_loc = jnp.where(mask, s_loc, _NEG_INF)
                m = jnp.maximum(m, jnp.max(s_loc, axis=1, keepdims=True))
            p_ctx = jnp.exp(s_ctx - m)
            den = jnp.sum(p_ctx, axis=1, keepdims=True) + jnp.exp(sink - m)
            o = jnp.dot(p_ctx.astype(BF16), vc, preferred_element_type=F32)
            if n_band:
                p_loc = jnp.exp(s_loc - m)
                den = den + jnp.sum(p_loc, axis=1, keepdims=True)
                o = o + jnp.dot(p_loc.astype(BF16), vw, preferred_element_type=F32)
            o_ref[0, h] = o / den


def _attention_pallas(q, k, v, kc, vc, sink):
    B, H, T, Dh = q.shape
    C = kc.shape[2]
    nb = T // ATT_BLOCK
    assert T % ATT_BLOCK == 0
    n_band = 0 if k is None else 3
    sink_b = jnp.broadcast_to(sink.astype(F32)[:, None, None], (H, 1, LANE))
    band_specs = [pl.BlockSpec((1, KV_HEADS, ATT_BLOCK, Dh),
                               lambda b, n, j=j: (b, 0, jnp.clip(n + j - 1, 0, nb - 1), 0)) for j in range(n_band)]
    ctx_spec = pl.BlockSpec((1, KV_HEADS, C, Dh), lambda b, n: (b, 0, 0, 0))
    kern = functools.partial(_attn_kernel, n_band=n_band, t_total=T)
    band_args = [] if k is None else [k] * 3 + [v] * 3
    return pl.pallas_call(
        kern,
        grid=(B, nb),
        in_specs=[pl.BlockSpec((1, H, ATT_BLOCK, Dh), lambda b, n: (b, 0, n, 0))] + band_specs * 2
                 + [ctx_spec, ctx_spec, pl.BlockSpec((H, 1, LANE), lambda b, n: (0, 0, 0))],
        out_specs=pl.BlockSpec((1, H, ATT_BLOCK, Dh), lambda b, n: (b, 0, n, 0)),
        out_shape=jax.ShapeDtypeStruct((B, H, T, Dh), F32),
        compiler_params=pltpu.CompilerParams(
            dimension_semantics=("arbitrary", "arbitrary"), vmem_limit_bytes=VMEM_LIMIT_BYTES),
        name="attention",
    )(q, *band_args, kc, vc, sink_b)


def rms_norm(x, g):
    xf = x.astype(F32)
    y = xf * lax.rsqrt(jnp.mean(xf * xf, axis=-1, keepdims=True) + EPS)
    return (y * g.astype(F32)).astype(x.dtype)


def heads(t):
    return t.reshape(t.shape[:-1] + (N_HEADS, HEAD_DIM))


def head_norm_merge(y, g):
    return rms_norm(y, g).reshape(y.shape[:-2] + (GROUP_W,))


def rope_2d(x, row, col):
    quarter = HEAD_DIM // 4
    inv = ROPE_BASE ** (-jnp.arange(quarter, dtype=F32) / quarter)
    xf = x.astype(F32)
    extra = (1,) * (x.ndim - 3)

    def rot(xa, pos):
        ang = pos.astype(F32)[:, None] * inv[None, :]
        ang = ang.reshape((1, ang.shape[0]) + extra + (quarter,))
        cos, sin = jnp.cos(ang), jnp.sin(ang)
        x1, x2 = xa[..., :quarter], xa[..., quarter:]
        return jnp.concatenate([x1 * cos - x2 * sin, x2 * cos + x1 * sin], axis=-1)

    half = HEAD_DIM // 2
    return jnp.concatenate([rot(xf[..., :half], row), rot(xf[..., half:], col)], axis=-1).astype(x.dtype)


def conv_mixer(hx, b_gate, c_gate, w, g):
    u = c_gate * hx
    up = jnp.pad(u, ((0, 0), (1, 1), (0, 0)))
    y = b_gate * (w[0] * up[:, :-2] + w[1] * up[:, 1:-1] + w[2] * up[:, 2:])
    return head_norm_merge(heads(y), g)


def rwkv_mixer(lat, ctx, w0, w2, a0, a2, g2, k_k, k_a, r_k, ln_g, need_ctx):
    n_ctx = ctx[0].shape[1]
    r, k, v, xw, xa, xg = (jnp.concatenate([c_, l_], axis=1) for c_, l_ in zip(ctx, lat))
    kk = heads(k * k_k)
    kk = (kk * lax.rsqrt(jnp.sum(kk * kk, axis=-1, keepdims=True) + EPS)).reshape(k.shape)
    g = jax.nn.sigmoid(xg) @ g2
    dirs = []
    for d in range(2):
        decay = jnp.exp(-RWKV_DECAY_SCALE * jax.nn.sigmoid(w0[d] + jnp.tanh(xw) @ w2[d]))
        a = jax.nn.sigmoid(a0[d] + xa @ a2[d])
        dirs.append(jnp.stack([decay, kk * a, k * (1 + (a - 1) * k_a)]))
    y0, y1 = _rwkv_scan_pallas(jnp.stack([kk, r, v]), dirs[0], dirs[1], n_ctx)
    y = rms_norm(heads(y0 + y1), ln_g)
    rh = heads(r)
    bonus = (jnp.sum(rh * heads(dirs[0][2]) * r_k, axis=-1, keepdims=True)
             + jnp.sum(rh * heads(dirs[1][2]) * r_k, axis=-1, keepdims=True)) * heads(v)
    out = (y + bonus).reshape(r.shape) * g
    return out[:, n_ctx:], (out[:, :n_ctx] if need_ctx else None)


def attn_project(q, k, v, q_g, k_g):
    B, T, _ = q.shape
    q = rms_norm(q.reshape(B, T, KV_HEADS, Q_PER_KV, HEAD_DIM), q_g)
    k = rms_norm(k.reshape(B, T, KV_HEADS, HEAD_DIM), k_g)
    v = v.reshape(B, T, KV_HEADS, HEAD_DIM)
    return q, k, v


def _head_major(t):
    B, T = t.shape[:2]
    return jnp.moveaxis(t.reshape(B, T, -1, HEAD_DIM), 2, 1)


def latent_attention(q, k, v, kc, vc, sink):
    o = _attention_pallas(_head_major(q), _head_major(k), _head_major(v), _head_major(kc), _head_major(vc), sink)
    return jnp.moveaxis(o, 1, 2)


def ctx_attention(qc, kc, vc, sink):
    o = _attention_pallas(_head_major(qc), None, None, _head_major(kc), _head_major(vc), sink)
    return jnp.moveaxis(o, 1, 2)


def mlstm_mixer(lat, ctx, i_b, f_b, out_g, need_ctx):
    n_ctx = ctx[0].shape[1]
    q, k, v, o, gates = (jnp.concatenate([c_, l_], axis=1) for c_, l_ in zip(ctx, lat))
    B, T, _ = q.shape
    L = MLSTM_CHUNK

    def th(t):
        return jnp.moveaxis(heads(t.astype(F32)), 2, 1)

    gates = gates.astype(F32).reshape(B, T, 2, 2, N_HEADS) + jnp.stack([i_b, f_b], axis=1).astype(F32)
    gates = jnp.moveaxis(gates, 1, -1)
    logi = gates[:, :, 0].reshape(B, 2, N_HEADS, T // L, L)
    logf = jax.nn.log_sigmoid(gates[:, :, 1]).reshape(B, 2, N_HEADS, T // L, L)
    bcum = jnp.stack([jnp.cumsum(logf[:, 0], axis=-1),
                      jnp.flip(jnp.cumsum(jnp.flip(logf[:, 1], axis=-1), axis=-1), axis=-1)], axis=1)
    rows = jnp.stack([bcum, logi], axis=-2).reshape(B * 2, N_HEADS, T // L, 2, L)
    cols = jnp.stack([bcum, logi], axis=-1).reshape(B * 2, N_HEADS, T // L, L, 2)
    h = _mlstm_scan_pallas(th(q), th(k) * (HEAD_DIM ** -0.5), th(v), rows, cols, n_ctx)
    h = h.reshape(B, 2, N_HEADS, T, HEAD_DIM)
    y = jax.nn.sigmoid(o) * head_norm_merge(jnp.moveaxis(h[:, 0] + h[:, 1], 1, 2), out_g)
    return y[:, n_ctx:], (y[:, :n_ctx] if need_ctx else None)


def kernel(x, c, ctx, c_ctx, ada_w, ada_b, norm1_g, norm2_g, w_in, w_out, conv_w, conv_g,
           rwkv_w0, rwkv_w2, rwkv_a0, rwkv_a2, rwkv_g2, rwkv_kk, rwkv_ka, rwkv_rk, rwkv_ln_g,
           att_q_g, att_k_g, att_sink, att_out_g, ml_i_b, ml_f_b, ml_out_g,
           peer_wq, peer_keys, peer_u, peer_v):
    B, T, D = x.shape
    ROWS = T // GRID_W
    row = jnp.repeat(jnp.arange(ROWS), GRID_W)
    col = jnp.arange(ROWS * GRID_W) % GRID_W
    for l in range(DEPTH):
        need_ctx = l < DEPTH - 1
        mod = jax.nn.silu(c) @ ada_w[l] + ada_b[l]
        mod_c = jax.nn.silu(c_ctx) @ ada_w[l] + ada_b[l]
        sh1, sc1, gt1, sh2, sc2, gt2 = jnp.split(mod[:, None, :], 6, axis=-1)
        csh1, csc1, cgt1, csh2, csc2, cgt2 = jnp.split(mod_c, 6, axis=-1)

        h = rms_norm(x, norm1_g[l]) * (1 + sc1) + sh1
        hc = rms_norm(ctx, norm1_g[l]) * (1 + csc1) + csh1
        P = jnp.split(_mm3(h, w_in[l]), IN_OFFSETS, axis=-1)
        Pc = jnp.split(_mm3(hc, w_in[l]), IN_OFFSETS, axis=-1)

        y_a = conv_mixer(P[0], P[1], P[2], conv_w[l], conv_g[l])
        y_b, yc_b = rwkv_mixer(P[3:9], Pc[3:9], rwkv_w0[l], rwkv_w2[l], rwkv_a0[l], rwkv_a2[l],
                               rwkv_g2[l], rwkv_kk[l], rwkv_ka[l], rwkv_rk[l], rwkv_ln_g[l], need_ctx)
        q, k, v = attn_project(P[9], P[10], P[11], att_q_g[l], att_k_g[l])
        q, k = rope_2d(q, row, col), rope_2d(k, row, col)
        qc, kc, vc = attn_project(Pc[9], Pc[10], Pc[11], att_q_g[l], att_k_g[l])
        y_c = head_norm_merge(latent_attention(q, k, v, kc, vc, att_sink[l]), att_out_g[l])
        y_d, yc_d = mlstm_mixer(P[12:17], Pc[12:17], ml_i_b[l], ml_f_b[l], ml_out_g[l], need_ctx)

        y = _mm3(jnp.concatenate([t.astype(x.dtype) for t in (y_a, y_b, y_c, y_d)], axis=-1), w_out[l])
        x = x + gt1 * y
        h2 = rms_norm(x, norm2_g[l]) * (1 + sc2) + sh2
        tok = [h2.reshape(B * T, D)]
        if need_ctx:
            yc_a = conv_mixer(Pc[0], Pc[1], Pc[2], conv_w[l], conv_g[l])
            yc_c = head_norm_merge(ctx_attention(qc, kc, vc, att_sink[l]), att_out_g[l])
            yc = _mm3(jnp.concatenate([t.astype(ctx.dtype) for t in (yc_a, yc_b, yc_c, yc_d)], axis=-1), w_out[l])
            ctx = ctx + cgt1 * yc
            hc2 = rms_norm(ctx, norm2_g[l]) * (1 + csc2) + csh2
            tok.append(hc2.reshape(-1, D))
        peer_t = _peer_dense(jnp.concatenate(tok, axis=0) if need_ctx else tok[0],
                             peer_wq[l].astype(BF16),
                             peer_keys[l].reshape(2 * PEER_HEADS, N_KEYS, PEER_HALF).astype(BF16),
                             peer_u[l].astype(BF16), peer_v[l].T.astype(BF16))
        x = x + gt2 * peer_t[:, :B * T].T.reshape(B, T, D)
        if need_ctx:
            ctx = ctx + cgt2 * peer_t[:, B * T:].T.reshape(ctx.shape)
    return x
```

```python
import functools
import math

import jax
import jax.numpy as jnp
import numpy as np
from jax import lax
from jax.experimental import pallas as pl
from jax.experimental.pallas import tpu as pltpu

D_MODEL = 1024
DEPTH = 2
GRID_W = 64
N_MIXERS = 4
GROUP_W = D_MODEL // N_MIXERS
HEAD_DIM = 64
N_HEADS = GROUP_W // HEAD_DIM
CONV_K = 3
W_LORA = 16
A_LORA = 16
G_LORA = 32
RWKV_DECAY_SCALE = math.exp(-0.5)
KV_HEADS = 2
Q_PER_KV = N_HEADS // KV_HEADS
KV_W = KV_HEADS * HEAD_DIM
WINDOW = 128
ATT_BLOCK = 128
ATT_SCALE = HEAD_DIM ** -0.5
ROPE_BASE = 10000.0
MLSTM_CHUNK = 128
N_GATE_COLS = 2 * 2 * N_HEADS
PEER_HEADS = 8
N_KEYS = 128
N_EXPERTS = N_KEYS * N_KEYS
PEER_TOPK = 16
PEER_QDIM = 256
PEER_HALF = PEER_QDIM // 2
PEER_BLOCK = 128
EPS = 1e-6
F32 = jnp.float32
BF16 = jnp.bfloat16
IN_SIZES = (GROUP_W, GROUP_W, GROUP_W,
            GROUP_W, GROUP_W, GROUP_W, W_LORA, A_LORA, G_LORA,
            GROUP_W, KV_W, KV_W,
            GROUP_W, GROUP_W, GROUP_W, GROUP_W, N_GATE_COLS)
D_IN = sum(IN_SIZES)
IN_OFFSETS = tuple(int(o) for o in np.cumsum(IN_SIZES)[:-1])

LANE = 128
VMEM_LIMIT_BYTES = 56 * 1024 * 1024


def _mm_kernel(x_ref, w_ref, o_ref):
    o_ref[...] = jnp.dot(x_ref[...].astype(BF16), w_ref[...], preferred_element_type=F32)


def _matmul(x, w, tm=512):
    M, K = x.shape
    N = w.shape[1]
    Np = -(-N // LANE) * LANE
    wb = w.astype(BF16)
    if Np != N:
        wb = jnp.pad(wb, ((0, 0), (0, Np - N)))
    tm = min(tm, M)
    assert M % tm == 0
    out = pl.pallas_call(
        _mm_kernel,
        grid=(M // tm,),
        in_specs=[pl.BlockSpec((tm, K), lambda i: (i, 0)),
                  pl.BlockSpec((K, Np), lambda i: (0, 0))],
        out_specs=pl.BlockSpec((tm, Np), lambda i: (i, 0)),
        out_shape=jax.ShapeDtypeStruct((M, Np), F32),
        compiler_params=pltpu.CompilerParams(
            dimension_semantics=("arbitrary",), vmem_limit_bytes=VMEM_LIMIT_BYTES),
        name="matmul",
    )(x, wb)
    return out[:, :N] if Np != N else out


def _mm3(x, w):
    lead = x.shape[:-1]
    return _matmul(x.reshape(-1, x.shape[-1]), w).reshape(lead + (w.shape[1],))


_PEER_CAND_ROWS = 80
_NEG_INF = float("-inf")


def _gelu_tanh(x):
    c = math.sqrt(2.0 / math.pi)
    return 0.5 * x * (1.0 + jnp.tanh(c * (x + 0.044715 * (x * x * x))))


def _top16_sorted(x, n_rows, want_rank=False):
    iota = lax.broadcasted_iota(jnp.int32, x.shape, 0)
    vals = []
    rank = jnp.full(x.shape, float(n_rows - 1), F32) if want_rank else None
    for step in range(PEER_TOPK):
        mx = jnp.max(x, axis=0, keepdims=True)
        vals.append(mx)
        hit = iota == jnp.min(jnp.where(x == mx, iota, n_rows), axis=0, keepdims=True)
        x = jnp.where(hit, _NEG_INF, x)
        if want_rank:
            rank = jnp.where(hit, float(step), rank)
    return vals, rank


def _peer_candidates(sv1, sv2):
    row8 = lax.broadcasted_iota(jnp.int32, (8, sv1.shape[1]), 0)
    blocks = [sv1[0:1, :] + sv2[0:8, :], sv1[0:1, :] + sv2[8:16, :]]
    for a in range(1, 8):
        n_valid = PEER_TOPK // (a + 1)
        blk = sv1[a:a + 1, :] + sv2[0:8, :]
        blocks.append(blk if n_valid >= 8 else jnp.where(row8 < n_valid, blk, _NEG_INF))
    blocks.append(sv1[8:16, :] + sv2[0:1, :])
    return jnp.concatenate(blocks, axis=0)


def _peer_kernel(h_ref, wq_ref, keys_ref, u_ref, vt_ref, o_ref,
                 hb_ref, s_ref, sv_ref, e1_ref, cnt_ref, e2_ref, rank_ref, act_ref, wg_ref, *, tm, te):
    j = pl.program_id(1)
    n_lg = tm // LANE
    n_pair = n_lg // 2
    n_ib = te // N_KEYS

    @pl.when(j == 0)
    def _prepare():
        hb = h_ref[...].astype(BF16)
        hb_ref[...] = hb
        qb = jnp.dot(hb, wq_ref[...], preferred_element_type=F32).astype(BF16)
        for hp in range(2 * PEER_HEADS):
            s_ref[hp] = lax.dot_general(keys_ref[hp], qb[:, hp * PEER_HALF:(hp + 1) * PEER_HALF],
                                        (((1,), (1,)), ((), ())), preferred_element_type=F32)

        def lane_groups(it):
            return it // n_pair, [pl.ds(pl.multiple_of(((it % n_pair) * 2 + half) * LANE, LANE), LANE)
                                  for half in range(2)]

        def top_body(it, carry):
            h, groups = lane_groups(it)
            for ls in groups:
                vals, _ = _top16_sorted(s_ref[2 * h, :, ls], N_KEYS)
                sv_ref[2 * h, :, ls] = jnp.concatenate(vals, axis=0)
                vals, rank = _top16_sorted(s_ref[2 * h + 1, :, ls], N_KEYS, want_rank=True)
                sv_ref[2 * h + 1, :, ls] = jnp.concatenate(vals, axis=0)
                rank_ref[h, :, ls] = rank.astype(BF16)
            return carry

        lax.fori_loop(0, PEER_HEADS * n_pair, top_body, 0)

        def head_body(it, carry):
            h, groups = lane_groups(it)
            for ls in groups:
                sv1 = sv_ref[2 * h, :, ls]
                sv2 = sv_ref[2 * h + 1, :, ls]
                tv, _ = _top16_sorted(_peer_candidates(sv1, sv2), _PEER_CAND_ROWS)
                thr = tv[PEER_TOPK - 1]
                z = jnp.zeros_like(thr)
                for t in tv:
                    z = z + jnp.exp(t - tv[0])
                s1 = s_ref[2 * h, :, ls]
                cnt = jnp.zeros_like(s1)
                for b in range(PEER_TOPK):
                    cnt = cnt + jnp.where(s1 + sv2[b:b + 1, :] >= thr, 1.0, 0.0)
                cnt_ref[h, :, ls] = cnt
                e1_ref[h, :, ls] = jnp.exp(s1 - sv1[0:1, :])
                e2_ref[h, :, ls] = (jnp.exp(s_ref[2 * h + 1, :, ls] - sv2[0:1, :]) / z).astype(BF16)
            return carry

        lax.fori_loop(0, PEER_HEADS * n_pair, head_body, 0)
        o_ref[...] = jnp.zeros_like(o_ref)

    i0 = pl.multiple_of(j * n_ib, 8)
    half_rows = te // 2
    for part in range(2):
        rows = slice(part * half_rows, (part + 1) * half_rows)
        act_ref[rows, :] = lax.dot_general(u_ref[rows, :], hb_ref[...], (((1,), (1,)), ((), ())),
                                           preferred_element_type=F32)
    for part in range(2):
        for lg in range(n_lg):
            ls = slice(lg * LANE, (lg + 1) * LANE)
            cnt8 = [cnt_ref[h, pl.ds(i0, 8), ls].astype(BF16) for h in range(PEER_HEADS)]
            e18 = [e1_ref[h, pl.ds(i0, 8), ls].astype(BF16) for h in range(PEER_HEADS)]
            for ib in range(part * n_ib // 2, (part + 1) * n_ib // 2):
                w = jnp.zeros((N_KEYS, LANE), BF16)
                for h in range(PEER_HEADS):
                    sel = rank_ref[h, :, ls] < cnt8[h][ib:ib + 1, :]
                    w = w + jnp.where(sel, e2_ref[h, :, ls] * e18[h][ib:ib + 1, :], jnp.zeros((), BF16))
                rs = slice(ib * N_KEYS, (ib + 1) * N_KEYS)
                wg_ref[rs, ls] = w * _gelu_tanh(act_ref[rs, ls]).astype(BF16)
        rows = slice(part * half_rows, (part + 1) * half_rows)
        o_ref[...] += jnp.dot(vt_ref[:, rows], wg_ref[rows, :], preferred_element_type=F32)


def _peer_dense(hf, wq_b, keys_b, u_b, vt_b, tm=512, te=1024):
    M, D = hf.shape
    E = u_b.shape[0]
    tm = min(tm, M)
    assert M % tm == 0 and E % te == 0 and tm % (2 * LANE) == 0 and te == 8 * N_KEYS
    kern = functools.partial(_peer_kernel, tm=tm, te=te)
    return pl.pallas_call(
        kern,
        grid=(M // tm, E // te),
        in_specs=[pl.BlockSpec((tm, D), lambda i, j: (i, 0)),
                  pl.BlockSpec(wq_b.shape, lambda i, j: (0, 0)),
                  pl.BlockSpec(keys_b.shape, lambda i, j: (0, 0, 0)),
                  pl.BlockSpec((te, D), lambda i, j: (j, 0)),
                  pl.BlockSpec((D, te), lambda i, j: (0, j))],
        out_specs=pl.BlockSpec((D, tm), lambda i, j: (0, i)),
        out_shape=jax.ShapeDtypeStruct((D, M), F32),
        scratch_shapes=[pltpu.VMEM((tm, D), BF16),
                        pltpu.VMEM((2 * PEER_HEADS, N_KEYS, tm), F32),
                        pltpu.VMEM((2 * PEER_HEADS, PEER_TOPK, tm), F32),
                        pltpu.VMEM((PEER_HEADS, N_KEYS, tm), F32),
                        pltpu.VMEM((PEER_HEADS, N_KEYS, tm), F32),
                        pltpu.VMEM((PEER_HEADS, N_KEYS, tm), BF16),
                        pltpu.VMEM((PEER_HEADS, N_KEYS, tm), BF16),
                        pltpu.VMEM((te, tm), F32),
                        pltpu.VMEM((te, tm), BF16)],
        compiler_params=pltpu.CompilerParams(
            dimension_semantics=("arbitrary", "arbitrary"), vmem_limit_bytes=VMEM_LIMIT_BYTES),
        name="peer_dense",
    )(hf, wq_b, keys_b, u_b, vt_b)


_RWKV_UNROLL = 8


def _rwkv_kernel(sh_f_ref, sh_b_ref, d0_ref, d1_ref, y0_ref, y1_ref, s_ref, sa_ref, *, tb, n_batch):
    i = pl.program_id(0)
    n_ch = 2 * n_batch
    n_tiles = tb // _RWKV_UNROLL

    @pl.when(i == 0)
    def _init():
        s_ref[...] = jnp.zeros_like(s_ref)

    lane = lax.broadcasted_iota(jnp.int32, (GROUP_W, GROUP_W), 1)
    sub = lax.broadcasted_iota(jnp.int32, (GROUP_W, GROUP_W), 0)
    seg_ones = jnp.where(lane // HEAD_DIM == sub // HEAD_DIM, 1.0, 0.0).astype(BF16)
    lane_v = lax.broadcasted_iota(jnp.int32, (HEAD_DIM, GROUP_W), 1)
    sub_v = lax.broadcasted_iota(jnp.int32, (HEAD_DIM, GROUP_W), 0)
    eye = (lane_v % HEAD_DIM == sub_v)

    def seg_sum(p):
        return jnp.dot(p.astype(BF16), seg_ones, preferred_element_type=F32)

    def chain_refs(c):
        d, b = divmod(c, n_batch)
        return d, b, (sh_f_ref, d0_ref) if d == 0 else (sh_b_ref, d1_ref)

    def tile_start(d, tt):
        return pl.multiple_of((tt if d == 0 else n_tiles - 1 - tt) * _RWKV_UNROLL, _RWKV_UNROLL)

    def first_kk(c, tt):
        d, b, (sh_ref, _) = chain_refs(c)
        kk8 = sh_ref[0, b, pl.ds(tile_start(d, tt), _RWKV_UNROLL), :]
        r0 = 0 if d == 0 else _RWKV_UNROLL - 1
        return kk8[r0:r0 + 1, :]

    sa_ref[...] = seg_sum(jnp.concatenate([s_ref[c] * first_kk(c, 0) for c in range(n_ch)], axis=0))

    def tile_body(tt, carry):
        rows, t8s, dirs = [], [], []
        for c in range(n_ch):
            d, b, (sh_ref, dr_ref) = chain_refs(c)
            t8 = tile_start(d, tt)
            rows.append([sh_ref[q, b, pl.ds(t8, _RWKV_UNROLL), :] for q in range(3)]
                        + [dr_ref[q, b, pl.ds(t8, _RWKV_UNROLL), :] for q in range(3)])
            t8s.append(t8)
            dirs.append(d)
        kk_next_tile = [first_kk(c, jnp.minimum(tt + 1, n_tiles - 1)) for c in range(n_ch)]
        vdiag = jnp.concatenate([jnp.where(eye, rows[c][2][s:s + 1, :], 0.0)
                                 for c in range(n_ch) for s in range(_RWKV_UNROLL)], axis=0)
        vexp = seg_sum(vdiag)
        S = [s_ref[c] for c in range(n_ch)]
        sa = sa_ref[...]
        ys = [[None] * _RWKV_UNROLL for _ in range(n_ch)]
        for step in range(_RWKV_UNROLL):
            row = [step if d == 0 else _RWKV_UNROLL - 1 - step for d in dirs]

            def r_(c, q, rw=None):
                rw = row[c] if rw is None else rw
                return rows[c][q][rw:rw + 1, :]

            for c in range(n_ch):
                v0 = (c * _RWKV_UNROLL + row[c]) * HEAD_DIM
                S[c] = (S[c] * r_(c, 3) - sa[c * HEAD_DIM:(c + 1) * HEAD_DIM, :] * r_(c, 4)
                        + vexp[v0:v0 + HEAD_DIM, :] * r_(c, 5))
            if step + 1 < _RWKV_UNROLL:
                kk_next = [r_(c, 0, row[c] + (1 if dirs[c] == 0 else -1)) for c in range(n_ch)]
            else:
                kk_next = kk_next_tile
            sa = seg_sum(jnp.concatenate([S[c] * kk_next[c] for c in range(n_ch)], axis=0))
            for c in range(n_ch):
                ys[c][row[c]] = S[c] * r_(c, 1)
        ye = seg_sum(jnp.concatenate([ys[c][s] for c in range(n_ch) for s in range(_RWKV_UNROLL)], axis=0))
        for c in range(n_ch):
            for s in range(_RWKV_UNROLL):
                v0 = (c * _RWKV_UNROLL + s) * HEAD_DIM
                ys[c][s] = jnp.sum(jnp.where(eye, ye[v0:v0 + HEAD_DIM, :], 0.0), axis=0, keepdims=True)
        sa_ref[...] = sa
        for c in range(n_ch):
            s_ref[c] = S[c]
            y_ref = y0_ref if dirs[c] == 0 else y1_ref
            y_ref[c % n_batch, pl.ds(t8s[c], _RWKV_UNROLL), :] = jnp.concatenate(ys[c], axis=0)
        return carry

    lax.fori_loop(0, n_tiles, tile_body, 0)


def _rwkv_scan_pallas(shared, dir0, dir1, n_ctx, tb=256):
    _, B, T, C = shared.shape
    assert T % tb == 0 and n_ctx % tb == 0 and tb % _RWKV_UNROLL == 0 and C == GROUP_W
    nblk, ncb = T // tb, n_ctx // tb

    def fwd3(i):
        return (0, 0, i, 0)

    def bwd_blk(i):
        return jnp.where(i < ncb, ncb - 1 - i, nblk - 1 - (i - ncb))

    def bwd3(i):
        return (0, 0, bwd_blk(i), 0)

    kern = functools.partial(_rwkv_kernel, tb=tb, n_batch=B)
    blk = (3, B, tb, C)
    return pl.pallas_call(
        kern,
        grid=(nblk,),
        in_specs=[pl.BlockSpec(blk, fwd3), pl.BlockSpec(blk, bwd3), pl.BlockSpec(blk, fwd3), pl.BlockSpec(blk, bwd3)],
        out_specs=[pl.BlockSpec((B, tb, C), lambda i: (0, i, 0)),
                   pl.BlockSpec((B, tb, C), lambda i: (0, bwd_blk(i), 0))],
        out_shape=[jax.ShapeDtypeStruct((B, T, C), F32)] * 2,
        scratch_shapes=[pltpu.VMEM((2 * B, HEAD_DIM, C), F32), pltpu.VMEM((2 * B * HEAD_DIM, C), F32)],
        compiler_params=pltpu.CompilerParams(
            dimension_semantics=("arbitrary",), vmem_limit_bytes=VMEM_LIMIT_BYTES),
        name="rwkv_scan",
    )(shared, shared, dir0, dir1)


def _mlstm_kernel(q_ref, k_ref, v_ref, row_ref, col_ref, h_ref, c_ref, n_ref, m_ref):
    g = pl.program_id(0)
    L = q_ref.shape[2]

    @pl.when(pl.program_id(1) == 0)
    def _init():
        c_ref[...] = jnp.zeros_like(c_ref)
        n_ref[...] = jnp.zeros_like(n_ref)
        m_ref[...] = jnp.zeros_like(m_ref)

    backward = (g % 2) == 1
    sgn = 1 - 2 * (g % 2)
    tt = lax.broadcasted_iota(jnp.int32, (L, L), 0)
    ss = lax.broadcasted_iota(jnp.int32, (L, L), 1)
    causal = (tt - ss) * sgn >= 0
    nt_dims = (((1,), (1,)), ((), ()))
    for h in range(N_HEADS):
        q = q_ref[0, h]
        k = k_ref[0, h]
        v = v_ref[0, h]
        qb, kb, vb = q.astype(BF16), k.astype(BF16), v.astype(BF16)
        brow = row_ref[0, h, 0, 0:1, :]
        lirow = row_ref[0, h, 0, 1:2, :]
        bcol = col_ref[0, h, 0, :, 0:1]
        licol = col_ref[0, h, 0, :, 1:2]
        m_prev = m_ref[h][:, 0:1]
        C = c_ref[h]
        n = n_ref[h]
        d_intra = jnp.where(causal, bcol - brow + lirow, _NEG_INF)
        d_inter = bcol + m_prev
        m_t = jnp.maximum(jnp.max(d_intra, axis=1, keepdims=True), d_inter)
        w_intra = jnp.exp(d_intra - m_t)
        w_inter = jnp.exp(d_inter - m_t)
        s = lax.dot_general(qb, kb, nt_dims, preferred_element_type=F32) * w_intra
        num = (jnp.dot(s.astype(BF16), vb, preferred_element_type=F32)
               + w_inter * lax.dot_general(qb, C.astype(BF16), nt_dims, preferred_element_type=F32))
        den = jnp.sum(s, axis=1, keepdims=True) + w_inter * jnp.sum(q * n, axis=1, keepdims=True)
        h_ref[0, h] = num / jnp.maximum(jnp.abs(den), jnp.exp(-m_t))
        b_end = jnp.where(backward, brow[:, 0:1], brow[:, L - 1:L])
        d_end = b_end - bcol + licol
        m_new = jnp.maximum(b_end + m_prev, jnp.max(d_end, axis=0, keepdims=True))
        w_end = jnp.exp(d_end - m_new)
        decay = jnp.exp(b_end + m_prev - m_new)
        wv_t = (w_end * v).T.astype(BF16)
        c_ref[h] = decay * C + jnp.dot(wv_t, kb, preferred_element_type=F32)
        n_ref[h] = decay * n + jnp.sum(w_end * k, axis=0, keepdims=True)
        m_ref[h] = jnp.broadcast_to(m_new, (1, LANE))


def _mlstm_scan_pallas(qh, kh, vh, rows, cols, n_ctx):
    B, H, T, Dh = qh.shape
    L = MLSTM_CHUNK
    nc, ncb = T // L, n_ctx // L
    assert T % L == 0 and n_ctx % L == 0 and H == N_HEADS

    def chunk(g, c):
        rev = jnp.where(c < ncb, ncb - 1 - c, nc - 1 - (c - ncb))
        return jnp.where(g % 2 == 0, c, rev)

    qkv_spec = pl.BlockSpec((1, H, L, Dh), lambda g, c: (g // 2, 0, chunk(g, c), 0))
    return pl.pallas_call(
        _mlstm_kernel,
        grid=(2 * B, nc),
        in_specs=[qkv_spec, qkv_spec, qkv_spec,
                  pl.BlockSpec((1, H, 1, 2, L), lambda g, c: (g, 0, chunk(g, c), 0, 0)),
                  pl.BlockSpec((1, H, 1, L, 2), lambda g, c: (g, 0, chunk(g, c), 0, 0))],
        out_specs=pl.BlockSpec((1, H, L, Dh), lambda g, c: (g, 0, chunk(g, c), 0)),
        out_shape=jax.ShapeDtypeStruct((2 * B, H, T, Dh), F32),
        scratch_shapes=[pltpu.VMEM((H, Dh, Dh), F32), pltpu.VMEM((H, 1, Dh), F32), pltpu.VMEM((H, 1, LANE), F32)],
        compiler_params=pltpu.CompilerParams(
            dimension_semantics=("arbitrary", "arbitrary"), vmem_limit_bytes=VMEM_LIMIT_BYTES),
        name="mlstm_scan",
    )(qh, kh, vh, rows, cols)


def _attn_kernel(*refs, n_band, t_total):
    q_ref = refs[0]
    band = refs[1:1 + 2 * n_band]
    kc_ref, vc_ref, sink_ref, o_ref = refs[1 + 2 * n_band:]
    n = pl.program_id(1)
    nt_dims = (((1,), (1,)), ((), ()))
    if n_band:
        qpos = n * ATT_BLOCK + lax.broadcasted_iota(jnp.int32, (ATT_BLOCK, n_band * ATT_BLOCK), 0)
        kpos = (n - 1) * ATT_BLOCK + lax.broadcasted_iota(jnp.int32, (ATT_BLOCK, n_band * ATT_BLOCK), 1)
        mask = (jnp.abs(qpos - kpos) <= WINDOW) & (kpos >= 0) & (kpos < t_total)
    for kvh in range(KV_HEADS):
        kc = kc_ref[0, kvh].astype(BF16)
        vc = vc_ref[0, kvh].astype(BF16)
        if n_band:
            kw = jnp.concatenate([band[j][0, kvh] for j in range(n_band)], axis=0).astype(BF16)
            vw = jnp.concatenate([band[n_band + j][0, kvh] for j in range(n_band)], axis=0).astype(BF16)
        for g in range(Q_PER_KV):
            h = kvh * Q_PER_KV + g
            q = q_ref[0, h].astype(BF16)
            sink = sink_ref[h][:, 0:1]
            s_ctx = lax.dot_general(q, kc, nt_dims, preferred_element_type=F32) * ATT_SCALE
            m = jnp.maximum(jnp.max(s_ctx, axis=1, keepdims=True), sink)
            if n_band:
                s_loc = lax.dot_general(q, kw, nt_dims, preferred_element_type=F32) * ATT_SCALE
                s_loc = jnp.where(mask, s_loc, _NEG_INF)
                m = jnp.maximum(m, jnp.max(s_loc, axis=1, keepdims=True))
            p_ctx = jnp.exp(s_ctx - m)
            den = jnp.sum(p_ctx, axis=1, keepdims=True) + jnp.exp(sink - m)
            o = jnp.dot(p_ctx.astype(BF16), vc, preferred_element_type=F32)
            if n_band:
                p_loc = jnp.exp(s_loc - m)
                den = den + jnp.sum(p_loc, axis=1, keepdims=True)
                o = o + jnp.dot(p_loc.astype(BF16), vw, preferred_element_type=F32)
            o_ref[0, h] = o / den


def _attention_pallas(q, k, v, kc, vc, sink):
    B, H, T, Dh = q.shape
    C = kc.shape[2]
    nb = T // ATT_BLOCK
    assert T % ATT_BLOCK == 0
    n_band = 0 if k is None else 3
    sink_b = jnp.broadcast_to(sink.astype(F32)[:, None, None], (H, 1, LANE))
    band_specs = [pl.BlockSpec((1, KV_HEADS, ATT_BLOCK, Dh),
                               lambda b, n, j=j: (b, 0, jnp.clip(n + j - 1, 0, nb - 1), 0)) for j in range(n_band)]
    ctx_spec = pl.BlockSpec((1, KV_HEADS, C, Dh), lambda b, n: (b, 0, 0, 0))
    kern = functools.partial(_attn_kernel, n_band=n_band, t_total=T)
    band_args = [] if k is None else [k] * 3 + [v] * 3
    return pl.pallas_call(
        kern,
        grid=(B, nb),
        in_specs=[pl.BlockSpec((1, H, ATT_BLOCK, Dh), lambda b, n: (b, 0, n, 0))] + band_specs * 2
                 + [ctx_spec, ctx_spec, pl.BlockSpec((H, 1, LANE), lambda b, n: (0, 0, 0))],
        out_specs=pl.BlockSpec((1, H, ATT_BLOCK, Dh), lambda b, n: (b, 0, n, 0)),
        out_shape=jax.ShapeDtypeStruct((B, H, T, Dh), F32),
        compiler_params=pltpu.CompilerParams(
            dimension_semantics=("arbitrary", "arbitrary"), vmem_limit_bytes=VMEM_LIMIT_BYTES),
        name="attention",
    )(q, *band_args, kc, vc, sink_b)


def rms_norm(x, g):
    xf = x.astype(F32)
    y = xf * lax.rsqrt(jnp.mean(xf * xf, axis=-1, keepdims=True) + EPS)
    return (y * g.astype(F32)).astype(x.dtype)


def heads(t):
    return t.reshape(t.shape[:-1] + (N_HEADS, HEAD_DIM))


def head_norm_merge(y, g):
    return rms_norm(y, g).reshape(y.shape[:-2] + (GROUP_W,))


def rope_2d(x, row, col):
    quarter = HEAD_DIM // 4
    inv = ROPE_BASE ** (-jnp.arange(quarter, dtype=F32) / quarter)
    xf = x.astype(F32)
    extra = (1,) * (x.ndim - 3)

    def rot(xa, pos):
        ang = pos.astype(F32)[:, None] * inv[None, :]
        ang = ang.reshape((1, ang.shape[0]) + extra + (quarter,))
        cos, sin = jnp.cos(ang), jnp.sin(ang)
        x1, x2 = xa[..., :quarter], xa[..., quarter:]
        return jnp.concatenate([x1 * cos - x2 * sin, x2 * cos + x1 * sin], axis=-1)

    half = HEAD_DIM // 2
    return jnp.concatenate([rot(xf[..., :half], row), rot(xf[..., half:], col)], axis=-1).astype(x.dtype)


def conv_mixer(hx, b_gate, c_gate, w, g):
    u = c_gate * hx
    up = jnp.pad(u, ((0, 0), (1, 1), (0, 0)))
    y = b_gate * (w[0] * up[:, :-2] + w[1] * up[:, 1:-1] + w[2] * up[:, 2:])
    return head_norm_merge(heads(y), g)


def rwkv_mixer(lat, ctx, w0, w2, a0, a2, g2, k_k, k_a, r_k, ln_g, need_ctx):
    n_ctx = ctx[0].shape[1]
    r, k, v, xw, xa, xg = (jnp.concatenate([c_, l_], axis=1) for c_, l_ in zip(ctx, lat))
    kk = heads(k * k_k)
    kk = (kk * lax.rsqrt(jnp.sum(kk * kk, axis=-1, keepdims=True) + EPS)).reshape(k.shape)
    g = jax.nn.sigmoid(xg) @ g2
    dirs = []
    for d in range(2):
        decay = jnp.exp(-RWKV_DECAY_SCALE * jax.nn.sigmoid(w0[d] + jnp.tanh(xw) @ w2[d]))
        a = jax.nn.sigmoid(a0[d] + xa @ a2[d])
        dirs.append(jnp.stack([decay, kk * a, k * (1 + (a - 1) * k_a)]))
    y0, y1 = _rwkv_scan_pallas(jnp.stack([kk, r, v]), dirs[0], dirs[1], n_ctx)
    y = rms_norm(heads(y0 + y1), ln_g)
    rh = heads(r)
    bonus = (jnp.sum(rh * heads(dirs[0][2]) * r_k, axis=-1, keepdims=True)
             + jnp.sum(rh * heads(dirs[1][2]) * r_k, axis=-1, keepdims=True)) * heads(v)
    out = (y + bonus).reshape(r.shape) * g
    return out[:, n_ctx:], (out[:, :n_ctx] if need_ctx else None)


def attn_project(q, k, v, q_g, k_g):
    B, T, _ = q.shape
    q = rms_norm(q.reshape(B, T, KV_HEADS, Q_PER_KV, HEAD_DIM), q_g)
    k = rms_norm(k.reshape(B, T, KV_HEADS, HEAD_DIM), k_g)
    v = v.reshape(B, T, KV_HEADS, HEAD_DIM)
    return q, k, v


def _head_major(t):
    B, T = t.shape[:2]
    return jnp.moveaxis(t.reshape(B, T, -1, HEAD_DIM), 2, 1)


def latent_attention(q, k, v, kc, vc, sink):
    o = _attention_pallas(_head_major(q), _head_major(k), _head_major(v), _head_major(kc), _head_major(vc), sink)
    return jnp.moveaxis(o, 1, 2)


def ctx_attention(qc, kc, vc, sink):
    o = _attention_pallas(_head_major(qc), None, None, _head_major(kc), _head_major(vc), sink)
    return jnp.moveaxis(o, 1, 2)


def mlstm_mixer(lat, ctx, i_b, f_b, out_g, need_ctx):
    n_ctx = ctx[0].shape[1]
    q, k, v, o, gates = (jnp.concatenate([c_, l_], axis=1) for c_, l_ in zip(ctx, lat))
    B, T, _ = q.shape
    L = MLSTM_CHUNK

    def th(t):
        return jnp.moveaxis(heads(t.astype(F32)), 2, 1)

    gates = gates.astype(F32).reshape(B, T, 2, 2, N_HEADS) + jnp.stack([i_b, f_b], axis=1).astype(F32)
    gates = jnp.moveaxis(gates, 1, -1)
    logi = gates[:, :, 0].reshape(B, 2, N_HEADS, T // L, L)
    logf = jax.nn.log_sigmoid(gates[:, :, 1]).reshape(B, 2, N_HEADS, T // L, L)
    bcum = jnp.stack([jnp.cumsum(logf[:, 0], axis=-1),
                      jnp.flip(jnp.cumsum(jnp.flip(logf[:, 1], axis=-1), axis=-1), axis=-1)], axis=1)
    rows = jnp.stack([bcum, logi], axis=-2).reshape(B * 2, N_HEADS, T // L, 2, L)
    cols = jnp.stack([bcum, logi], axis=-1).reshape(B * 2, N_HEADS, T // L, L, 2)
    h = _mlstm_scan_pallas(th(q), th(k) * (HEAD_DIM ** -0.5), th(v), rows, cols, n_ctx)
    h = h.reshape(B, 2, N_HEADS, T, HEAD_DIM)
    y = jax.nn.sigmoid(o) * head_norm_merge(jnp.moveaxis(h[:, 0] + h[:, 1], 1, 2), out_g)
    return y[:, n_ctx:], (y[:, :n_ctx] if need_ctx else None)


def kernel(x, c, ctx, c_ctx, ada_w, ada_b, norm1_g, norm2_g, w_in, w_out, conv_w, conv_g,
           rwkv_w0, rwkv_w2, rwkv_a0, rwkv_a2, rwkv_g2, rwkv_kk, rwkv_ka, rwkv_rk, rwkv_ln_g,
           att_q_g, att_k_g, att_sink, att_out_g, ml_i_b, ml_f_b, ml_out_g,
           peer_wq, peer_keys, peer_u, peer_v):
    B, T, D = x.shape
    ROWS = T // GRID_W
    row = jnp.repeat(jnp.arange(ROWS), GRID_W)
    col = jnp.arange(ROWS * GRID_W) % GRID_W
    for l in range(DEPTH):
        need_ctx = l < DEPTH - 1
        mod = jax.nn.silu(c) @ ada_w[l] + ada_b[l]
        mod_c = jax.nn.silu(c_ctx) @ ada_w[l] + ada_b[l]
        sh1, sc1, gt1, sh2, sc2, gt2 = jnp.split(mod[:, None, :], 6, axis=-1)
        csh1, csc1, cgt1, csh2, csc2, cgt2 = jnp.split(mod_c, 6, axis=-1)

        h = rms_norm(x, norm1_g[l]) * (1 + sc1) + sh1
        hc = rms_norm(ctx, norm1_g[l]) * (1 + csc1) + csh1
        P = jnp.split(_mm3(h, w_in[l]), IN_OFFSETS, axis=-1)
        Pc = jnp.split(_mm3(hc, w_in[l]), IN_OFFSETS, axis=-1)

        y_a = conv_mixer(P[0], P[1], P[2], conv_w[l], conv_g[l])
        y_b, yc_b = rwkv_mixer(P[3:9], Pc[3:9], rwkv_w0[l], rwkv_w2[l], rwkv_a0[l], rwkv_a2[l],
                               rwkv_g2[l], rwkv_kk[l], rwkv_ka[l], rwkv_rk[l], rwkv_ln_g[l], need_ctx)
        q, k, v = attn_project(P[9], P[10], P[11], att_q_g[l], att_k_g[l])
        q, k = rope_2d(q, row, col), rope_2d(k, row, col)
        qc, kc, vc = attn_project(Pc[9], Pc[10], Pc[11], att_q_g[l], att_k_g[l])
        y_c = head_norm_merge(latent_attention(q, k, v, kc, vc, att_sink[l]), att_out_g[l])
        y_d, yc_d = mlstm_mixer(P[12:17], Pc[12:17], ml_i_b[l], ml_f_b[l], ml_out_g[l], need_ctx)

        y = _mm3(jnp.concatenate([t.astype(x.dtype) for t in (y_a, y_b, y_c, y_d)], axis=-1), w_out[l])
        x = x + gt1 * y
        h2 = rms_norm(x, norm2_g[l]) * (1 + sc2) + sh2
        tok = [h2.reshape(B * T, D)]
        if need_ctx:
            yc_a = conv_mixer(Pc[0], Pc[1], Pc[2], conv_w[l], conv_g[l])
            yc_c = head_norm_merge(ctx_attention(qc, kc, vc, att_sink[l]), att_out_g[l])
            yc = _mm3(jnp.concatenate([t.astype(ctx.dtype) for t in (yc_a, yc_b, yc_c, yc_d)], axis=-1), w_out[l])
            ctx = ctx + cgt1 * yc
            hc2 = rms_norm(ctx, norm2_g[l]) * (1 + csc2) + csh2
            tok.append(hc2.reshape(-1, D))
        peer_t = _peer_dense(jnp.concatenate(tok, axis=0) if need_ctx else tok[0],
                             peer_wq[l].astype(BF16),
                             peer_keys[l].reshape(2 * PEER_HEADS, N_KEYS, PEER_HALF).astype(BF16),
                             peer_u[l].astype(BF16), peer_v[l].T.astype(BF16))
        x = x + gt2 * peer_t[:, :B * T].T.reshape(B, T, D)
        if need_ctx:
            ctx = ctx + cgt2 * peer_t[:, B * T:].T.reshape(ctx.shape)
    return x
```

```python
import functools
import math

import jax
import jax.numpy as jnp
import numpy as np
from jax import lax
from jax.experimental import pallas as pl
from jax.experimental.pallas import tpu as pltpu

D_MODEL = 1024
DEPTH = 2
GRID_W = 64
N_MIXERS = 4
GROUP_W = D_MODEL // N_MIXERS
HEAD_DIM = 64
N_HEADS = GROUP_W // HEAD_DIM
CONV_K = 3
W_LORA = 16
A_LORA = 16
G_LORA = 32
RWKV_DECAY_SCALE = math.exp(-0.5)
KV_HEADS = 2
Q_PER_KV = N_HEADS // KV_HEADS
KV_W = KV_HEADS * HEAD_DIM
WINDOW = 128
ATT_BLOCK = 128
ATT_SCALE = HEAD_DIM ** -0.5
ROPE_BASE = 10000.0
MLSTM_CHUNK = 128
N_GATE_COLS = 2 * 2 * N_HEADS
PEER_HEADS = 8
N_KEYS = 128
N_EXPERTS = N_KEYS * N_KEYS
PEER_TOPK = 16
PEER_QDIM = 256
PEER_HALF = PEER_QDIM // 2
PEER_BLOCK = 128
EPS = 1e-6
F32 = jnp.float32
BF16 = jnp.bfloat16
IN_SIZES = (GROUP_W, GROUP_W, GROUP_W,
            GROUP_W, GROUP_W, GROUP_W, W_LORA, A_LORA, G_LORA,
            GROUP_W, KV_W, KV_W,
            GROUP_W, GROUP_W, GROUP_W, GROUP_W, N_GATE_COLS)
D_IN = sum(IN_SIZES)
IN_OFFSETS = tuple(int(o) for o in np.cumsum(IN_SIZES)[:-1])

LANE = 128
VMEM_LIMIT_BYTES = 56 * 1024 * 1024


def _mm_kernel(x_ref, w_ref, o_ref):
    o_ref[...] = jnp.dot(x_ref[...].astype(BF16), w_ref[...], preferred_element_type=F32)


def _matmul(x, w, tm=512):
    M, K = x.shape
    N = w.shape[1]
    Np = -(-N // LANE) * LANE
    wb = w.astype(BF16)
    if Np != N:
        wb = jnp.pad(wb, ((0, 0), (0, Np - N)))
    tm = min(tm, M)
    assert M % tm == 0
    out = pl.pallas_call(
        _mm_kernel,
        grid=(M // tm,),
        in_specs=[pl.BlockSpec((tm, K), lambda i: (i, 0)),
                  pl.BlockSpec((K, Np), lambda i: (0, 0))],
        out_specs=pl.BlockSpec((tm, Np), lambda i: (i, 0)),
        out_shape=jax.ShapeDtypeStruct((M, Np), F32),
        compiler_params=pltpu.CompilerParams(
            dimension_semantics=("arbitrary",), vmem_limit_bytes=VMEM_LIMIT_BYTES),
        name="matmul",
    )(x, wb)
    return out[:, :N] if Np != N else out


def _mm3(x, w):
    lead = x.shape[:-1]
    return _matmul(x.reshape(-1, x.shape[-1]), w).reshape(lead + (w.shape[1],))


_PEER_CAND_ROWS = 80
_NEG_INF = float("-inf")


def _gelu_tanh(x):
    c = math.sqrt(2.0 / math.pi)
    return 0.5 * x * (1.0 + jnp.tanh(c * (x + 0.044715 * (x * x * x))))


def _top16_sorted(x, n_rows, want_rank=False):
    iota = lax.broadcasted_iota(jnp.int32, x.shape, 0)
    vals = []
    rank = jnp.full(x.shape, float(n_rows - 1), F32) if want_rank else None
    for step in range(PEER_TOPK):
        mx = jnp.max(x, axis=0, keepdims=True)
        vals.append(mx)
        hit = iota == jnp.min(jnp.where(x == mx, iota, n_rows), axis=0, keepdims=True)
        x = jnp.where(hit, _NEG_INF, x)
        if want_rank:
            rank = jnp.where(hit, float(step), rank)
    return vals, rank


def _peer_candidates(sv1, sv2):
    row8 = lax.broadcasted_iota(jnp.int32, (8, sv1.shape[1]), 0)
    blocks = [sv1[0:1, :] + sv2[0:8, :], sv1[0:1, :] + sv2[8:16, :]]
    for a in range(1, 8):
        n_valid = PEER_TOPK // (a + 1)
        blk = sv1[a:a + 1, :] + sv2[0:8, :]
        blocks.append(blk if n_valid >= 8 else jnp.where(row8 < n_valid, blk, _NEG_INF))
    blocks.append(sv1[8:16, :] + sv2[0:1, :])
    return jnp.concatenate(blocks, axis=0)


def _peer_kernel(h_ref, wq_ref, keys_ref, u_ref, vt_ref, o_ref,
                 hbt_ref, s_ref, sv_ref, e1_ref, cnt_ref, e2_ref, rank_ref, act_ref, wg_ref, *, tm, te):
    j = pl.program_id(1)
    n_lg = tm // LANE
    n_pair = n_lg // 2
    n_ib = te // N_KEYS

    @pl.when(j == 0)
    def _prepare():
        hb = h_ref[...].astype(BF16)
        hbt_ref[...] = h_ref[...].T.astype(BF16)
        qb = jnp.dot(hb, wq_ref[...], preferred_element_type=F32).astype(BF16)
        for hp in range(2 * PEER_HEADS):
            s_ref[hp] = lax.dot_general(keys_ref[hp], qb[:, hp * PEER_HALF:(hp + 1) * PEER_HALF],
                                        (((1,), (1,)), ((), ())), preferred_element_type=F32)

        def lane_groups(it):
            return it // n_pair, [pl.ds(pl.multiple_of(((it % n_pair) * 2 + half) * LANE, LANE), LANE)
                                  for half in range(2)]

        def top_body(it, carry):
            h, groups = lane_groups(it)
            for ls in groups:
                vals, _ = _top16_sorted(s_ref[2 * h, :, ls], N_KEYS)
                sv_ref[2 * h, :, ls] = jnp.concatenate(vals, axis=0)
                vals, rank = _top16_sorted(s_ref[2 * h + 1, :, ls], N_KEYS, want_rank=True)
                sv_ref[2 * h + 1, :, ls] = jnp.concatenate(vals, axis=0)
                rank_ref[h, :, ls] = rank.astype(BF16)
            return carry

        lax.fori_loop(0, PEER_HEADS * n_pair, top_body, 0)

        def head_body(it, carry):
            h, groups = lane_groups(it)
            for ls in groups:
                sv1 = sv_ref[2 * h, :, ls]
                sv2 = sv_ref[2 * h + 1, :, ls]
                tv, _ = _top16_sorted(_peer_candidates(sv1, sv2), _PEER_CAND_ROWS)
                thr = tv[PEER_TOPK - 1]
                z = jnp.zeros_like(thr)
                for t in tv:
                    z = z + jnp.exp(t - tv[0])
                s1 = s_ref[2 * h, :, ls]
                cnt = jnp.zeros_like(s1)
                for b in range(PEER_TOPK):
                    cnt = cnt + jnp.where(s1 + sv2[b:b + 1, :] >= thr, 1.0, 0.0)
                cnt_ref[h, :, ls] = cnt
                e1_ref[h, :, ls] = jnp.exp(s1 - sv1[0:1, :])
                e2_ref[h, :, ls] = (jnp.exp(s_ref[2 * h + 1, :, ls] - sv2[0:1, :]) / z).astype(BF16)
            return carry

        lax.fori_loop(0, PEER_HEADS * n_pair, head_body, 0)
        o_ref[...] = jnp.zeros_like(o_ref)

    i0 = pl.multiple_of(j * n_ib, 8)
    half_rows = te // 2
    act_ref[...] = jnp.dot(u_ref[...], hbt_ref[...], preferred_element_type=F32)
    for part in range(2):
        for lg in range(n_lg):
            ls = slice(lg * LANE, (lg + 1) * LANE)
            cnt8 = [cnt_ref[h, pl.ds(i0, 8), ls].astype(BF16) for h in range(PEER_HEADS)]
            e18 = [e1_ref[h, pl.ds(i0, 8), ls].astype(BF16) for h in range(PEER_HEADS)]
            for ib in range(part * n_ib // 2, (part + 1) * n_ib // 2):
                w = jnp.zeros((N_KEYS, LANE), BF16)
                for h in range(PEER_HEADS):
                    sel = rank_ref[h, :, ls] < cnt8[h][ib:ib + 1, :]
                    w = w + jnp.where(sel, e2_ref[h, :, ls] * e18[h][ib:ib + 1, :], jnp.zeros((), BF16))
                rs = slice(ib * N_KEYS, (ib + 1) * N_KEYS)
                wg_ref[rs, ls] = w * _gelu_tanh(act_ref[rs, ls]).astype(BF16)
        rows = slice(part * half_rows, (part + 1) * half_rows)
        o_ref[...] += jnp.dot(vt_ref[0, :, rows], wg_ref[rows, :], preferred_element_type=F32)


_PEER_TE = 8 * N_KEYS


def _peer_v_tiles(v):
    E, D = v.shape
    return jnp.transpose(v.astype(BF16).reshape(E // _PEER_TE, _PEER_TE, D), (0, 2, 1))


def _peer_dense(hf, wq_b, keys_b, u_b, vt_b, tm=512):
    M, D = hf.shape
    E = u_b.shape[0]
    te = _PEER_TE
    tm = min(tm, M)
    assert M % tm == 0 and E % te == 0 and tm % (2 * LANE) == 0 and vt_b.shape == (E // te, D, te)
    kern = functools.partial(_peer_kernel, tm=tm, te=te)
    return pl.pallas_call(
        kern,
        grid=(M // tm, E // te),
        in_specs=[pl.BlockSpec((tm, D), lambda i, j: (i, 0)),
                  pl.BlockSpec(wq_b.shape, lambda i, j: (0, 0)),
                  pl.BlockSpec(keys_b.shape, lambda i, j: (0, 0, 0)),
                  pl.BlockSpec((te, D), lambda i, j: (j, 0)),
                  pl.BlockSpec((1, D, te), lambda i, j: (j, 0, 0))],
        out_specs=pl.BlockSpec((D, tm), lambda i, j: (0, i)),
        out_shape=jax.ShapeDtypeStruct((D, M), F32),
        scratch_shapes=[pltpu.VMEM((D, tm), BF16),
                        pltpu.VMEM((2 * PEER_HEADS, N_KEYS, tm), F32),
                        pltpu.VMEM((2 * PEER_HEADS, PEER_TOPK, tm), F32),
                        pltpu.VMEM((PEER_HEADS, N_KEYS, tm), F32),
                        pltpu.VMEM((PEER_HEADS, N_KEYS, tm), F32),
                        pltpu.VMEM((PEER_HEADS, N_KEYS, tm), BF16),
                        pltpu.VMEM((PEER_HEADS, N_KEYS, tm), BF16),
                        pltpu.VMEM((te, tm), F32),
                        pltpu.VMEM((te, tm), BF16)],
        compiler_params=pltpu.CompilerParams(
            dimension_semantics=("arbitrary", "arbitrary"), vmem_limit_bytes=VMEM_LIMIT_BYTES),
        name="peer_dense",
    )(hf, wq_b, keys_b, u_b, vt_b)


_RWKV_UNROLL = 8


def _rwkv_kernel(sh_f_ref, sh_b_ref, d0_ref, d1_ref, y0_ref, y1_ref, s_ref, sa_ref, *, tb, n_batch):
    i = pl.program_id(0)
    n_ch = 2 * n_batch
    n_tiles = tb // _RWKV_UNROLL

    @pl.when(i == 0)
    def _init():
        s_ref[...] = jnp.zeros_like(s_ref)

    lane = lax.broadcasted_iota(jnp.int32, (GROUP_W, GROUP_W), 1)
    sub = lax.broadcasted_iota(jnp.int32, (GROUP_W, GROUP_W), 0)
    seg_ones = jnp.where(lane // HEAD_DIM == sub // HEAD_DIM, 1.0, 0.0).astype(BF16)
    lane_v = lax.broadcasted_iota(jnp.int32, (HEAD_DIM, GROUP_W), 1)
    sub_v = lax.broadcasted_iota(jnp.int32, (HEAD_DIM, GROUP_W), 0)
    eye = (lane_v % HEAD_DIM == sub_v)

    def seg_sum(p):
        return jnp.dot(p.astype(BF16), seg_ones, preferred_element_type=F32)

    def chain_refs(c):
        d, b = divmod(c, n_batch)
        return d, b, (sh_f_ref, d0_ref) if d == 0 else (sh_b_ref, d1_ref)

    def tile_start(d, tt):
        return pl.multiple_of((tt if d == 0 else n_tiles - 1 - tt) * _RWKV_UNROLL, _RWKV_UNROLL)

    def first_kk(c, tt):
        d, b, (sh_ref, _) = chain_refs(c)
        kk8 = sh_ref[0, b, pl.ds(tile_start(d, tt), _RWKV_UNROLL), :]
        r0 = 0 if d == 0 else _RWKV_UNROLL - 1
        return kk8[r0:r0 + 1, :]

    sa_ref[...] = seg_sum(jnp.concatenate([s_ref[c] * first_kk(c, 0) for c in range(n_ch)], axis=0))

    def tile_body(tt, carry):
        rows, t8s, dirs = [], [], []
        for c in range(n_ch):
            d, b, (sh_ref, dr_ref) = chain_refs(c)
            t8 = tile_start(d, tt)
            rows.append([sh_ref[q, b, pl.ds(t8, _RWKV_UNROLL), :] for q in range(3)]
                        + [dr_ref[q, b, pl.ds(t8, _RWKV_UNROLL), :] for q in range(3)])
            t8s.append(t8)
            dirs.append(d)
        kk_next_tile = [first_kk(c, jnp.minimum(tt + 1, n_tiles - 1)) for c in range(n_ch)]

        def step_rows(step):
            return [step if d == 0 else _RWKV_UNROLL - 1 - step for d in dirs]

        vexp = seg_sum(jnp.concatenate([jnp.where(eye, rows[c][2][s:s + 1, :], 0.0)
                                        for c in range(n_ch) for s in range(_RWKV_UNROLL)], axis=0))
        S = [s_ref[c] for c in range(n_ch)]
        sa = [sa_ref[c * HEAD_DIM:(c + 1) * HEAD_DIM, :] for c in range(n_ch)]
        ys = [[None] * _RWKV_UNROLL for _ in range(n_ch)]
        for step in range(_RWKV_UNROLL):
            row = step_rows(step)

            def r_(c, q, rw=None):
                rw = row[c] if rw is None else rw
                return rows[c][q][rw:rw + 1, :]

            if step + 1 < _RWKV_UNROLL:
                kk_next = [r_(c, 0, row[c] + (1 if dirs[c] == 0 else -1)) for c in range(n_ch)]
            else:
                kk_next = kk_next_tile
            prods = []
            for c in range(n_ch):
                v0 = (c * _RWKV_UNROLL + row[c]) * HEAD_DIM
                ahead = S[c] * r_(c, 3) + vexp[v0:v0 + HEAD_DIM, :] * r_(c, 5)
                prods.append(ahead * kk_next[c] - sa[c] * (r_(c, 4) * kk_next[c]))
                S[c] = ahead - sa[c] * r_(c, 4)
            res = seg_sum(jnp.concatenate(prods, axis=0))
            sa = [res[c * HEAD_DIM:(c + 1) * HEAD_DIM, :] for c in range(n_ch)]
            for c in range(n_ch):
                ys[c][row[c]] = S[c] * r_(c, 1)
        ye = seg_sum(jnp.concatenate([ys[c][s] for c in range(n_ch) for s in range(_RWKV_UNROLL)], axis=0))
        for c in range(n_ch):
            for s in range(_RWKV_UNROLL):
                v0 = (c * _RWKV_UNROLL + s) * HEAD_DIM
                ys[c][s] = jnp.sum(jnp.where(eye, ye[v0:v0 + HEAD_DIM, :], 0.0), axis=0, keepdims=True)
        sa_ref[...] = jnp.concatenate(sa, axis=0)
        for c in range(n_ch):
            s_ref[c] = S[c]
            y_ref = y0_ref if dirs[c] == 0 else y1_ref
            y_ref[c % n_batch, pl.ds(t8s[c], _RWKV_UNROLL), :] = jnp.concatenate(ys[c], axis=0)
        return carry

    lax.fori_loop(0, n_tiles, tile_body, 0)


def _rwkv_scan_pallas(shared, dir0, dir1, n_ctx, tb=256):
    _, B, T, C = shared.shape
    assert T % tb == 0 and n_ctx % tb == 0 and tb % _RWKV_UNROLL == 0 and C == GROUP_W
    nblk, ncb = T // tb, n_ctx // tb

    def fwd3(i):
        return (0, 0, i, 0)

    def bwd_blk(i):
        return jnp.where(i < ncb, ncb - 1 - i, nblk - 1 - (i - ncb))

    def bwd3(i):
        return (0, 0, bwd_blk(i), 0)

    kern = functools.partial(_rwkv_kernel, tb=tb, n_batch=B)
    blk = (3, B, tb, C)
    return pl.pallas_call(
        kern,
        grid=(nblk,),
        in_specs=[pl.BlockSpec(blk, fwd3), pl.BlockSpec(blk, bwd3), pl.BlockSpec(blk, fwd3), pl.BlockSpec(blk, bwd3)],
        out_specs=[pl.BlockSpec((B, tb, C), lambda i: (0, i, 0)),
                   pl.BlockSpec((B, tb, C), lambda i: (0, bwd_blk(i), 0))],
        out_shape=[jax.ShapeDtypeStruct((B, T, C), F32)] * 2,
        scratch_shapes=[pltpu.VMEM((2 * B, HEAD_DIM, C), F32), pltpu.VMEM((2 * B * HEAD_DIM, C), F32)],
        compiler_params=pltpu.CompilerParams(
            dimension_semantics=("arbitrary",), vmem_limit_bytes=VMEM_LIMIT_BYTES),
        name="rwkv_scan",
    )(shared, shared, dir0, dir1)


def _mlstm_kernel(q_ref, k_ref, v_ref, row_ref, col_ref, h_ref, c_ref, n_ref, m_ref):
    g = pl.program_id(0)
    L = q_ref.shape[2]

    @pl.when(pl.program_id(1) == 0)
    def _init():
        c_ref[...] = jnp.zeros_like(c_ref)
        n_ref[...] = jnp.zeros_like(n_ref)
        m_ref[...] = jnp.zeros_like(m_ref)

    backward = (g % 2) == 1
    sgn = 1 - 2 * (g % 2)
    tt = lax.broadcasted_iota(jnp.int32, (L, L), 0)
    ss = lax.broadcasted_iota(jnp.int32, (L, L), 1)
    causal = (tt - ss) * sgn >= 0
    nt_dims = (((1,), (1,)), ((), ()))
    for h in range(N_HEADS):
        q = q_ref[0, h]
        k = k_ref[0, h]
        v = v_ref[0, h]
        qb, kb, vb = q.astype(BF16), k.astype(BF16), v.astype(BF16)
        brow = row_ref[0, h, 0, 0:1, :]
        lirow = row_ref[0, h, 0, 1:2, :]
        bcol = col_ref[0, h, 0, :, 0:1]
        licol = col_ref[0, h, 0, :, 1:2]
        m_prev = m_ref[h][:, 0:1]
        C = c_ref[h]
        n = n_ref[h]
        d_intra = jnp.where(causal, bcol - brow + lirow, _NEG_INF)
        d_inter = bcol + m_prev
        m_t = jnp.maximum(jnp.max(d_intra, axis=1, keepdims=True), d_inter)
        w_intra = jnp.exp(d_intra - m_t)
        w_inter = jnp.exp(d_inter - m_t)
        s = lax.dot_general(qb, kb, nt_dims, preferred_element_type=F32) * w_intra
        num = (jnp.dot(s.astype(BF16), vb, preferred_element_type=F32)
               + w_inter * lax.dot_general(qb, C.astype(BF16), nt_dims, preferred_element_type=F32))
        den = jnp.sum(s, axis=1, keepdims=True) + w_inter * jnp.sum(q * n, axis=1, keepdims=True)
        h_ref[0, h] = num / jnp.maximum(jnp.abs(den), jnp.exp(-m_t))
        b_end = jnp.where(backward, brow[:, 0:1], brow[:, L - 1:L])
        d_end = b_end - bcol + licol
        m_new = jnp.maximum(b_end + m_prev, jnp.max(d_end, axis=0, keepdims=True))
        w_end = jnp.exp(d_end - m_new)
        decay = jnp.exp(b_end + m_prev - m_new)
        wv_t = (w_end * v).T.astype(BF16)
        c_ref[h] = decay * C + jnp.dot(wv_t, kb, preferred_element_type=F32)
        n_ref[h] = decay * n + jnp.sum(w_end * k, axis=0, keepdims=True)
        m_ref[h] = jnp.broadcast_to(m_new, (1, LANE))


def _mlstm_scan_pallas(qh, kh, vh, rows, cols, n_ctx):
    B, H, T, Dh = qh.shape
    L = MLSTM_CHUNK
    nc, ncb = T // L, n_ctx // L
    assert T % L == 0 and n_ctx % L == 0 and H == N_HEADS

    def chunk(g, c):
        rev = jnp.where(c < ncb, ncb - 1 - c, nc - 1 - (c - ncb))
        return jnp.where(g % 2 == 0, c, rev)

    qkv_spec = pl.BlockSpec((1, H, L, Dh), lambda g, c: (g // 2, 0, chunk(g, c), 0))
    return pl.pallas_call(
        _mlstm_kernel,
        grid=(2 * B, nc),
        in_specs=[qkv_spec, qkv_spec, qkv_spec,
                  pl.BlockSpec((1, H, 1, 2, L), lambda g, c: (g, 0, chunk(g, c), 0, 0)),
                  pl.BlockSpec((1, H, 1, L, 2), lambda g, c: (g, 0, chunk(g, c), 0, 0))],
        out_specs=pl.BlockSpec((1, H, L, Dh), lambda g, c: (g, 0, chunk(g, c), 0)),
        out_shape=jax.ShapeDtypeStruct((2 * B, H, T, Dh), F32),
        scratch_shapes=[pltpu.VMEM((H, Dh, Dh), F32), pltpu.VMEM((H, 1, Dh), F32), pltpu.VMEM((H, 1, LANE), F32)],
        compiler_params=pltpu.CompilerParams(
            dimension_semantics=("arbitrary", "arbitrary"), vmem_limit_bytes=VMEM_LIMIT_BYTES),
        name="mlstm_scan",
    )(qh, kh, vh, rows, cols)


def _attn_kernel(*refs, n_band, t_total):
    q_ref = refs[0]
    band = refs[1:1 + 2 * n_band]
    kc_ref, vc_ref, sink_ref, o_ref = refs[1 + 2 * n_band:]
    n = pl.program_id(1)
    nt_dims = (((1,), (1,)), ((), ()))
    if n_band:
        qpos = n * ATT_BLOCK + lax.broadcasted_iota(jnp.int32, (ATT_BLOCK, n_band * ATT_BLOCK), 0)
        kpos = (n - 1) * ATT_BLOCK + lax.broadcasted_iota(jnp.int32, (ATT_BLOCK, n_band * ATT_BLOCK), 1)
        mask = (jnp.abs(qpos - kpos) <= WINDOW) & (kpos >= 0) & (kpos < t_total)
    for kvh in range(KV_HEADS):
        kc = kc_ref[0, kvh].astype(BF16)
        vc = vc_ref[0, kvh].astype(BF16)
        if n_band:
            kw = jnp.concatenate([band[j][0, kvh] for j in range(n_band)], axis=0).astype(BF16)
            vw = jnp.concatenate([band[n_band + j][0, kvh] for j in range(n_band)], axis=0).astype(BF16)
        for g in range(Q_PER_KV):
            h = kvh * Q_PER_KV + g
            q = q_ref[0, h].astype(BF16)
            sink = sink_ref[h][:, 0:1]
            s_ctx = lax.dot_general(q, kc, nt_dims, preferred_element_type=F32) * ATT_SCALE
            m = jnp.maximum(jnp.max(s_ctx, axis=1, keepdims=True), sink)
            if n_band:
                s_loc = lax.dot_general(q, kw, nt_dims, preferred_element_type=F32) * ATT_SCALE
                s_loc = jnp.where(mask, s_loc, _NEG_INF)
                m = jnp.maximum(m, jnp.max(s_loc, axis=1, keepdims=True))
            p_ctx = jnp.exp(s_ctx - m)
            den = jnp.sum(p_ctx, axis=1, keepdims=True) + jnp.exp(sink - m)
            o = jnp.dot(p_ctx.astype(BF16), vc, preferred_element_type=F32)
            if n_band:
                p_loc = jnp.exp(s_loc - m)
                den = den + jnp.sum(p_loc, axis=1, keepdims=True)
                o = o + jnp.dot(p_loc.astype(BF16), vw, preferred_element_type=F32)
            o_ref[0, h] = o / den


def _attention_pallas(q, k, v, kc, vc, sink):
    B, H, T, Dh = q.shape
    C = kc.shape[2]
    nb = T // ATT_BLOCK
    assert T % ATT_BLOCK == 0
    n_band = 0 if k is None else 3
    sink_b = jnp.broadcast_to(sink.astype(F32)[:, None, None], (H, 1, LANE))
    band_specs = [pl.BlockSpec((1, KV_HEADS, ATT_BLOCK, Dh),
                               lambda b, n, j=j: (b, 0, jnp.clip(n + j - 1, 0, nb - 1), 0)) for j in range(n_band)]
    ctx_spec = pl.BlockSpec((1, KV_HEADS, C, Dh), lambda b, n: (b, 0, 0, 0))
    kern = functools.partial(_attn_kernel, n_band=n_band, t_total=T)
    band_args = [] if k is None else [k] * 3 + [v] * 3
    return pl.pallas_call(
        kern,
        grid=(B, nb),
        in_specs=[pl.BlockSpec((1, H, ATT_BLOCK, Dh), lambda b, n: (b, 0, n, 0))] + band_specs * 2
                 + [ctx_spec, ctx_spec, pl.BlockSpec((H, 1, LANE), lambda b, n: (0, 0, 0))],
        out_specs=pl.BlockSpec((1, H, ATT_BLOCK, Dh), lambda b, n: (b, 0, n, 0)),
        out_shape=jax.ShapeDtypeStruct((B, H, T, Dh), F32),
        compiler_params=pltpu.CompilerParams(
            dimension_semantics=("arbitrary", "arbitrary"), vmem_limit_bytes=VMEM_LIMIT_BYTES),
        name="attention",
    )(q, *band_args, kc, vc, sink_b)


def rms_norm(x, g):
    xf = x.astype(F32)
    y = xf * lax.rsqrt(jnp.mean(xf * xf, axis=-1, keepdims=True) + EPS)
    return (y * g.astype(F32)).astype(x.dtype)


def heads(t):
    return t.reshape(t.shape[:-1] + (N_HEADS, HEAD_DIM))


def head_norm_merge(y, g):
    return rms_norm(y, g).reshape(y.shape[:-2] + (GROUP_W,))


def rope_2d(x, row, col):
    quarter = HEAD_DIM // 4
    inv = ROPE_BASE ** (-jnp.arange(quarter, dtype=F32) / quarter)
    xf = x.astype(F32)
    extra = (1,) * (x.ndim - 3)

    def rot(xa, pos):
        ang = pos.astype(F32)[:, None] * inv[None, :]
        ang = ang.reshape((1, ang.shape[0]) + extra + (quarter,))
        cos, sin = jnp.cos(ang), jnp.sin(ang)
        x1, x2 = xa[..., :quarter], xa[..., quarter:]
        return jnp.concatenate([x1 * cos - x2 * sin, x2 * cos + x1 * sin], axis=-1)

    half = HEAD_DIM // 2
    return jnp.concatenate([rot(xf[..., :half], row), rot(xf[..., half:], col)], axis=-1).astype(x.dtype)


def conv_mixer(hx, b_gate, c_gate, w, g):
    u = c_gate * hx
    up = jnp.pad(u, ((0, 0), (1, 1), (0, 0)))
    y = b_gate * (w[0] * up[:, :-2] + w[1] * up[:, 1:-1] + w[2] * up[:, 2:])
    return head_norm_merge(heads(y), g)


def rwkv_mixer(lat, ctx, w0, w2, a0, a2, g2, k_k, k_a, r_k, ln_g, need_ctx):
    n_ctx = ctx[0].shape[1]
    r, k, v, xw, xa, xg = (jnp.concatenate([c_, l_], axis=1) for c_, l_ in zip(ctx, lat))
    kk = heads(k * k_k)
    kk = (kk * lax.rsqrt(jnp.sum(kk * kk, axis=-1, keepdims=True) + EPS)).reshape(k.shape)
    g = jax.nn.sigmoid(xg) @ g2
    dirs = []
    for d in range(2):
        decay = jnp.exp(-RWKV_DECAY_SCALE * jax.nn.sigmoid(w0[d] + jnp.tanh(xw) @ w2[d]))
        a = jax.nn.sigmoid(a0[d] + xa @ a2[d])
        dirs.append(jnp.stack([decay, kk * a, k * (1 + (a - 1) * k_a)]))
    y0, y1 = _rwkv_scan_pallas(jnp.stack([kk, r, v]), dirs[0], dirs[1], n_ctx)
    y = rms_norm(heads(y0 + y1), ln_g)
    rh = heads(r)
    bonus = (jnp.sum(rh * heads(dirs[0][2]) * r_k, axis=-1, keepdims=True)
             + jnp.sum(rh * heads(dirs[1][2]) * r_k, axis=-1, keepdims=True)) * heads(v)
    out = (y + bonus).reshape(r.shape) * g
    return out[:, n_ctx:], (out[:, :n_ctx] if need_ctx else None)


def attn_project(q, k, v, q_g, k_g):
    B, T, _ = q.shape
    q = rms_norm(q.reshape(B, T, KV_HEADS, Q_PER_KV, HEAD_DIM), q_g)
    k = rms_norm(k.reshape(B, T, KV_HEADS, HEAD_DIM), k_g)
    v = v.reshape(B, T, KV_HEADS, HEAD_DIM)
    return q, k, v


def _head_major(t):
    B, T = t.shape[:2]
    return jnp.moveaxis(t.reshape(B, T, -1, HEAD_DIM), 2, 1)


def latent_attention(q, k, v, kc, vc, sink):
    o = _attention_pallas(_head_major(q), _head_major(k), _head_major(v), _head_major(kc), _head_major(vc), sink)
    return jnp.moveaxis(o, 1, 2)


def ctx_attention(qc, kc, vc, sink):
    o = _attention_pallas(_head_major(qc), None, None, _head_major(kc), _head_major(vc), sink)
    return jnp.moveaxis(o, 1, 2)


def mlstm_mixer(lat, ctx, i_b, f_b, out_g, need_ctx):
    n_ctx = ctx[0].shape[1]
    q, k, v, o, gates = (jnp.concatenate([c_, l_], axis=1) for c_, l_ in zip(ctx, lat))
    B, T, _ = q.shape
    L = MLSTM_CHUNK

    def th(t):
        return jnp.moveaxis(heads(t.astype(F32)), 2, 1)

    gates = gates.astype(F32).reshape(B, T, 2, 2, N_HEADS) + jnp.stack([i_b, f_b], axis=1).astype(F32)
    gates = jnp.moveaxis(gates, 1, -1)
    logi = gates[:, :, 0].reshape(B, 2, N_HEADS, T // L, L)
    logf = jax.nn.log_sigmoid(gates[:, :, 1]).reshape(B, 2, N_HEADS, T // L, L)
    bcum = jnp.stack([jnp.cumsum(logf[:, 0], axis=-1),
                      jnp.flip(jnp.cumsum(jnp.flip(logf[:, 1], axis=-1), axis=-1), axis=-1)], axis=1)
    rows = jnp.stack([bcum, logi], axis=-2).reshape(B * 2, N_HEADS, T // L, 2, L)
    cols = jnp.stack([bcum, logi], axis=-1).reshape(B * 2, N_HEADS, T // L, L, 2)
    h = _mlstm_scan_pallas(th(q), th(k) * (HEAD_DIM ** -0.5), th(v), rows, cols, n_ctx)
    h = h.reshape(B, 2, N_HEADS, T, HEAD_DIM)
    y = jax.nn.sigmoid(o) * head_norm_merge(jnp.moveaxis(h[:, 0] + h[:, 1], 1, 2), out_g)
    return y[:, n_ctx:], (y[:, :n_ctx] if need_ctx else None)


def kernel(x, c, ctx, c_ctx, ada_w, ada_b, norm1_g, norm2_g, w_in, w_out, conv_w, conv_g,
           rwkv_w0, rwkv_w2, rwkv_a0, rwkv_a2, rwkv_g2, rwkv_kk, rwkv_ka, rwkv_rk, rwkv_ln_g,
           att_q_g, att_k_g, att_sink, att_out_g, ml_i_b, ml_f_b, ml_out_g,
           peer_wq, peer_keys, peer_u, peer_v):
    B, T, D = x.shape
    ROWS = T // GRID_W
    row = jnp.repeat(jnp.arange(ROWS), GRID_W)
    col = jnp.arange(ROWS * GRID_W) % GRID_W
    for l in range(DEPTH):
        need_ctx = l < DEPTH - 1
        mod = jax.nn.silu(c) @ ada_w[l] + ada_b[l]
        mod_c = jax.nn.silu(c_ctx) @ ada_w[l] + ada_b[l]
        sh1, sc1, gt1, sh2, sc2, gt2 = jnp.split(mod[:, None, :], 6, axis=-1)
        csh1, csc1, cgt1, csh2, csc2, cgt2 = jnp.split(mod_c, 6, axis=-1)

        h = rms_norm(x, norm1_g[l]) * (1 + sc1) + sh1
        hc = rms_norm(ctx, norm1_g[l]) * (1 + csc1) + csh1
        P = jnp.split(_mm3(h, w_in[l]), IN_OFFSETS, axis=-1)
        Pc = jnp.split(_mm3(hc, w_in[l]), IN_OFFSETS, axis=-1)

        y_a = conv_mixer(P[0], P[1], P[2], conv_w[l], conv_g[l])
        y_b, yc_b = rwkv_mixer(P[3:9], Pc[3:9], rwkv_w0[l], rwkv_w2[l], rwkv_a0[l], rwkv_a2[l],
                               rwkv_g2[l], rwkv_kk[l], rwkv_ka[l], rwkv_rk[l], rwkv_ln_g[l], need_ctx)
        q, k, v = attn_project(P[9], P[10], P[11], att_q_g[l], att_k_g[l])
        q, k = rope_2d(q, row, col), rope_2d(k, row, col)
        qc, kc, vc = attn_project(Pc[9], Pc[10], Pc[11], att_q_g[l], att_k_g[l])
        y_c = head_norm_merge(latent_attention(q, k, v, kc, vc, att_sink[l]), att_out_g[l])
        y_d, yc_d = mlstm_mixer(P[12:17], Pc[12:17], ml_i_b[l], ml_f_b[l], ml_out_g[l], need_ctx)

        y = _mm3(jnp.concatenate([t.astype(x.dtype) for t in (y_a, y_b, y_c, y_d)], axis=-1), w_out[l])
        x = x + gt1 * y
        h2 = rms_norm(x, norm2_g[l]) * (1 + sc2) + sh2
        tok = [h2.reshape(B * T, D)]
        if need_ctx:
            yc_a = conv_mixer(Pc[0], Pc[1], Pc[2], conv_w[l], conv_g[l])
            yc_c = head_norm_merge(ctx_attention(qc, kc, vc, att_sink[l]), att_out_g[l])
            yc = _mm3(jnp.concatenate([t.astype(ctx.dtype) for t in (yc_a, yc_b, yc_c, yc_d)], axis=-1), w_out[l])
            ctx = ctx + cgt1 * yc
            hc2 = rms_norm(ctx, norm2_g[l]) * (1 + csc2) + csh2
            tok.append(hc2.reshape(-1, D))
        peer_t = _peer_dense(jnp.concatenate(tok, axis=0) if need_ctx else tok[0],
                             peer_wq[l].astype(BF16),
                             peer_keys[l].reshape(2 * PEER_HEADS, N_KEYS, PEER_HALF).astype(BF16),
                             peer_u[l].astype(BF16), _peer_v_tiles(peer_v[l]))
        x = x + gt2 * peer_t[:, :B * T].T.reshape(B, T, D)
        if need_ctx:
            ctx = ctx + cgt2 * peer_t[:, B * T:].T.reshape(ctx.shape)
    return x
```

```python
import functools
import math

import jax
import jax.numpy as jnp
import numpy as np
from jax import lax
from jax.experimental import pallas as pl
from jax.experimental.pallas import tpu as pltpu

D_MODEL = 1024
DEPTH = 2
GRID_W = 64
N_MIXERS = 4
GROUP_W = D_MODEL // N_MIXERS
HEAD_DIM = 64
N_HEADS = GROUP_W // HEAD_DIM
CONV_K = 3
W_LORA = 16
A_LORA = 16
G_LORA = 32
RWKV_DECAY_SCALE = math.exp(-0.5)
KV_HEADS = 2
Q_PER_KV = N_HEADS // KV_HEADS
KV_W = KV_HEADS * HEAD_DIM
WINDOW = 128
ATT_BLOCK = 128
ATT_SCALE = HEAD_DIM ** -0.5
ROPE_BASE = 10000.0
MLSTM_CHUNK = 128
N_GATE_COLS = 2 * 2 * N_HEADS
PEER_HEADS = 8
N_KEYS = 128
N_EXPERTS = N_KEYS * N_KEYS
PEER_TOPK = 16
PEER_QDIM = 256
PEER_HALF = PEER_QDIM // 2
PEER_BLOCK = 128
EPS = 1e-6
F32 = jnp.float32
BF16 = jnp.bfloat16
IN_SIZES = (GROUP_W, GROUP_W, GROUP_W,
            GROUP_W, GROUP_W, GROUP_W, W_LORA, A_LORA, G_LORA,
            GROUP_W, KV_W, KV_W,
            GROUP_W, GROUP_W, GROUP_W, GROUP_W, N_GATE_COLS)
D_IN = sum(IN_SIZES)
IN_OFFSETS = tuple(int(o) for o in np.cumsum(IN_SIZES)[:-1])

LANE = 128
VMEM_LIMIT_BYTES = 56 * 1024 * 1024


def _mm_kernel(x_ref, w_ref, o_ref):
    o_ref[...] = jnp.dot(x_ref[...].astype(BF16), w_ref[...], preferred_element_type=F32)


def _matmul(x, w, tm=512):
    M, K = x.shape
    N = w.shape[1]
    Np = -(-N // LANE) * LANE
    wb = w.astype(BF16)
    if Np != N:
        wb = jnp.pad(wb, ((0, 0), (0, Np - N)))
    tm = min(tm, M)
    assert M % tm == 0
    out = pl.pallas_call(
        _mm_kernel,
        grid=(M // tm,),
        in_specs=[pl.BlockSpec((tm, K), lambda i: (i, 0)),
                  pl.BlockSpec((K, Np), lambda i: (0, 0))],
        out_specs=pl.BlockSpec((tm, Np), lambda i: (i, 0)),
        out_shape=jax.ShapeDtypeStruct((M, Np), F32),
        compiler_params=pltpu.CompilerParams(
            dimension_semantics=("arbitrary",), vmem_limit_bytes=VMEM_LIMIT_BYTES),
        name="matmul",
    )(x, wb)
    return out[:, :N] if Np != N else out


def _mm3(x, w):
    lead = x.shape[:-1]
    return _matmul(x.reshape(-1, x.shape[-1]), w).reshape(lead + (w.shape[1],))


_NEG_INF = float("-inf")


def _gelu_tanh(x):
    c = math.sqrt(2.0 / math.pi)
    return 0.5 * x * (1.0 + jnp.tanh(c * (x + 0.044715 * (x * x * x))))


_SUBLANES = 8


def _sort_network(n):
    pairs, p = [], 1
    while p < 16:
        k = p
        while k >= 1:
            for j in range(k % p, 16 - k, 2 * k):
                for i in range(min(k, 16 - j - k)):
                    if (i + j) // (2 * p) == (i + j + k) // (2 * p):
                        pairs.append((i + j, i + j + k))
            k //= 2
        p *= 2
    return [(a, b) for a, b in pairs if b < n]


def _top16_values(x):
    v = [x[_SUBLANES * k:_SUBLANES * (k + 1), :] for k in range(x.shape[0] // _SUBLANES)]
    for a, b in _sort_network(len(v)):
        v[a], v[b] = jnp.maximum(v[a], v[b]), jnp.minimum(v[a], v[b])
    sub = lax.broadcasted_iota(jnp.int32, v[0].shape, 0)
    vals = []
    for step in range(PEER_TOPK):
        mx = jnp.max(v[0], axis=0, keepdims=True)
        vals.append(mx)
        remaining = PEER_TOPK - 1 - step
        if remaining == 0:
            break
        hit = sub == jnp.min(jnp.where(v[0] == mx, sub, _SUBLANES), axis=0, keepdims=True)
        for k in range(min(remaining, len(v))):
            v[k] = jnp.where(hit, v[k + 1] if k + 1 < len(v) else _NEG_INF, v[k])
    return vals


def _peer_candidates(sv1, sv2):
    row8 = lax.broadcasted_iota(jnp.int32, (8, sv1.shape[1]), 0)
    blocks = [sv1[0:1, :] + sv2[0:8, :], sv1[0:1, :] + sv2[8:16, :]]
    for a in range(1, 8):
        n_valid = PEER_TOPK // (a + 1)
        blk = sv1[a:a + 1, :] + sv2[0:8, :]
        blocks.append(blk if n_valid >= 8 else jnp.where(row8 < n_valid, blk, _NEG_INF))
    blocks.append(sv1[8:16, :] + sv2[0:1, :])
    return jnp.concatenate(blocks, axis=0)


def _peer_kernel(h_ref, wq_ref, keys_ref, u_ref, vt_ref, o_ref,
                 hbt_ref, s_ref, sv_ref, e1_ref, cnt_ref, e2_ref, rank_ref, act_ref, wg_ref, *, tm, te):
    j = pl.program_id(1)
    n_lg = tm // LANE
    n_pair = n_lg // 2
    n_ib = te // N_KEYS

    @pl.when(j == 0)
    def _prepare():
        hb = h_ref[...].astype(BF16)
        hbt_ref[...] = h_ref[...].T.astype(BF16)
        qb = jnp.dot(hb, wq_ref[...], preferred_element_type=F32).astype(BF16)
        for hp in range(2 * PEER_HEADS):
            s_ref[hp] = lax.dot_general(keys_ref[hp], qb[:, hp * PEER_HALF:(hp + 1) * PEER_HALF],
                                        (((1,), (1,)), ((), ())), preferred_element_type=F32)

        def lane_groups(it):
            return it // n_pair, [pl.ds(pl.multiple_of(((it % n_pair) * 2 + half) * LANE, LANE), LANE)
                                  for half in range(2)]

        def top_body(it, carry):
            h, groups = lane_groups(it)
            for ls in groups:
                sv_ref[2 * h, :, ls] = jnp.concatenate(_top16_values(s_ref[2 * h, :, ls]), axis=0)
                s2 = s_ref[2 * h + 1, :, ls]
                vals = _top16_values(s2)
                sv_ref[2 * h + 1, :, ls] = jnp.concatenate(vals, axis=0)
                rank = jnp.zeros_like(s2)
                for val in vals:
                    rank = rank + jnp.where(val > s2, 1.0, 0.0)
                rank_ref[h, :, ls] = rank.astype(BF16)
            return carry

        lax.fori_loop(0, PEER_HEADS * n_pair, top_body, 0)

        def head_body(it, carry):
            h, groups = lane_groups(it)
            for ls in groups:
                sv1 = sv_ref[2 * h, :, ls]
                sv2 = sv_ref[2 * h + 1, :, ls]
                tv = _top16_values(_peer_candidates(sv1, sv2))
                thr = tv[PEER_TOPK - 1]
                z = jnp.zeros_like(thr)
                for t in tv:
                    z = z + jnp.exp(t - tv[0])
                s1 = s_ref[2 * h, :, ls]
                cnt = jnp.zeros_like(s1)
                for b in range(PEER_TOPK):
                    cnt = cnt + jnp.where(s1 + sv2[b:b + 1, :] >= thr, 1.0, 0.0)
                cnt_ref[h, :, ls] = cnt
                e1_ref[h, :, ls] = jnp.exp(s1 - sv1[0:1, :])
                e2_ref[h, :, ls] = (jnp.exp(s_ref[2 * h + 1, :, ls] - sv2[0:1, :]) / z).astype(BF16)
            return carry

        lax.fori_loop(0, PEER_HEADS * n_pair, head_body, 0)
        o_ref[...] = jnp.zeros_like(o_ref)

    i0 = pl.multiple_of(j * n_ib, 8)
    half_rows = te // 2
    act_ref[...] = jnp.dot(u_ref[...], hbt_ref[...], preferred_element_type=F32)
    for part in range(2):
        for lg in range(n_lg):
            ls = slice(lg * LANE, (lg + 1) * LANE)
            cnt8 = [cnt_ref[h, pl.ds(i0, 8), ls].astype(BF16) for h in range(PEER_HEADS)]
            e18 = [e1_ref[h, pl.ds(i0, 8), ls].astype(BF16) for h in range(PEER_HEADS)]
            for ib in range(part * n_ib // 2, (part + 1) * n_ib // 2):
                w = jnp.zeros((N_KEYS, LANE), BF16)
                for h in range(PEER_HEADS):
                    sel = rank_ref[h, :, ls] < cnt8[h][ib:ib + 1, :]
                    w = w + jnp.where(sel, e2_ref[h, :, ls] * e18[h][ib:ib + 1, :], jnp.zeros((), BF16))
                rs = slice(ib * N_KEYS, (ib + 1) * N_KEYS)
                wg_ref[rs, ls] = w * _gelu_tanh(act_ref[rs, ls]).astype(BF16)
        rows = slice(part * half_rows, (part + 1) * half_rows)
        o_ref[...] += jnp.dot(vt_ref[0, :, rows], wg_ref[rows, :], preferred_element_type=F32)


_PEER_TE = 8 * N_KEYS


def _peer_v_tiles(v):
    E, D = v.shape
    return jnp.transpose(v.astype(BF16).reshape(E // _PEER_TE, _PEER_TE, D), (0, 2, 1))


def _peer_dense(hf, wq_b, keys_b, u_b, vt_b, tm=512):
    M, D = hf.shape
    E = u_b.shape[0]
    te = _PEER_TE
    tm = min(tm, M)
    assert M % tm == 0 and E % te == 0 and tm % (2 * LANE) == 0 and vt_b.shape == (E // te, D, te)
    kern = functools.partial(_peer_kernel, tm=tm, te=te)
    return pl.pallas_call(
        kern,
        grid=(M // tm, E // te),
        in_specs=[pl.BlockSpec((tm, D), lambda i, j: (i, 0)),
                  pl.BlockSpec(wq_b.shape, lambda i, j: (0, 0)),
                  pl.BlockSpec(keys_b.shape, lambda i, j: (0, 0, 0)),
                  pl.BlockSpec((te, D), lambda i, j: (j, 0)),
                  pl.BlockSpec((1, D, te), lambda i, j: (j, 0, 0))],
        out_specs=pl.BlockSpec((D, tm), lambda i, j: (0, i)),
        out_shape=jax.ShapeDtypeStruct((D, M), F32),
        scratch_shapes=[pltpu.VMEM((D, tm), BF16),
                        pltpu.VMEM((2 * PEER_HEADS, N_KEYS, tm), F32),
                        pltpu.VMEM((2 * PEER_HEADS, PEER_TOPK, tm), F32),
                        pltpu.VMEM((PEER_HEADS, N_KEYS, tm), F32),
                        pltpu.VMEM((PEER_HEADS, N_KEYS, tm), F32),
                        pltpu.VMEM((PEER_HEADS, N_KEYS, tm), BF16),
                        pltpu.VMEM((PEER_HEADS, N_KEYS, tm), BF16),
                        pltpu.VMEM((te, tm), F32),
                        pltpu.VMEM((te, tm), BF16)],
        compiler_params=pltpu.CompilerParams(
            dimension_semantics=("arbitrary", "arbitrary"), vmem_limit_bytes=VMEM_LIMIT_BYTES),
        name="peer_dense",
    )(hf, wq_b, keys_b, u_b, vt_b)


_RWKV_UNROLL = 8


def _rwkv_kernel(sh_f_ref, sh_b_ref, d0_ref, d1_ref, y0_ref, y1_ref, s_ref, sa_ref, *, tb, n_batch):
    i = pl.program_id(0)
    n_ch = 2 * n_batch
    n_tiles = tb // _RWKV_UNROLL

    @pl.when(i == 0)
    def _init():
        s_ref[...] = jnp.zeros_like(s_ref)

    lane = lax.broadcasted_iota(jnp.int32, (GROUP_W, GROUP_W), 1)
    sub = lax.broadcasted_iota(jnp.int32, (GROUP_W, GROUP_W), 0)
    seg_ones = jnp.where(lane // HEAD_DIM == sub // HEAD_DIM, 1.0, 0.0).astype(BF16)
    lane_v = lax.broadcasted_iota(jnp.int32, (HEAD_DIM, GROUP_W), 1)
    sub_v = lax.broadcasted_iota(jnp.int32, (HEAD_DIM, GROUP_W), 0)
    eye = (lane_v % HEAD_DIM == sub_v)

    def seg_sum(p):
        return jnp.dot(p.astype(BF16), seg_ones, preferred_element_type=F32)

    def chain_refs(c):
        d, b = divmod(c, n_batch)
        return d, b, (sh_f_ref, d0_ref) if d == 0 else (sh_b_ref, d1_ref)

    def tile_start(d, tt):
        return pl.multiple_of((tt if d == 0 else n_tiles - 1 - tt) * _RWKV_UNROLL, _RWKV_UNROLL)

    def first_kk(c, tt):
        d, b, (sh_ref, _) = chain_refs(c)
        kk8 = sh_ref[0, b, pl.ds(tile_start(d, tt), _RWKV_UNROLL), :]
        r0 = 0 if d == 0 else _RWKV_UNROLL - 1
        return kk8[r0:r0 + 1, :]

    sa_ref[...] = seg_sum(jnp.concatenate([s_ref[c] * first_kk(c, 0) for c in range(n_ch)], axis=0))

    def tile_body(tt, carry):
        rows, t8s, dirs = [], [], []
        for c in range(n_ch):
            d, b, (sh_ref, dr_ref) = chain_refs(c)
            t8 = tile_start(d, tt)
            rows.append([sh_ref[q, b, pl.ds(t8, _RWKV_UNROLL), :] for q in range(3)]
                        + [dr_ref[q, b, pl.ds(t8, _RWKV_UNROLL), :] for q in range(3)])
            t8s.append(t8)
            dirs.append(d)
        kk_next_tile = [first_kk(c, jnp.minimum(tt + 1, n_tiles - 1)) for c in range(n_ch)]

        def step_rows(step):
            return [step if d == 0 else _RWKV_UNROLL - 1 - step for d in dirs]

        vexp = seg_sum(jnp.concatenate([jnp.where(eye, rows[c][2][s:s + 1, :], 0.0)
                                        for c in range(n_ch) for s in range(_RWKV_UNROLL)], axis=0))
        S = [s_ref[c] for c in range(n_ch)]
        sa = [sa_ref[c * HEAD_DIM:(c + 1) * HEAD_DIM, :] for c in range(n_ch)]
        ys = [[None] * _RWKV_UNROLL for _ in range(n_ch)]
        for step in range(_RWKV_UNROLL):
            row = step_rows(step)

            def r_(c, q, rw=None):
                rw = row[c] if rw is None else rw
                return rows[c][q][rw:rw + 1, :]

            if step + 1 < _RWKV_UNROLL:
                kk_next = [r_(c, 0, row[c] + (1 if dirs[c] == 0 else -1)) for c in range(n_ch)]
            else:
                kk_next = kk_next_tile
            prods = []
            for c in range(n_ch):
                v0 = (c * _RWKV_UNROLL + row[c]) * HEAD_DIM
                ahead = S[c] * r_(c, 3) + vexp[v0:v0 + HEAD_DIM, :] * r_(c, 5)
                prods.append(ahead * kk_next[c] - sa[c] * (r_(c, 4) * kk_next[c]))
                S[c] = ahead - sa[c] * r_(c, 4)
            res = seg_sum(jnp.concatenate(prods, axis=0))
            sa = [res[c * HEAD_DIM:(c + 1) * HEAD_DIM, :] for c in range(n_ch)]
            for c in range(n_ch):
                ys[c][row[c]] = S[c] * r_(c, 1)
        ye = seg_sum(jnp.concatenate([ys[c][s] for c in range(n_ch) for s in range(_RWKV_UNROLL)], axis=0))
        for c in range(n_ch):
            for s in range(_RWKV_UNROLL):
                v0 = (c * _RWKV_UNROLL + s) * HEAD_DIM
                ys[c][s] = jnp.sum(jnp.where(eye, ye[v0:v0 + HEAD_DIM, :], 0.0), axis=0, keepdims=True)
        sa_ref[...] = jnp.concatenate(sa, axis=0)
        for c in range(n_ch):
            s_ref[c] = S[c]
            y_ref = y0_ref if dirs[c] == 0 else y1_ref
            y_ref[c % n_batch, pl.ds(t8s[c], _RWKV_UNROLL), :] = jnp.concatenate(ys[c], axis=0)
        return carry

    lax.fori_loop(0, n_tiles, tile_body, 0)


def _rwkv_scan_pallas(shared, dir0, dir1, n_ctx, tb=256):
    _, B, T, C = shared.shape
    assert T % tb == 0 and n_ctx % tb == 0 and tb % _RWKV_UNROLL == 0 and C == GROUP_W
    nblk, ncb = T // tb, n_ctx // tb

    def fwd3(i):
        return (0, 0, i, 0)

    def bwd_blk(i):
        return jnp.where(i < ncb, ncb - 1 - i, nblk - 1 - (i - ncb))

    def bwd3(i):
        return (0, 0, bwd_blk(i), 0)

    kern = functools.partial(_rwkv_kernel, tb=tb, n_batch=B)
    blk = (3, B, tb, C)
    return pl.pallas_call(
        kern,
        grid=(nblk,),
        in_specs=[pl.BlockSpec(blk, fwd3), pl.BlockSpec(blk, bwd3), pl.BlockSpec(blk, fwd3), pl.BlockSpec(blk, bwd3)],
        out_specs=[pl.BlockSpec((B, tb, C), lambda i: (0, i, 0)),
                   pl.BlockSpec((B, tb, C), lambda i: (0, bwd_blk(i), 0))],
        out_shape=[jax.ShapeDtypeStruct((B, T, C), F32)] * 2,
        scratch_shapes=[pltpu.VMEM((2 * B, HEAD_DIM, C), F32), pltpu.VMEM((2 * B * HEAD_DIM, C), F32)],
        compiler_params=pltpu.CompilerParams(
            dimension_semantics=("arbitrary",), vmem_limit_bytes=VMEM_LIMIT_BYTES),
        name="rwkv_scan",
    )(shared, shared, dir0, dir1)


def _mlstm_kernel(q_ref, k_ref, v_ref, row_ref, col_ref, h_ref, c_ref, n_ref, m_ref):
    g = pl.program_id(0)
    L = q_ref.shape[2]

    @pl.when(pl.program_id(1) == 0)
    def _init():
        c_ref[...] = jnp.zeros_like(c_ref)
        n_ref[...] = jnp.zeros_like(n_ref)
        m_ref[...] = jnp.zeros_like(m_ref)

    backward = (g % 2) == 1
    sgn = 1 - 2 * (g % 2)
    tt = lax.broadcasted_iota(jnp.int32, (L, L), 0)
    ss = lax.broadcasted_iota(jnp.int32, (L, L), 1)
    causal = (tt - ss) * sgn >= 0
    nt_dims = (((1,), (1,)), ((), ()))
    for h in range(N_HEADS):
        q = q_ref[0, h]
        k = k_ref[0, h]
        v = v_ref[0, h]
        qb, kb, vb = q.astype(BF16), k.astype(BF16), v.astype(BF16)
        brow = row_ref[0, h, 0, 0:1, :]
        lirow = row_ref[0, h, 0, 1:2, :]
        bcol = col_ref[0, h, 0, :, 0:1]
        licol = col_ref[0, h, 0, :, 1:2]
        m_prev = m_ref[h][:, 0:1]
        C = c_ref[h]
        n = n_ref[h]
        d_intra = jnp.where(causal, bcol - brow + lirow, _NEG_INF)
        d_inter = bcol + m_prev
        m_t = jnp.maximum(jnp.max(d_intra, axis=1, keepdims=True), d_inter)
        w_intra = jnp.exp(d_intra - m_t)
        w_inter = jnp.exp(d_inter - m_t)
        s = lax.dot_general(qb, kb, nt_dims, preferred_element_type=F32) * w_intra
        num = (jnp.dot(s.astype(BF16), vb, preferred_element_type=F32)
               + w_inter * lax.dot_general(qb, C.astype(BF16), nt_dims, preferred_element_type=F32))
        den = jnp.sum(s, axis=1, keepdims=True) + w_inter * jnp.sum(q * n, axis=1, keepdims=True)
        h_ref[0, h] = num / jnp.maximum(jnp.abs(den), jnp.exp(-m_t))
        b_end = jnp.where(backward, brow[:, 0:1], brow[:, L - 1:L])
        d_end = b_end - bcol + licol
        m_new = jnp.maximum(b_end + m_prev, jnp.max(d_end, axis=0, keepdims=True))
        w_end = jnp.exp(d_end - m_new)
        decay = jnp.exp(b_end + m_prev - m_new)
        wv_t = (w_end * v).T.astype(BF16)
        c_ref[h] = decay * C + jnp.dot(wv_t, kb, preferred_element_type=F32)
        n_ref[h] = decay * n + jnp.sum(w_end * k, axis=0, keepdims=True)
        m_ref[h] = jnp.broadcast_to(m_new, (1, LANE))


def _mlstm_scan_pallas(qh, kh, vh, rows, cols, n_ctx):
    B, H, T, Dh = qh.shape
    L = MLSTM_CHUNK
    nc, ncb = T // L, n_ctx // L
    assert T % L == 0 and n_ctx % L == 0 and H == N_HEADS

    def chunk(g, c):
        rev = jnp.where(c < ncb, ncb - 1 - c, nc - 1 - (c - ncb))
        return jnp.where(g % 2 == 0, c, rev)

    qkv_spec = pl.BlockSpec((1, H, L, Dh), lambda g, c: (g // 2, 0, chunk(g, c), 0))
    return pl.pallas_call(
        _mlstm_kernel,
        grid=(2 * B, nc),
        in_specs=[qkv_spec, qkv_spec, qkv_spec,
                  pl.BlockSpec((1, H, 1, 2, L), lambda g, c: (g, 0, chunk(g, c), 0, 0)),
                  pl.BlockSpec((1, H, 1, L, 2), lambda g, c: (g, 0, chunk(g, c), 0, 0))],
        out_specs=pl.BlockSpec((1, H, L, Dh), lambda g, c: (g, 0, chunk(g, c), 0)),
        out_shape=jax.ShapeDtypeStruct((2 * B, H, T, Dh), F32),
        scratch_shapes=[pltpu.VMEM((H, Dh, Dh), F32), pltpu.VMEM((H, 1, Dh), F32), pltpu.VMEM((H, 1, LANE), F32)],
        compiler_params=pltpu.CompilerParams(
            dimension_semantics=("arbitrary", "arbitrary"), vmem_limit_bytes=VMEM_LIMIT_BYTES),
        name="mlstm_scan",
    )(qh, kh, vh, rows, cols)


def _attn_kernel(*refs, n_band, t_total):
    q_ref = refs[0]
    band = refs[1:1 + 2 * n_band]
    kc_ref, vc_ref, sink_ref, o_ref = refs[1 + 2 * n_band:]
    n = pl.program_id(1)
    nt_dims = (((1,), (1,)), ((), ()))
    if n_band:
        qpos = n * ATT_BLOCK + lax.broadcasted_iota(jnp.int32, (ATT_BLOCK, n_band * ATT_BLOCK), 0)
        kpos = (n - 1) * ATT_BLOCK + lax.broadcasted_iota(jnp.int32, (ATT_BLOCK, n_band * ATT_BLOCK), 1)
        mask = (jnp.abs(qpos - kpos) <= WINDOW) & (kpos >= 0) & (kpos < t_total)
    for kvh in range(KV_HEADS):
        kc = kc_ref[0, kvh].astype(BF16)
        vc = vc_ref[0, kvh].astype(BF16)
        if n_band:
            kw = jnp.concatenate([band[j][0, kvh] for j in range(n_band)], axis=0).astype(BF16)
            vw = jnp.concatenate([band[n_band + j][0, kvh] for j in range(n_band)], axis=0).astype(BF16)
        for g in range(Q_PER_KV):
            h = kvh * Q_PER_KV + g
            q = q_ref[0, h].astype(BF16)
            sink = sink_ref[h][:, 0:1]
            s_ctx = lax.dot_general(q, kc, nt_dims, preferred_element_type=F32) * ATT_SCALE
            m = jnp.maximum(jnp.max(s_ctx, axis=1, keepdims=True), sink)
            if n_band:
                s_loc = lax.dot_general(q, kw, nt_dims, preferred_element_type=F32) * ATT_SCALE
                s_loc = jnp.where(mask, s_loc, _NEG_INF)
                m = jnp.maximum(m, jnp.max(s_loc, axis=1, keepdims=True))
            p_ctx = jnp.exp(s_ctx - m)
            den = jnp.sum(p_ctx, axis=1, keepdims=True) + jnp.exp(sink - m)
            o = jnp.dot(p_ctx.astype(BF16), vc, preferred_element_type=F32)
            if n_band:
                p_loc = jnp.exp(s_loc - m)
                den = den + jnp.sum(p_loc, axis=1, keepdims=True)
                o = o + jnp.dot(p_loc.astype(BF16), vw, preferred_element_type=F32)
            o_ref[0, h] = o / den


def _attention_pallas(q, k, v, kc, vc, sink):
    B, H, T, Dh = q.shape
    C = kc.shape[2]
    nb = T // ATT_BLOCK
    assert T % ATT_BLOCK == 0
    n_band = 0 if k is None else 3
    sink_b = jnp.broadcast_to(sink.astype(F32)[:, None, None], (H, 1, LANE))
    band_specs = [pl.BlockSpec((1, KV_HEADS, ATT_BLOCK, Dh),
                               lambda b, n, j=j: (b, 0, jnp.clip(n + j - 1, 0, nb - 1), 0)) for j in range(n_band)]
    ctx_spec = pl.BlockSpec((1, KV_HEADS, C, Dh), lambda b, n: (b, 0, 0, 0))
    kern = functools.partial(_attn_kernel, n_band=n_band, t_total=T)
    band_args = [] if k is None else [k] * 3 + [v] * 3
    return pl.pallas_call(
        kern,
        grid=(B, nb),
        in_specs=[pl.BlockSpec((1, H, ATT_BLOCK, Dh), lambda b, n: (b, 0, n, 0))] + band_specs * 2
                 + [ctx_spec, ctx_spec, pl.BlockSpec((H, 1, LANE), lambda b, n: (0, 0, 0))],
        out_specs=pl.BlockSpec((1, H, ATT_BLOCK, Dh), lambda b, n: (b, 0, n, 0)),
        out_shape=jax.ShapeDtypeStruct((B, H, T, Dh), F32),
        compiler_params=pltpu.CompilerParams(
            dimension_semantics=("arbitrary", "arbitrary"), vmem_limit_bytes=VMEM_LIMIT_BYTES),
        name="attention",
    )(q, *band_args, kc, vc, sink_b)


def rms_norm(x, g):
    xf = x.astype(F32)
    y = xf * lax.rsqrt(jnp.mean(xf * xf, axis=-1, keepdims=True) + EPS)
    return (y * g.astype(F32)).astype(x.dtype)


def heads(t):
    return t.reshape(t.shape[:-1] + (N_HEADS, HEAD_DIM))


def head_norm_merge(y, g):
    return rms_norm(y, g).reshape(y.shape[:-2] + (GROUP_W,))


def rope_2d(x, row, col):
    quarter = HEAD_DIM // 4
    inv = ROPE_BASE ** (-jnp.arange(quarter, dtype=F32) / quarter)
    xf = x.astype(F32)
    extra = (1,) * (x.ndim - 3)

    def rot(xa, pos):
        ang = pos.astype(F32)[:, None] * inv[None, :]
        ang = ang.reshape((1, ang.shape[0]) + extra + (quarter,))
        cos, sin = jnp.cos(ang), jnp.sin(ang)
        x1, x2 = xa[..., :quarter], xa[..., quarter:]
        return jnp.concatenate([x1 * cos - x2 * sin, x2 * cos + x1 * sin], axis=-1)

    half = HEAD_DIM // 2
    return jnp.concatenate([rot(xf[..., :half], row), rot(xf[..., half:], col)], axis=-1).astype(x.dtype)


def conv_mixer(hx, b_gate, c_gate, w, g):
    u = c_gate * hx
    up = jnp.pad(u, ((0, 0), (1, 1), (0, 0)))
    y = b_gate * (w[0] * up[:, :-2] + w[1] * up[:, 1:-1] + w[2] * up[:, 2:])
    return head_norm_merge(heads(y), g)


def rwkv_mixer(lat, ctx, w0, w2, a0, a2, g2, k_k, k_a, r_k, ln_g, need_ctx):
    n_ctx = ctx[0].shape[1]
    r, k, v, xw, xa, xg = (jnp.concatenate([c_, l_], axis=1) for c_, l_ in zip(ctx, lat))
    kk = heads(k * k_k)
    kk = (kk * lax.rsqrt(jnp.sum(kk * kk, axis=-1, keepdims=True) + EPS)).reshape(k.shape)
    g = jax.nn.sigmoid(xg) @ g2
    dirs = []
    for d in range(2):
        decay = jnp.exp(-RWKV_DECAY_SCALE * jax.nn.sigmoid(w0[d] + jnp.tanh(xw) @ w2[d]))
        a = jax.nn.sigmoid(a0[d] + xa @ a2[d])
        dirs.append(jnp.stack([decay, kk * a, k * (1 + (a - 1) * k_a)]))
    y0, y1 = _rwkv_scan_pallas(jnp.stack([kk, r, v]), dirs[0], dirs[1], n_ctx)
    y = rms_norm(heads(y0 + y1), ln_g)
    rh = heads(r)
    bonus = (jnp.sum(rh * heads(dirs[0][2]) * r_k, axis=-1, keepdims=True)
             + jnp.sum(rh * heads(dirs[1][2]) * r_k, axis=-1, keepdims=True)) * heads(v)
    out = (y + bonus).reshape(r.shape) * g
    return out[:, n_ctx:], (out[:, :n_ctx] if need_ctx else None)


def attn_project(q, k, v, q_g, k_g):
    B, T, _ = q.shape
    q = rms_norm(q.reshape(B, T, KV_HEADS, Q_PER_KV, HEAD_DIM), q_g)
    k = rms_norm(k.reshape(B, T, KV_HEADS, HEAD_DIM), k_g)
    v = v.reshape(B, T, KV_HEADS, HEAD_DIM)
    return q, k, v


def _head_major(t):
    B, T = t.shape[:2]
    return jnp.moveaxis(t.reshape(B, T, -1, HEAD_DIM), 2, 1)


def latent_attention(q, k, v, kc, vc, sink):
    o = _attention_pallas(_head_major(q), _head_major(k), _head_major(v), _head_major(kc), _head_major(vc), sink)
    return jnp.moveaxis(o, 1, 2)


def ctx_attention(qc, kc, vc, sink):
    o = _attention_pallas(_head_major(qc), None, None, _head_major(kc), _head_major(vc), sink)
    return jnp.moveaxis(o, 1, 2)


def mlstm_mixer(lat, ctx, i_b, f_b, out_g, need_ctx):
    n_ctx = ctx[0].shape[1]
    q, k, v, o, gates = (jnp.concatenate([c_, l_], axis=1) for c_, l_ in zip(ctx, lat))
    B, T, _ = q.shape
    L = MLSTM_CHUNK

    def th(t):
        return jnp.moveaxis(heads(t.astype(F32)), 2, 1)

    gates = gates.astype(F32).reshape(B, T, 2, 2, N_HEADS) + jnp.stack([i_b, f_b], axis=1).astype(F32)
    gates = jnp.moveaxis(gates, 1, -1)
    logi = gates[:, :, 0].reshape(B, 2, N_HEADS, T // L, L)
    logf = jax.nn.log_sigmoid(gates[:, :, 1]).reshape(B, 2, N_HEADS, T // L, L)
    bcum = jnp.stack([jnp.cumsum(logf[:, 0], axis=-1),
                      jnp.flip(jnp.cumsum(jnp.flip(logf[:, 1], axis=-1), axis=-1), axis=-1)], axis=1)
    rows = jnp.stack([bcum, logi], axis=-2).reshape(B * 2, N_HEADS, T // L, 2, L)
    cols = jnp.stack([bcum, logi], axis=-1).reshape(B * 2, N_HEADS, T // L, L, 2)
    h = _mlstm_scan_pallas(th(q), th(k) * (HEAD_DIM ** -0.5), th(v), rows, cols, n_ctx)
    h = h.reshape(B, 2, N_HEADS, T, HEAD_DIM)
    y = jax.nn.sigmoid(o) * head_norm_merge(jnp.moveaxis(h[:, 0] + h[:, 1], 1, 2), out_g)
    return y[:, n_ctx:], (y[:, :n_ctx] if need_ctx else None)


def kernel(x, c, ctx, c_ctx, ada_w, ada_b, norm1_g, norm2_g, w_in, w_out, conv_w, conv_g,
           rwkv_w0, rwkv_w2, rwkv_a0, rwkv_a2, rwkv_g2, rwkv_kk, rwkv_ka, rwkv_rk, rwkv_ln_g,
           att_q_g, att_k_g, att_sink, att_out_g, ml_i_b, ml_f_b, ml_out_g,
           peer_wq, peer_keys, peer_u, peer_v):
    B, T, D = x.shape
    ROWS = T // GRID_W
    row = jnp.repeat(jnp.arange(ROWS), GRID_W)
    col = jnp.arange(ROWS * GRID_W) % GRID_W
    for l in range(DEPTH):
        need_ctx = l < DEPTH - 1
        mod = jax.nn.silu(c) @ ada_w[l] + ada_b[l]
        mod_c = jax.nn.silu(c_ctx) @ ada_w[l] + ada_b[l]
        sh1, sc1, gt1, sh2, sc2, gt2 = jnp.split(mod[:, None, :], 6, axis=-1)
        csh1, csc1, cgt1, csh2, csc2, cgt2 = jnp.split(mod_c, 6, axis=-1)

        h = rms_norm(x, norm1_g[l]) * (1 + sc1) + sh1
        hc = rms_norm(ctx, norm1_g[l]) * (1 + csc1) + csh1
        P = jnp.split(_mm3(h, w_in[l]), IN_OFFSETS, axis=-1)
        Pc = jnp.split(_mm3(hc, w_in[l]), IN_OFFSETS, axis=-1)

        y_a = conv_mixer(P[0], P[1], P[2], conv_w[l], conv_g[l])
        y_b, yc_b = rwkv_mixer(P[3:9], Pc[3:9], rwkv_w0[l], rwkv_w2[l], rwkv_a0[l], rwkv_a2[l],
                               rwkv_g2[l], rwkv_kk[l], rwkv_ka[l], rwkv_rk[l], rwkv_ln_g[l], need_ctx)
        q, k, v = attn_project(P[9], P[10], P[11], att_q_g[l], att_k_g[l])
        q, k = rope_2d(q, row, col), rope_2d(k, row, col)
        qc, kc, vc = attn_project(Pc[9], Pc[10], Pc[11], att_q_g[l], att_k_g[l])
        y_c = head_norm_merge(latent_attention(q, k, v, kc, vc, att_sink[l]), att_out_g[l])
        y_d, yc_d = mlstm_mixer(P[12:17], Pc[12:17], ml_i_b[l], ml_f_b[l], ml_out_g[l], need_ctx)

        y = _mm3(jnp.concatenate([t.astype(x.dtype) for t in (y_a, y_b, y_c, y_d)], axis=-1), w_out[l])
        x = x + gt1 * y
        h2 = rms_norm(x, norm2_g[l]) * (1 + sc2) + sh2
        tok = [h2.reshape(B * T, D)]
        if need_ctx:
            yc_a = conv_mixer(Pc[0], Pc[1], Pc[2], conv_w[l], conv_g[l])
            yc_c = head_norm_merge(ctx_attention(qc, kc, vc, att_sink[l]), att_out_g[l])
            yc = _mm3(jnp.concatenate([t.astype(ctx.dtype) for t in (yc_a, yc_b, yc_c, yc_d)], axis=-1), w_out[l])
            ctx = ctx + cgt1 * yc
            hc2 = rms_norm(ctx, norm2_g[l]) * (1 + csc2) + csh2
            tok.append(hc2.reshape(-1, D))
        peer_t = _peer_dense(jnp.concatenate(tok, axis=0) if need_ctx else tok[0],
                             peer_wq[l].astype(BF16),
                             peer_keys[l].reshape(2 * PEER_HEADS, N_KEYS, PEER_HALF).astype(BF16),
                             peer_u[l].astype(BF16), _peer_v_tiles(peer_v[l]))
        x = x + gt2 * peer_t[:, :B * T].T.reshape(B, T, D)
        if need_ctx:
            ctx = ctx + cgt2 * peer_t[:, B * T:].T.reshape(ctx.shape)
    return x
```

```python
import functools
import math

import jax
import jax.numpy as jnp
import numpy as np
from jax import lax
from jax.experimental import pallas as pl
from jax.experimental.pallas import tpu as pltpu

D_MODEL = 1024
DEPTH = 2
GRID_W = 64
N_MIXERS = 4
GROUP_W = D_MODEL // N_MIXERS
HEAD_DIM = 64
N_HEADS = GROUP_W // HEAD_DIM
CONV_K = 3
W_LORA = 16
A_LORA = 16
G_LORA = 32
RWKV_DECAY_SCALE = math.exp(-0.5)
KV_HEADS = 2
Q_PER_KV = N_HEADS // KV_HEADS
KV_W = KV_HEADS * HEAD_DIM
WINDOW = 128
ATT_BLOCK = 128
ATT_SCALE = HEAD_DIM ** -0.5
ROPE_BASE = 10000.0
MLSTM_CHUNK = 128
N_GATE_COLS = 2 * 2 * N_HEADS
PEER_HEADS = 8
N_KEYS = 128
N_EXPERTS = N_KEYS * N_KEYS
PEER_TOPK = 16
PEER_QDIM = 256
PEER_HALF = PEER_QDIM // 2
PEER_BLOCK = 128
EPS = 1e-6
F32 = jnp.float32
BF16 = jnp.bfloat16
IN_SIZES = (GROUP_W, GROUP_W, GROUP_W,
            GROUP_W, GROUP_W, GROUP_W, W_LORA, A_LORA, G_LORA,
            GROUP_W, KV_W, KV_W,
            GROUP_W, GROUP_W, GROUP_W, GROUP_W, N_GATE_COLS)
D_IN = sum(IN_SIZES)
IN_OFFSETS = tuple(int(o) for o in np.cumsum(IN_SIZES)[:-1])

LANE = 128
VMEM_LIMIT_BYTES = 56 * 1024 * 1024


def _mm_kernel(x_ref, w_ref, o_ref):
    o_ref[...] = jnp.dot(x_ref[...].astype(BF16), w_ref[...], preferred_element_type=F32)


def _matmul(x, w, tm=512):
    M, K = x.shape
    N = w.shape[1]
    Np = -(-N // LANE) * LANE
    wb = w.astype(BF16)
    if Np != N:
        wb = jnp.pad(wb, ((0, 0), (0, Np - N)))
    tm = min(tm, M)
    assert M % tm == 0
    out = pl.pallas_call(
        _mm_kernel,
        grid=(M // tm,),
        in_specs=[pl.BlockSpec((tm, K), lambda i: (i, 0)),
                  pl.BlockSpec((K, Np), lambda i: (0, 0))],
        out_specs=pl.BlockSpec((tm, Np), lambda i: (i, 0)),
        out_shape=jax.ShapeDtypeStruct((M, Np), F32),
        compiler_params=pltpu.CompilerParams(
            dimension_semantics=("arbitrary",), vmem_limit_bytes=VMEM_LIMIT_BYTES),
        name="matmul",
    )(x, wb)
    return out


def _mm3(x, w):
    lead = x.shape[:-1]
    out = _matmul(x.reshape(-1, x.shape[-1]), w)
    return out.reshape(lead + (out.shape[1],))


_NEG_INF = float("-inf")


def _gelu_tanh(x):
    c = math.sqrt(2.0 / math.pi)
    return 0.5 * x * (1.0 + jnp.tanh(c * (x + 0.044715 * (x * x * x))))


_SUBLANES = 8


def _sort_network(n):
    pairs, p = [], 1
    while p < 16:
        k = p
        while k >= 1:
            for j in range(k % p, 16 - k, 2 * k):
                for i in range(min(k, 16 - j - k)):
                    if (i + j) // (2 * p) == (i + j + k) // (2 * p):
                        pairs.append((i + j, i + j + k))
            k //= 2
        p *= 2
    return [(a, b) for a, b in pairs if b < n]


def _top16_values(x):
    v = [x[_SUBLANES * k:_SUBLANES * (k + 1), :] for k in range(x.shape[0] // _SUBLANES)]
    for a, b in _sort_network(len(v)):
        v[a], v[b] = jnp.maximum(v[a], v[b]), jnp.minimum(v[a], v[b])
    sub = lax.broadcasted_iota(jnp.int32, v[0].shape, 0)
    vals = []
    for step in range(PEER_TOPK):
        mx = jnp.max(v[0], axis=0, keepdims=True)
        vals.append(mx)
        remaining = PEER_TOPK - 1 - step
        if remaining == 0:
            break
        hit = sub == jnp.min(jnp.where(v[0] == mx, sub, _SUBLANES), axis=0, keepdims=True)
        for k in range(min(remaining, len(v))):
            v[k] = jnp.where(hit, v[k + 1] if k + 1 < len(v) else _NEG_INF, v[k])
    return vals


def _peer_candidates(sv1, sv2):
    row8 = lax.broadcasted_iota(jnp.int32, (8, sv1.shape[1]), 0)
    blocks = [sv1[0:1, :] + sv2[0:8, :], sv1[0:1, :] + sv2[8:16, :]]
    for a in range(1, 8):
        n_valid = PEER_TOPK // (a + 1)
        blk = sv1[a:a + 1, :] + sv2[0:8, :]
        blocks.append(blk if n_valid >= 8 else jnp.where(row8 < n_valid, blk, _NEG_INF))
    blocks.append(sv1[8:16, :] + sv2[0:1, :])
    return jnp.concatenate(blocks, axis=0)


def _peer_kernel(h_ref, wq_ref, keys_ref, u_ref, vt_ref, o_ref,
                 hbt_ref, s_ref, sv_ref, e1_ref, cnt_ref, e2_ref, rank_ref, act_ref, wg_ref, *, tm, te):
    j = pl.program_id(1)
    n_lg = tm // LANE
    n_pair = n_lg // 2
    n_ib = te // N_KEYS

    @pl.when(j == 0)
    def _prepare():
        hb = h_ref[...].astype(BF16)
        hbt_ref[...] = h_ref[...].T.astype(BF16)
        qb = jnp.dot(hb, wq_ref[...], preferred_element_type=F32).astype(BF16)
        for hp in range(2 * PEER_HEADS):
            s_ref[hp] = lax.dot_general(keys_ref[hp], qb[:, hp * PEER_HALF:(hp + 1) * PEER_HALF],
                                        (((1,), (1,)), ((), ())), preferred_element_type=F32)

        def lane_groups(it):
            return it // n_pair, [pl.ds(pl.multiple_of(((it % n_pair) * 2 + half) * LANE, LANE), LANE)
                                  for half in range(2)]

        def top_body(it, carry):
            h, groups = lane_groups(it)
            for ls in groups:
                sv_ref[2 * h, :, ls] = jnp.concatenate(_top16_values(s_ref[2 * h, :, ls]), axis=0)
                s2 = s_ref[2 * h + 1, :, ls]
                vals = _top16_values(s2)
                sv_ref[2 * h + 1, :, ls] = jnp.concatenate(vals, axis=0)
                rank = jnp.zeros_like(s2)
                for val in vals:
                    rank = rank + jnp.where(val > s2, 1.0, 0.0)
                rank_ref[h, :, ls] = rank.astype(BF16)
            return carry

        lax.fori_loop(0, PEER_HEADS * n_pair, top_body, 0)

        def head_body(it, carry):
            h, groups = lane_groups(it)
            for ls in groups:
                sv1 = sv_ref[2 * h, :, ls]
                sv2 = sv_ref[2 * h + 1, :, ls]
                tv = _top16_values(_peer_candidates(sv1, sv2))
                thr = tv[PEER_TOPK - 1]
                z = jnp.zeros_like(thr)
                for t in tv:
                    z = z + jnp.exp(t - tv[0])
                s1 = s_ref[2 * h, :, ls]
                cnt = jnp.zeros_like(s1)
                for b in range(PEER_TOPK):
                    cnt = cnt + jnp.where(s1 + sv2[b:b + 1, :] >= thr, 1.0, 0.0)
                cnt_ref[h, :, ls] = cnt
                e1_ref[h, :, ls] = jnp.exp(s1 - sv1[0:1, :])
                e2_ref[h, :, ls] = (jnp.exp(s_ref[2 * h + 1, :, ls] - sv2[0:1, :]) / z).astype(BF16)
            return carry

        lax.fori_loop(0, PEER_HEADS * n_pair, head_body, 0)
        o_ref[...] = jnp.zeros_like(o_ref)

    i0 = pl.multiple_of(j * n_ib, 8)
    half_rows = te // 2
    act_ref[...] = jnp.dot(u_ref[...], hbt_ref[...], preferred_element_type=F32)
    for part in range(2):
        for lg in range(n_lg):
            ls = slice(lg * LANE, (lg + 1) * LANE)
            cnt8 = [cnt_ref[h, pl.ds(i0, 8), ls].astype(BF16) for h in range(PEER_HEADS)]
            e18 = [e1_ref[h, pl.ds(i0, 8), ls].astype(BF16) for h in range(PEER_HEADS)]
            for ib in range(part * n_ib // 2, (part + 1) * n_ib // 2):
                w = jnp.zeros((N_KEYS, LANE), BF16)
                for h in range(PEER_HEADS):
                    sel = rank_ref[h, :, ls] < cnt8[h][ib:ib + 1, :]
                    w = w + jnp.where(sel, e2_ref[h, :, ls] * e18[h][ib:ib + 1, :], jnp.zeros((), BF16))
                rs = slice(ib * N_KEYS, (ib + 1) * N_KEYS)
                wg_ref[rs, ls] = w * _gelu_tanh(act_ref[rs, ls].astype(BF16))
        rows = slice(part * half_rows, (part + 1) * half_rows)
        o_ref[...] += jnp.dot(vt_ref[0, :, rows], wg_ref[rows, :], preferred_element_type=F32)


_PEER_TE = 8 * N_KEYS


def _peer_v_tiles(v):
    E, D = v.shape
    return jnp.transpose(v.astype(BF16).reshape(E // _PEER_TE, _PEER_TE, D), (0, 2, 1))


def _peer_dense(hf, wq_b, keys_b, u_b, vt_b, tm=512):
    M, D = hf.shape
    E = u_b.shape[0]
    te = _PEER_TE
    tm = min(tm, M)
    assert M % tm == 0 and E % te == 0 and tm % (2 * LANE) == 0 and vt_b.shape == (E // te, D, te)
    kern = functools.partial(_peer_kernel, tm=tm, te=te)
    return pl.pallas_call(
        kern,
        grid=(M // tm, E // te),
        in_specs=[pl.BlockSpec((tm, D), lambda i, j: (i, 0)),
                  pl.BlockSpec(wq_b.shape, lambda i, j: (0, 0)),
                  pl.BlockSpec(keys_b.shape, lambda i, j: (0, 0, 0)),
                  pl.BlockSpec((te, D), lambda i, j: (j, 0)),
                  pl.BlockSpec((1, D, te), lambda i, j: (j, 0, 0))],
        out_specs=pl.BlockSpec((D, tm), lambda i, j: (0, i)),
        out_shape=jax.ShapeDtypeStruct((D, M), F32),
        scratch_shapes=[pltpu.VMEM((D, tm), BF16),
                        pltpu.VMEM((2 * PEER_HEADS, N_KEYS, tm), F32),
                        pltpu.VMEM((2 * PEER_HEADS, PEER_TOPK, tm), F32),
                        pltpu.VMEM((PEER_HEADS, N_KEYS, tm), F32),
                        pltpu.VMEM((PEER_HEADS, N_KEYS, tm), F32),
                        pltpu.VMEM((PEER_HEADS, N_KEYS, tm), BF16),
                        pltpu.VMEM((PEER_HEADS, N_KEYS, tm), BF16),
                        pltpu.VMEM((te, tm), F32),
                        pltpu.VMEM((te, tm), BF16)],
        compiler_params=pltpu.CompilerParams(
            dimension_semantics=("arbitrary", "arbitrary"), vmem_limit_bytes=VMEM_LIMIT_BYTES),
        name="peer_dense",
    )(hf, wq_b, keys_b, u_b, vt_b)


_RWKV_UNROLL = 8


def _rwkv_kernel(sh_f_ref, sh_b_ref, d0_ref, d1_ref, y0_ref, y1_ref, s_ref, sa_ref, *, tb, n_batch):
    i = pl.program_id(0)
    n_ch = 2 * n_batch
    n_tiles = tb // _RWKV_UNROLL

    @pl.when(i == 0)
    def _init():
        s_ref[...] = jnp.zeros_like(s_ref)

    lane = lax.broadcasted_iota(jnp.int32, (GROUP_W, GROUP_W), 1)
    sub = lax.broadcasted_iota(jnp.int32, (GROUP_W, GROUP_W), 0)
    seg_ones = jnp.where(lane // HEAD_DIM == sub // HEAD_DIM, 1.0, 0.0).astype(BF16)
    lane_v = lax.broadcasted_iota(jnp.int32, (HEAD_DIM, GROUP_W), 1)
    sub_v = lax.broadcasted_iota(jnp.int32, (HEAD_DIM, GROUP_W), 0)
    eye = (lane_v % HEAD_DIM == sub_v)

    def seg_sum(p):
        return jnp.dot(p.astype(BF16), seg_ones, preferred_element_type=F32)

    def chain_refs(c):
        d, b = divmod(c, n_batch)
        return d, b, (sh_f_ref, d0_ref) if d == 0 else (sh_b_ref, d1_ref)

    def tile_start(d, tt):
        return pl.multiple_of((tt if d == 0 else n_tiles - 1 - tt) * _RWKV_UNROLL, _RWKV_UNROLL)

    def first_kk(c, tt):
        d, b, (sh_ref, _) = chain_refs(c)
        kk8 = sh_ref[0, b, pl.ds(tile_start(d, tt), _RWKV_UNROLL), :]
        r0 = 0 if d == 0 else _RWKV_UNROLL - 1
        return kk8[r0:r0 + 1, :]

    sa_ref[...] = seg_sum(jnp.concatenate([s_ref[c] * first_kk(c, 0) for c in range(n_ch)], axis=0))

    def tile_body(tt, carry):
        rows, t8s, dirs = [], [], []
        for c in range(n_ch):
            d, b, (sh_ref, dr_ref) = chain_refs(c)
            t8 = tile_start(d, tt)
            rows.append([sh_ref[q, b, pl.ds(t8, _RWKV_UNROLL), :] for q in range(3)]
                        + [dr_ref[q, b, pl.ds(t8, _RWKV_UNROLL), :] for q in range(3)])
            t8s.append(t8)
            dirs.append(d)
        kk_next_tile = [first_kk(c, jnp.minimum(tt + 1, n_tiles - 1)) for c in range(n_ch)]

        def step_rows(step):
            return [step if d == 0 else _RWKV_UNROLL - 1 - step for d in dirs]

        vexp = seg_sum(jnp.concatenate([jnp.where(eye, rows[c][2][s:s + 1, :], 0.0)
                                        for c in range(n_ch) for s in range(_RWKV_UNROLL)], axis=0))
        S = [s_ref[c] for c in range(n_ch)]
        sa = [sa_ref[c * HEAD_DIM:(c + 1) * HEAD_DIM, :] for c in range(n_ch)]
        ys = [[None] * _RWKV_UNROLL for _ in range(n_ch)]
        for step in range(_RWKV_UNROLL):
            row = step_rows(step)

            def r_(c, q, rw=None):
                rw = row[c] if rw is None else rw
                return rows[c][q][rw:rw + 1, :]

            if step + 1 < _RWKV_UNROLL:
                kk_next = [r_(c, 0, row[c] + (1 if dirs[c] == 0 else -1)) for c in range(n_ch)]
            else:
                kk_next = kk_next_tile
            prods = []
            for c in range(n_ch):
                v0 = (c * _RWKV_UNROLL + row[c]) * HEAD_DIM
                ahead = S[c] * r_(c, 3) + vexp[v0:v0 + HEAD_DIM, :] * r_(c, 5)
                prods.append(ahead * kk_next[c] - sa[c] * (r_(c, 4) * kk_next[c]))
                S[c] = ahead - sa[c] * r_(c, 4)
            res = seg_sum(jnp.concatenate(prods, axis=0))
            sa = [res[c * HEAD_DIM:(c + 1) * HEAD_DIM, :] for c in range(n_ch)]
            for c in range(n_ch):
                ys[c][row[c]] = S[c] * r_(c, 1)
        ye = seg_sum(jnp.concatenate([ys[c][s] for c in range(n_ch) for s in range(_RWKV_UNROLL)], axis=0))
        for c in range(n_ch):
            for s in range(_RWKV_UNROLL):
                v0 = (c * _RWKV_UNROLL + s) * HEAD_DIM
                ys[c][s] = jnp.sum(jnp.where(eye, ye[v0:v0 + HEAD_DIM, :], 0.0), axis=0, keepdims=True)
        sa_ref[...] = jnp.concatenate(sa, axis=0)
        for c in range(n_ch):
            s_ref[c] = S[c]
            y_ref = y0_ref if dirs[c] == 0 else y1_ref
            y_ref[c % n_batch, pl.ds(t8s[c], _RWKV_UNROLL), :] = jnp.concatenate(ys[c], axis=0)
        return carry

    lax.fori_loop(0, n_tiles, tile_body, 0)


def _rwkv_scan_pallas(shared, dir0, dir1, n_ctx, tb=256):
    _, B, T, C = shared.shape
    assert T % tb == 0 and n_ctx % tb == 0 and tb % _RWKV_UNROLL == 0 and C == GROUP_W
    nblk, ncb = T // tb, n_ctx // tb

    def fwd3(i):
        return (0, 0, i, 0)

    def bwd_blk(i):
        return jnp.where(i < ncb, ncb - 1 - i, nblk - 1 - (i - ncb))

    def bwd3(i):
        return (0, 0, bwd_blk(i), 0)

    kern = functools.partial(_rwkv_kernel, tb=tb, n_batch=B)
    blk = (3, B, tb, C)
    return pl.pallas_call(
        kern,
        grid=(nblk,),
        in_specs=[pl.BlockSpec(blk, fwd3), pl.BlockSpec(blk, bwd3), pl.BlockSpec(blk, fwd3), pl.BlockSpec(blk, bwd3)],
        out_specs=[pl.BlockSpec((B, tb, C), lambda i: (0, i, 0)),
                   pl.BlockSpec((B, tb, C), lambda i: (0, bwd_blk(i), 0))],
        out_shape=[jax.ShapeDtypeStruct((B, T, C), F32)] * 2,
        scratch_shapes=[pltpu.VMEM((2 * B, HEAD_DIM, C), F32), pltpu.VMEM((2 * B * HEAD_DIM, C), F32)],
        compiler_params=pltpu.CompilerParams(
            dimension_semantics=("arbitrary",), vmem_limit_bytes=VMEM_LIMIT_BYTES),
        name="rwkv_scan",
    )(shared, shared, dir0, dir1)


def _mlstm_kernel(q_ref, k_ref, v_ref, row_ref, col_ref, h_ref, c_ref, n_ref, m_ref):
    g = pl.program_id(0)
    L = q_ref.shape[2]

    @pl.when(pl.program_id(1) == 0)
    def _init():
        c_ref[...] = jnp.zeros_like(c_ref)
        n_ref[...] = jnp.zeros_like(n_ref)
        m_ref[...] = jnp.zeros_like(m_ref)

    backward = (g % 2) == 1
    sgn = 1 - 2 * (g % 2)
    tt = lax.broadcasted_iota(jnp.int32, (L, L), 0)
    ss = lax.broadcasted_iota(jnp.int32, (L, L), 1)
    causal = (tt - ss) * sgn >= 0
    nt_dims = (((1,), (1,)), ((), ()))
    for h in range(N_HEADS):
        q = q_ref[0, h]
        k = k_ref[0, h]
        v = v_ref[0, h]
        qb, kb, vb = q.astype(BF16), k.astype(BF16), v.astype(BF16)
        brow = row_ref[0, h, 0, 0:1, :]
        lirow = row_ref[0, h, 0, 1:2, :]
        bcol = col_ref[0, h, 0, :, 0:1]
        licol = col_ref[0, h, 0, :, 1:2]
        m_prev = m_ref[h][:, 0:1]
        C = c_ref[h]
        n = n_ref[h]
        d_intra = jnp.where(causal, bcol - brow + lirow, _NEG_INF)
        d_inter = bcol + m_prev
        m_t = jnp.maximum(jnp.max(d_intra, axis=1, keepdims=True), d_inter)
        w_intra = jnp.exp(d_intra - m_t)
        w_inter = jnp.exp(d_inter - m_t)
        s = lax.dot_general(qb, kb, nt_dims, preferred_element_type=F32) * w_intra
        num = (jnp.dot(s.astype(BF16), vb, preferred_element_type=F32)
               + w_inter * lax.dot_general(qb, C.astype(BF16), nt_dims, preferred_element_type=F32))
        den = jnp.sum(s, axis=1, keepdims=True) + w_inter * jnp.sum(q * n, axis=1, keepdims=True)
        h_ref[0, h] = num / jnp.maximum(jnp.abs(den), jnp.exp(-m_t))
        b_end = jnp.where(backward, brow[:, 0:1], brow[:, L - 1:L])
        d_end = b_end - bcol + licol
        m_new = jnp.maximum(b_end + m_prev, jnp.max(d_end, axis=0, keepdims=True))
        w_end = jnp.exp(d_end - m_new)
        decay = jnp.exp(b_end + m_prev - m_new)
        wv_t = (w_end * v).T.astype(BF16)
        c_ref[h] = decay * C + jnp.dot(wv_t, kb, preferred_element_type=F32)
        n_ref[h] = decay * n + jnp.sum(w_end * k, axis=0, keepdims=True)
        m_ref[h] = jnp.broadcast_to(m_new, (1, LANE))


def _mlstm_scan_pallas(qh, kh, vh, rows, cols, n_ctx):
    B, H, T, Dh = qh.shape
    L = MLSTM_CHUNK
    nc, ncb = T // L, n_ctx // L
    assert T % L == 0 and n_ctx % L == 0 and H == N_HEADS

    def chunk(g, c):
        rev = jnp.where(c < ncb, ncb - 1 - c, nc - 1 - (c - ncb))
        return jnp.where(g % 2 == 0, c, rev)

    qkv_spec = pl.BlockSpec((1, H, L, Dh), lambda g, c: (g // 2, 0, chunk(g, c), 0))
    return pl.pallas_call(
        _mlstm_kernel,
        grid=(2 * B, nc),
        in_specs=[qkv_spec, qkv_spec, qkv_spec,
                  pl.BlockSpec((1, H, 1, 2, L), lambda g, c: (g, 0, chunk(g, c), 0, 0)),
                  pl.BlockSpec((1, H, 1, L, 2), lambda g, c: (g, 0, chunk(g, c), 0, 0))],
        out_specs=pl.BlockSpec((1, H, L, Dh), lambda g, c: (g, 0, chunk(g, c), 0)),
        out_shape=jax.ShapeDtypeStruct((2 * B, H, T, Dh), F32),
        scratch_shapes=[pltpu.VMEM((H, Dh, Dh), F32), pltpu.VMEM((H, 1, Dh), F32), pltpu.VMEM((H, 1, LANE), F32)],
        compiler_params=pltpu.CompilerParams(
            dimension_semantics=("arbitrary", "arbitrary"), vmem_limit_bytes=VMEM_LIMIT_BYTES),
        name="mlstm_scan",
    )(qh, kh, vh, rows, cols)


def _attn_kernel(*refs, n_band, t_total):
    q_ref = refs[0]
    band = refs[1:1 + 2 * n_band]
    kc_ref, vc_ref, sink_ref, o_ref = refs[1 + 2 * n_band:]
    n = pl.program_id(1)
    nt_dims = (((1,), (1,)), ((), ()))
    if n_band:
        qpos = n * ATT_BLOCK + lax.broadcasted_iota(jnp.int32, (ATT_BLOCK, n_band * ATT_BLOCK), 0)
        kpos = (n - 1) * ATT_BLOCK + lax.broadcasted_iota(jnp.int32, (ATT_BLOCK, n_band * ATT_BLOCK), 1)
        mask = (jnp.abs(qpos - kpos) <= WINDOW) & (kpos >= 0) & (kpos < t_total)
    for kvh in range(KV_HEADS):
        kc = kc_ref[0, kvh].astype(BF16)
        vc = vc_ref[0, kvh].astype(BF16)
        if n_band:
            kw = jnp.concatenate([band[j][0, kvh] for j in range(n_band)], axis=0).astype(BF16)
            vw = jnp.concatenate([band[n_band + j][0, kvh] for j in range(n_band)], axis=0).astype(BF16)
        for g in range(Q_PER_KV):
            h = kvh * Q_PER_KV + g
            q = q_ref[0, h].astype(BF16)
            sink = sink_ref[h][:, 0:1]
            s_ctx = lax.dot_general(q, kc, nt_dims, preferred_element_type=F32) * ATT_SCALE
            m = jnp.maximum(jnp.max(s_ctx, axis=1, keepdims=True), sink)
            if n_band:
                s_loc = lax.dot_general(q, kw, nt_dims, preferred_element_type=F32) * ATT_SCALE
                s_loc = jnp.where(mask, s_loc, _NEG_INF)
                m = jnp.maximum(m, jnp.max(s_loc, axis=1, keepdims=True))
            p_ctx = jnp.exp(s_ctx - m)
            den = jnp.sum(p_ctx, axis=1, keepdims=True) + jnp.exp(sink - m)
            o = jnp.dot(p_ctx.astype(BF16), vc, preferred_element_type=F32)
            if n_band:
                p_loc = jnp.exp(s_loc - m)
                den = den + jnp.sum(p_loc, axis=1, keepdims=True)
                o = o + jnp.dot(p_loc.astype(BF16), vw, preferred_element_type=F32)
            o_ref[0, h] = o / den


def _attention_pallas(q, k, v, kc, vc, sink):
    B, H, T, Dh = q.shape
    C = kc.shape[2]
    nb = T // ATT_BLOCK
    assert T % ATT_BLOCK == 0
    n_band = 0 if k is None else 3
    sink_b = jnp.broadcast_to(sink.astype(F32)[:, None, None], (H, 1, LANE))
    band_specs = [pl.BlockSpec((1, KV_HEADS, ATT_BLOCK, Dh),
                               lambda b, n, j=j: (b, 0, jnp.clip(n + j - 1, 0, nb - 1), 0)) for j in range(n_band)]
    ctx_spec = pl.BlockSpec((1, KV_HEADS, C, Dh), lambda b, n: (b, 0, 0, 0))
    kern = functools.partial(_attn_kernel, n_band=n_band, t_total=T)
    band_args = [] if k is None else [k] * 3 + [v] * 3
    return pl.pallas_call(
        kern,
        grid=(B, nb),
        in_specs=[pl.BlockSpec((1, H, ATT_BLOCK, Dh), lambda b, n: (b, 0, n, 0))] + band_specs * 2
                 + [ctx_spec, ctx_spec, pl.BlockSpec((H, 1, LANE), lambda b, n: (0, 0, 0))],
        out_specs=pl.BlockSpec((1, H, ATT_BLOCK, Dh), lambda b, n: (b, 0, n, 0)),
        out_shape=jax.ShapeDtypeStruct((B, H, T, Dh), F32),
        compiler_params=pltpu.CompilerParams(
            dimension_semantics=("arbitrary", "arbitrary"), vmem_limit_bytes=VMEM_LIMIT_BYTES),
        name="attention",
    )(q, *band_args, kc, vc, sink_b)


def rms_norm(x, g):
    xf = x.astype(F32)
    y = xf * lax.rsqrt(jnp.mean(xf * xf, axis=-1, keepdims=True) + EPS)
    return (y * g.astype(F32)).astype(x.dtype)


def heads(t):
    return t.reshape(t.shape[:-1] + (N_HEADS, HEAD_DIM))


def head_norm_merge(y, g):
    return rms_norm(y, g).reshape(y.shape[:-2] + (GROUP_W,))


def rope_2d(x, row, col):
    quarter = HEAD_DIM // 4
    inv = ROPE_BASE ** (-jnp.arange(quarter, dtype=F32) / quarter)
    xf = x.astype(F32)
    extra = (1,) * (x.ndim - 3)

    def rot(xa, pos):
        ang = pos.astype(F32)[:, None] * inv[None, :]
        ang = ang.reshape((1, ang.shape[0]) + extra + (quarter,))
        cos, sin = jnp.cos(ang), jnp.sin(ang)
        x1, x2 = xa[..., :quarter], xa[..., quarter:]
        return jnp.concatenate([x1 * cos - x2 * sin, x2 * cos + x1 * sin], axis=-1)

    half = HEAD_DIM // 2
    return jnp.concatenate([rot(xf[..., :half], row), rot(xf[..., half:], col)], axis=-1).astype(x.dtype)


def conv_mixer(hx, b_gate, c_gate, w, g):
    u = c_gate * hx
    up = jnp.pad(u, ((0, 0), (1, 1), (0, 0)))
    y = b_gate * (w[0] * up[:, :-2] + w[1] * up[:, 1:-1] + w[2] * up[:, 2:])
    return head_norm_merge(heads(y), g)


def rwkv_mixer(seq, n_ctx, w0, w2, a0, a2, g2, k_k, k_a, r_k, ln_g, need_ctx):
    r, k, v, xw, xa, xg = seq
    kk = heads(k * k_k)
    kk = (kk * lax.rsqrt(jnp.sum(kk * kk, axis=-1, keepdims=True) + EPS)).reshape(k.shape)
    g = jax.nn.sigmoid(xg) @ g2
    dirs = []
    for d in range(2):
        decay = jnp.exp(-RWKV_DECAY_SCALE * jax.nn.sigmoid(w0[d] + jnp.tanh(xw) @ w2[d]))
        a = jax.nn.sigmoid(a0[d] + xa @ a2[d])
        dirs.append(jnp.stack([decay, kk * a, k * (1 + (a - 1) * k_a)]))
    y0, y1 = _rwkv_scan_pallas(jnp.stack([kk, r, v]), dirs[0], dirs[1], n_ctx)
    y = rms_norm(heads(y0 + y1), ln_g)
    rh = heads(r)
    bonus = (jnp.sum(rh * heads(dirs[0][2]) * r_k, axis=-1, keepdims=True)
             + jnp.sum(rh * heads(dirs[1][2]) * r_k, axis=-1, keepdims=True)) * heads(v)
    out = (y + bonus).reshape(r.shape) * g
    return out[:, n_ctx:], (out[:, :n_ctx] if need_ctx else None)


def attn_project(q, k, v, q_g, k_g):
    B, T, _ = q.shape
    q = rms_norm(q.reshape(B, T, KV_HEADS, Q_PER_KV, HEAD_DIM), q_g)
    k = rms_norm(k.reshape(B, T, KV_HEADS, HEAD_DIM), k_g)
    v = v.reshape(B, T, KV_HEADS, HEAD_DIM)
    return q, k, v


def _head_major(t):
    B, T = t.shape[:2]
    return jnp.moveaxis(t.reshape(B, T, -1, HEAD_DIM), 2, 1)


def latent_attention(q, k, v, kc, vc, sink):
    o = _attention_pallas(_head_major(q), _head_major(k), _head_major(v), _head_major(kc), _head_major(vc), sink)
    return jnp.moveaxis(o, 1, 2)


def ctx_attention(qc, kc, vc, sink):
    o = _attention_pallas(_head_major(qc), None, None, _head_major(kc), _head_major(vc), sink)
    return jnp.moveaxis(o, 1, 2)


def mlstm_mixer(seq, n_ctx, i_b, f_b, out_g, need_ctx):
    q, k, v, o, gates = seq
    B, T, _ = q.shape
    L = MLSTM_CHUNK

    def th(t):
        return jnp.moveaxis(heads(t.astype(F32)), 2, 1)

    gates = gates.astype(F32).reshape(B, T, 2, 2, N_HEADS) + jnp.stack([i_b, f_b], axis=1).astype(F32)
    gates = jnp.moveaxis(gates, 1, -1)
    logi = gates[:, :, 0].reshape(B, 2, N_HEADS, T // L, L)
    logf = jax.nn.log_sigmoid(gates[:, :, 1]).reshape(B, 2, N_HEADS, T // L, L)
    bcum = jnp.stack([jnp.cumsum(logf[:, 0], axis=-1),
                      jnp.flip(jnp.cumsum(jnp.flip(logf[:, 1], axis=-1), axis=-1), axis=-1)], axis=1)
    rows = jnp.stack([bcum, logi], axis=-2).reshape(B * 2, N_HEADS, T // L, 2, L)
    cols = jnp.stack([bcum, logi], axis=-1).reshape(B * 2, N_HEADS, T // L, L, 2)
    h = _mlstm_scan_pallas(th(q), th(k) * (HEAD_DIM ** -0.5), th(v), rows, cols, n_ctx)
    h = h.reshape(B, 2, N_HEADS, T, HEAD_DIM)
    y = jax.nn.sigmoid(o) * head_norm_merge(jnp.moveaxis(h[:, 0] + h[:, 1], 1, 2), out_g)
    return y[:, n_ctx:], (y[:, :n_ctx] if need_ctx else None)


def kernel(x, c, ctx, c_ctx, ada_w, ada_b, norm1_g, norm2_g, w_in, w_out, conv_w, conv_g,
           rwkv_w0, rwkv_w2, rwkv_a0, rwkv_a2, rwkv_g2, rwkv_kk, rwkv_ka, rwkv_rk, rwkv_ln_g,
           att_q_g, att_k_g, att_sink, att_out_g, ml_i_b, ml_f_b, ml_out_g,
           peer_wq, peer_keys, peer_u, peer_v):
    B, T, D = x.shape
    n_ctx = ctx.shape[1]
    ROWS = T // GRID_W
    row = jnp.repeat(jnp.arange(ROWS), GRID_W)
    col = jnp.arange(ROWS * GRID_W) % GRID_W
    for l in range(DEPTH):
        need_ctx = l < DEPTH - 1
        mod = jax.nn.silu(c) @ ada_w[l] + ada_b[l]
        mod_c = jax.nn.silu(c_ctx) @ ada_w[l] + ada_b[l]
        sh1, sc1, gt1, sh2, sc2, gt2 = jnp.split(mod[:, None, :], 6, axis=-1)
        csh1, csc1, cgt1, csh2, csc2, cgt2 = jnp.split(mod_c, 6, axis=-1)

        h = rms_norm(x, norm1_g[l]) * (1 + sc1) + sh1
        hc = rms_norm(ctx, norm1_g[l]) * (1 + csc1) + csh1
        p_all = _mm3(jnp.concatenate([hc, h], axis=1), w_in[l])
        offs = (0,) + IN_OFFSETS
        S_ = [p_all[..., o_:o_ + n_] for o_, n_ in zip(offs, IN_SIZES)]
        P = [t[:, n_ctx:] for t in S_]
        Pc = [t[:, :n_ctx] for t in S_]

        y_a = conv_mixer(P[0], P[1], P[2], conv_w[l], conv_g[l])
        y_b, yc_b = rwkv_mixer(S_[3:9], n_ctx, rwkv_w0[l], rwkv_w2[l], rwkv_a0[l], rwkv_a2[l],
                               rwkv_g2[l], rwkv_kk[l], rwkv_ka[l], rwkv_rk[l], rwkv_ln_g[l], need_ctx)
        q, k, v = attn_project(P[9], P[10], P[11], att_q_g[l], att_k_g[l])
        q, k = rope_2d(q, row, col), rope_2d(k, row, col)
        qc, kc, vc = attn_project(Pc[9], Pc[10], Pc[11], att_q_g[l], att_k_g[l])
        y_c = head_norm_merge(latent_attention(q, k, v, kc, vc, att_sink[l]), att_out_g[l])
        y_d, yc_d = mlstm_mixer(S_[12:17], n_ctx, ml_i_b[l], ml_f_b[l], ml_out_g[l], need_ctx)

        y = _mm3(jnp.concatenate([t.astype(x.dtype) for t in (y_a, y_b, y_c, y_d)], axis=-1), w_out[l])
        x = x + gt1 * y
        h2 = rms_norm(x, norm2_g[l]) * (1 + sc2) + sh2
        tok = [h2.reshape(B * T, D)]
        if need_ctx:
            yc_a = conv_mixer(Pc[0], Pc[1], Pc[2], conv_w[l], conv_g[l])
            yc_c = head_norm_merge(ctx_attention(qc, kc, vc, att_sink[l]), att_out_g[l])
            yc = _mm3(jnp.concatenate([t.astype(ctx.dtype) for t in (yc_a, yc_b, yc_c, yc_d)], axis=-1), w_out[l])
            ctx = ctx + cgt1 * yc
            hc2 = rms_norm(ctx, norm2_g[l]) * (1 + csc2) + csh2
            tok.append(hc2.reshape(-1, D))
        peer_t = _peer_dense(jnp.concatenate(tok, axis=0) if need_ctx else tok[0],
                             peer_wq[l].astype(BF16),
                             peer_keys[l].reshape(2 * PEER_HEADS, N_KEYS, PEER_HALF).astype(BF16),
                             peer_u[l].astype(BF16), _peer_v_tiles(peer_v[l]))
        x = x + gt2 * peer_t[:, :B * T].T.reshape(B, T, D)
        if need_ctx:
            ctx = ctx + cgt2 * peer_t[:, B * T:].T.reshape(ctx.shape)
    return x
```

```python
import functools
import math

import jax
import jax.numpy as jnp
import numpy as np
from jax import lax
from jax.experimental import pallas as pl
from jax.experimental.pallas import tpu as pltpu

D_MODEL = 1024
DEPTH = 2
GRID_W = 64
N_MIXERS = 4
GROUP_W = D_MODEL // N_MIXERS
HEAD_DIM = 64
N_HEADS = GROUP_W // HEAD_DIM
CONV_K = 3
W_LORA = 16
A_LORA = 16
G_LORA = 32
RWKV_DECAY_SCALE = math.exp(-0.5)
KV_HEADS = 2
Q_PER_KV = N_HEADS // KV_HEADS
KV_W = KV_HEADS * HEAD_DIM
WINDOW = 128
ATT_BLOCK = 128
ATT_SCALE = HEAD_DIM ** -0.5
ROPE_BASE = 10000.0
MLSTM_CHUNK = 128
N_GATE_COLS = 2 * 2 * N_HEADS
PEER_HEADS = 8
N_KEYS = 128
N_EXPERTS = N_KEYS * N_KEYS
PEER_TOPK = 16
PEER_QDIM = 256
PEER_HALF = PEER_QDIM // 2
PEER_BLOCK = 128
EPS = 1e-6
F32 = jnp.float32
BF16 = jnp.bfloat16
IN_SIZES = (GROUP_W, GROUP_W, GROUP_W,
            GROUP_W, GROUP_W, GROUP_W, W_LORA, A_LORA, G_LORA,
            GROUP_W, KV_W, KV_W,
            GROUP_W, GROUP_W, GROUP_W, GROUP_W, N_GATE_COLS)
D_IN = sum(IN_SIZES)
IN_OFFSETS = tuple(int(o) for o in np.cumsum(IN_SIZES)[:-1])

LANE = 128
VMEM_LIMIT_BYTES = 56 * 1024 * 1024


def _mm_kernel(x_ref, w_ref, o_ref):
    o_ref[...] = jnp.dot(x_ref[...].astype(BF16), w_ref[...], preferred_element_type=F32)


def _matmul(x, w, tm=512):
    M, K = x.shape
    N = w.shape[1]
    Np = -(-N // LANE) * LANE
    wb = w.astype(BF16)
    if Np != N:
        wb = jnp.pad(wb, ((0, 0), (0, Np - N)))
    tm = min(tm, M)
    assert M % tm == 0
    out = pl.pallas_call(
        _mm_kernel,
        grid=(M // tm,),
        in_specs=[pl.BlockSpec((tm, K), lambda i: (i, 0)),
                  pl.BlockSpec((K, Np), lambda i: (0, 0))],
        out_specs=pl.BlockSpec((tm, Np), lambda i: (i, 0)),
        out_shape=jax.ShapeDtypeStruct((M, Np), F32),
        compiler_params=pltpu.CompilerParams(
            dimension_semantics=("arbitrary",), vmem_limit_bytes=VMEM_LIMIT_BYTES),
        name="matmul",
    )(x, wb)
    return out


def _mm3(x, w):
    lead = x.shape[:-1]
    out = _matmul(x.reshape(-1, x.shape[-1]), w)
    return out.reshape(lead + (out.shape[1],))


_NEG_INF = float("-inf")


def _gelu_tanh(x):
    c = math.sqrt(2.0 / math.pi)
    return 0.5 * x * (1.0 + jnp.tanh(c * (x + 0.044715 * (x * x * x))))


_SUBLANES = 8


def _sort_network(n):
    pairs, p = [], 1
    while p < 16:
        k = p
        while k >= 1:
            for j in range(k % p, 16 - k, 2 * k):
                for i in range(min(k, 16 - j - k)):
                    if (i + j) // (2 * p) == (i + j + k) // (2 * p):
                        pairs.append((i + j, i + j + k))
            k //= 2
        p *= 2
    return [(a, b) for a, b in pairs if b < n]


def _top16_values(x):
    v = [x[_SUBLANES * k:_SUBLANES * (k + 1), :] for k in range(x.shape[0] // _SUBLANES)]
    for a, b in _sort_network(len(v)):
        v[a], v[b] = jnp.maximum(v[a], v[b]), jnp.minimum(v[a], v[b])
    sub = lax.broadcasted_iota(jnp.int32, v[0].shape, 0)
    vals = []
    for step in range(PEER_TOPK):
        mx = jnp.max(v[0], axis=0, keepdims=True)
        vals.append(mx)
        remaining = PEER_TOPK - 1 - step
        if remaining == 0:
            break
        hit = sub == jnp.min(jnp.where(v[0] == mx, sub, _SUBLANES), axis=0, keepdims=True)
        for k in range(min(remaining, len(v))):
            v[k] = jnp.where(hit, v[k + 1] if k + 1 < len(v) else _NEG_INF, v[k])
    return vals


def _peer_candidates(sv1, sv2):
    row8 = lax.broadcasted_iota(jnp.int32, (8, sv1.shape[1]), 0)
    blocks = [sv1[0:1, :] + sv2[0:8, :], sv1[0:1, :] + sv2[8:16, :]]
    for a in range(1, 8):
        n_valid = PEER_TOPK // (a + 1)
        blk = sv1[a:a + 1, :] + sv2[0:8, :]
        blocks.append(blk if n_valid >= 8 else jnp.where(row8 < n_valid, blk, _NEG_INF))
    blocks.append(sv1[8:16, :] + sv2[0:1, :])
    return jnp.concatenate(blocks, axis=0)


def _peer_kernel(h_ref, wq_ref, keys_ref, u_ref, vt_ref, o_ref,
                 hbt_ref, s_ref, sv_ref, e1_ref, cnt_ref, e2_ref, rank_ref, act_ref, wg_ref, *, tm, te):
    j = pl.program_id(1)
    n_lg = tm // LANE
    n_pair = n_lg // 2
    n_ib = te // N_KEYS

    @pl.when(j == 0)
    def _prepare():
        hb = h_ref[...].astype(BF16)
        hbt_ref[...] = h_ref[...].T.astype(BF16)
        qb = jnp.dot(hb, wq_ref[...], preferred_element_type=F32).astype(BF16)
        for hp in range(2 * PEER_HEADS):
            s_ref[hp] = lax.dot_general(keys_ref[hp], qb[:, hp * PEER_HALF:(hp + 1) * PEER_HALF],
                                        (((1,), (1,)), ((), ())), preferred_element_type=F32)

        def lane_groups(it):
            return it // n_pair, [pl.ds(pl.multiple_of(((it % n_pair) * 2 + half) * LANE, LANE), LANE)
                                  for half in range(2)]

        def top_body(it, carry):
            h, groups = lane_groups(it)
            for ls in groups:
                sv_ref[2 * h, :, ls] = jnp.concatenate(_top16_values(s_ref[2 * h, :, ls]), axis=0)
                s2 = s_ref[2 * h + 1, :, ls]
                vals = _top16_values(s2)
                sv_ref[2 * h + 1, :, ls] = jnp.concatenate(vals, axis=0)
                rank = jnp.zeros_like(s2)
                for val in vals:
                    rank = rank + jnp.where(val > s2, 1.0, 0.0)
                rank_ref[h, :, ls] = rank.astype(BF16)
            return carry

        lax.fori_loop(0, PEER_HEADS * n_pair, top_body, 0)

        def head_body(it, carry):
            h, groups = lane_groups(it)
            for ls in groups:
                sv1 = sv_ref[2 * h, :, ls]
                sv2 = sv_ref[2 * h + 1, :, ls]
                tv = _top16_values(_peer_candidates(sv1, sv2))
                thr = tv[PEER_TOPK - 1]
                z = jnp.zeros_like(thr)
                for t in tv:
                    z = z + jnp.exp(t - tv[0])
                s1 = s_ref[2 * h, :, ls]
                cnt = jnp.zeros_like(s1)
                for b in range(PEER_TOPK):
                    cnt = cnt + jnp.where(s1 + sv2[b:b + 1, :] >= thr, 1.0, 0.0)
                cnt_ref[h, :, ls] = cnt
                e1_ref[h, :, ls] = jnp.exp(s1 - sv1[0:1, :])
                e2_ref[h, :, ls] = (jnp.exp(s_ref[2 * h + 1, :, ls] - sv2[0:1, :]) / z).astype(BF16)
            return carry

        lax.fori_loop(0, PEER_HEADS * n_pair, head_body, 0)
        o_ref[...] = jnp.zeros_like(o_ref)

    i0 = pl.multiple_of(j * n_ib, 8)
    half_rows = te // 2
    act_ref[...] = jnp.dot(u_ref[...], hbt_ref[...], preferred_element_type=F32)
    for part in range(2):
        for lg in range(n_lg):
            ls = slice(lg * LANE, (lg + 1) * LANE)
            cnt8 = [cnt_ref[h, pl.ds(i0, 8), ls].astype(BF16) for h in range(PEER_HEADS)]
            e18 = [e1_ref[h, pl.ds(i0, 8), ls].astype(BF16) for h in range(PEER_HEADS)]
            for ib in range(part * n_ib // 2, (part + 1) * n_ib // 2):
                w = jnp.zeros((N_KEYS, LANE), BF16)
                for h in range(PEER_HEADS):
                    sel = rank_ref[h, :, ls] < cnt8[h][ib:ib + 1, :]
                    w = w + jnp.where(sel, e2_ref[h, :, ls] * e18[h][ib:ib + 1, :], jnp.zeros((), BF16))
                rs = slice(ib * N_KEYS, (ib + 1) * N_KEYS)
                wg_ref[rs, ls] = w * _gelu_tanh(act_ref[rs, ls].astype(BF16))
        rows = slice(part * half_rows, (part + 1) * half_rows)
        o_ref[...] += jnp.dot(vt_ref[0, :, rows], wg_ref[rows, :], preferred_element_type=F32)


_PEER_TE = 8 * N_KEYS


def _peer_v_tiles(v):
    E, D = v.shape
    return jnp.transpose(v.astype(BF16).reshape(E // _PEER_TE, _PEER_TE, D), (0, 2, 1))


def _peer_dense(hf, wq_b, keys_b, u_b, vt_b, tm=512):
    M, D = hf.shape
    E = u_b.shape[0]
    te = _PEER_TE
    tm = min(tm, M)
    assert M % tm == 0 and E % te == 0 and tm % (2 * LANE) == 0 and vt_b.shape == (E // te, D, te)
    kern = functools.partial(_peer_kernel, tm=tm, te=te)
    return pl.pallas_call(
        kern,
        grid=(M // tm, E // te),
        in_specs=[pl.BlockSpec((tm, D), lambda i, j: (i, 0)),
                  pl.BlockSpec(wq_b.shape, lambda i, j: (0, 0)),
                  pl.BlockSpec(keys_b.shape, lambda i, j: (0, 0, 0)),
                  pl.BlockSpec((te, D), lambda i, j: (j, 0)),
                  pl.BlockSpec((1, D, te), lambda i, j: (j, 0, 0))],
        out_specs=pl.BlockSpec((D, tm), lambda i, j: (0, i)),
        out_shape=jax.ShapeDtypeStruct((D, M), F32),
        scratch_shapes=[pltpu.VMEM((D, tm), BF16),
                        pltpu.VMEM((2 * PEER_HEADS, N_KEYS, tm), F32),
                        pltpu.VMEM((2 * PEER_HEADS, PEER_TOPK, tm), F32),
                        pltpu.VMEM((PEER_HEADS, N_KEYS, tm), F32),
                        pltpu.VMEM((PEER_HEADS, N_KEYS, tm), F32),
                        pltpu.VMEM((PEER_HEADS, N_KEYS, tm), BF16),
                        pltpu.VMEM((PEER_HEADS, N_KEYS, tm), BF16),
                        pltpu.VMEM((te, tm), F32),
                        pltpu.VMEM((te, tm), BF16)],
        compiler_params=pltpu.CompilerParams(
            dimension_semantics=("arbitrary", "arbitrary"), vmem_limit_bytes=VMEM_LIMIT_BYTES),
        name="peer_dense",
    )(hf, wq_b, keys_b, u_b, vt_b)


_RWKV_UNROLL = 8


def _rwkv_kernel(sh_f_ref, sh_b_ref, d0_ref, d1_ref, y0_ref, y1_ref, s_ref, sa_ref, *, tb, n_batch):
    i = pl.program_id(0)
    n_ch = 2 * n_batch
    n_tiles = tb // _RWKV_UNROLL

    @pl.when(i == 0)
    def _init():
        s_ref[...] = jnp.zeros_like(s_ref)

    lane = lax.broadcasted_iota(jnp.int32, (GROUP_W, GROUP_W), 1)
    sub = lax.broadcasted_iota(jnp.int32, (GROUP_W, GROUP_W), 0)
    seg_ones = jnp.where(lane // HEAD_DIM == sub // HEAD_DIM, 1.0, 0.0).astype(BF16)
    lane_v = lax.broadcasted_iota(jnp.int32, (HEAD_DIM, GROUP_W), 1)
    sub_v = lax.broadcasted_iota(jnp.int32, (HEAD_DIM, GROUP_W), 0)
    eye = (lane_v % HEAD_DIM == sub_v)

    def seg_sum(p):
        return jnp.dot(p.astype(BF16), seg_ones, preferred_element_type=F32)

    def chain_refs(c):
        d, b = divmod(c, n_batch)
        return d, b, (sh_f_ref, d0_ref) if d == 0 else (sh_b_ref, d1_ref)

    def tile_start(d, tt):
        return pl.multiple_of((tt if d == 0 else n_tiles - 1 - tt) * _RWKV_UNROLL, _RWKV_UNROLL)

    def first_kk(c, tt):
        d, b, (sh_ref, _) = chain_refs(c)
        kk8 = sh_ref[0, b, pl.ds(tile_start(d, tt), _RWKV_UNROLL), :]
        r0 = 0 if d == 0 else _RWKV_UNROLL - 1
        return kk8[r0:r0 + 1, :]

    sa_ref[...] = seg_sum(jnp.concatenate([s_ref[c] * first_kk(c, 0) for c in range(n_ch)], axis=0))

    def tile_body(tt, carry):
        rows, t8s, dirs = [], [], []
        for c in range(n_ch):
            d, b, (sh_ref, dr_ref) = chain_refs(c)
            t8 = tile_start(d, tt)
            rows.append([sh_ref[q, b, pl.ds(t8, _RWKV_UNROLL), :] for q in range(3)]
                        + [dr_ref[q, b, pl.ds(t8, _RWKV_UNROLL), :] for q in range(3)])
            t8s.append(t8)
            dirs.append(d)
        kk_next_tile = [first_kk(c, jnp.minimum(tt + 1, n_tiles - 1)) for c in range(n_ch)]

        def step_rows(step):
            return [step if d == 0 else _RWKV_UNROLL - 1 - step for d in dirs]

        vexp = seg_sum(jnp.concatenate([jnp.where(eye, rows[c][2][s:s + 1, :], 0.0)
                                        for c in range(n_ch) for s in range(_RWKV_UNROLL)], axis=0))
        S = [s_ref[c] for c in range(n_ch)]
        sa = [sa_ref[c * HEAD_DIM:(c + 1) * HEAD_DIM, :] for c in range(n_ch)]
        ys = [[None] * _RWKV_UNROLL for _ in range(n_ch)]
        for step in range(_RWKV_UNROLL):
            row = step_rows(step)

            def r_(c, q, rw=None):
                rw = row[c] if rw is None else rw
                return rows[c][q][rw:rw + 1, :]

            if step + 1 < _RWKV_UNROLL:
                kk_next = [r_(c, 0, row[c] + (1 if dirs[c] == 0 else -1)) for c in range(n_ch)]
            else:
                kk_next = kk_next_tile
            prods = []
            for c in range(n_ch):
                v0 = (c * _RWKV_UNROLL + row[c]) * HEAD_DIM
                ahead = S[c] * r_(c, 3) + vexp[v0:v0 + HEAD_DIM, :] * r_(c, 5)
                prods.append(ahead * kk_next[c] - sa[c] * (r_(c, 4) * kk_next[c]))
                S[c] = ahead - sa[c] * r_(c, 4)
            res = seg_sum(jnp.concatenate(prods, axis=0))
            sa = [res[c * HEAD_DIM:(c + 1) * HEAD_DIM, :] for c in range(n_ch)]
            for c in range(n_ch):
                ys[c][row[c]] = S[c] * r_(c, 1)
        ye = seg_sum(jnp.concatenate([ys[c][s] for c in range(n_ch) for s in range(_RWKV_UNROLL)], axis=0))
        for c in range(n_ch):
            for s in range(_RWKV_UNROLL):
                v0 = (c * _RWKV_UNROLL + s) * HEAD_DIM
                ys[c][s] = jnp.sum(jnp.where(eye, ye[v0:v0 + HEAD_DIM, :], 0.0), axis=0, keepdims=True)
        sa_ref[...] = jnp.concatenate(sa, axis=0)
        for c in range(n_ch):
            s_ref[c] = S[c]
            y_ref = y0_ref if dirs[c] == 0 else y1_ref
            y_ref[c % n_batch, pl.ds(t8s[c], _RWKV_UNROLL), :] = jnp.concatenate(ys[c], axis=0)
        return carry

    lax.fori_loop(0, n_tiles, tile_body, 0)


def _rwkv_scan_pallas(shared, dir0, dir1, n_ctx, tb=256):
    _, B, T, C = shared.shape
    assert T % tb == 0 and n_ctx % tb == 0 and tb % _RWKV_UNROLL == 0 and C == GROUP_W
    nblk, ncb = T // tb, n_ctx // tb

    def fwd3(i):
        return (0, 0, i, 0)

    def bwd_blk(i):
        return jnp.where(i < ncb, ncb - 1 - i, nblk - 1 - (i - ncb))

    def bwd3(i):
        return (0, 0, bwd_blk(i), 0)

    kern = functools.partial(_rwkv_kernel, tb=tb, n_batch=B)
    blk = (3, B, tb, C)
    return pl.pallas_call(
        kern,
        grid=(nblk,),
        in_specs=[pl.BlockSpec(blk, fwd3), pl.BlockSpec(blk, bwd3), pl.BlockSpec(blk, fwd3), pl.BlockSpec(blk, bwd3)],
        out_specs=[pl.BlockSpec((B, tb, C), lambda i: (0, i, 0)),
                   pl.BlockSpec((B, tb, C), lambda i: (0, bwd_blk(i), 0))],
        out_shape=[jax.ShapeDtypeStruct((B, T, C), F32)] * 2,
        scratch_shapes=[pltpu.VMEM((2 * B, HEAD_DIM, C), F32), pltpu.VMEM((2 * B * HEAD_DIM, C), F32)],
        compiler_params=pltpu.CompilerParams(
            dimension_semantics=("arbitrary",), vmem_limit_bytes=VMEM_LIMIT_BYTES),
        name="rwkv_scan",
    )(shared, shared, dir0, dir1)


def _mlstm_kernel(q_ref, k_ref, v_ref, row_ref, col_ref, h_ref, c_ref, n_ref, m_ref):
    g = pl.program_id(0)
    L = q_ref.shape[2]

    @pl.when(pl.program_id(1) == 0)
    def _init():
        c_ref[...] = jnp.zeros_like(c_ref)
        n_ref[...] = jnp.zeros_like(n_ref)
        m_ref[...] = jnp.zeros_like(m_ref)

    backward = (g % 2) == 1
    sgn = 1 - 2 * (g % 2)
    tt = lax.broadcasted_iota(jnp.int32, (L, L), 0)
    ss = lax.broadcasted_iota(jnp.int32, (L, L), 1)
    causal = (tt - ss) * sgn >= 0
    nt_dims = (((1,), (1,)), ((), ()))
    for h in range(N_HEADS):
        q = q_ref[0, h]
        k = k_ref[0, h]
        v = v_ref[0, h]
        qb, kb, vb = q.astype(BF16), k.astype(BF16), v.astype(BF16)
        brow = row_ref[0, h, 0, 0:1, :]
        lirow = row_ref[0, h, 0, 1:2, :]
        bcol = col_ref[0, h, 0, :, 0:1]
        licol = col_ref[0, h, 0, :, 1:2]
        m_prev = m_ref[h][:, 0:1]
        C = c_ref[h]
        n = n_ref[h]
        d_intra = jnp.where(causal, bcol - brow + lirow, _NEG_INF)
        d_inter = bcol + m_prev
        m_t = jnp.maximum(jnp.max(d_intra, axis=1, keepdims=True), d_inter)
        w_intra = jnp.exp(d_intra - m_t)
        w_inter = jnp.exp(d_inter - m_t)
        s = lax.dot_general(qb, kb, nt_dims, preferred_element_type=F32) * w_intra
        num = (jnp.dot(s.astype(BF16), vb, preferred_element_type=F32)
               + w_inter * lax.dot_general(qb, C.astype(BF16), nt_dims, preferred_element_type=F32))
        den = jnp.sum(s, axis=1, keepdims=True) + w_inter * jnp.sum(q * n, axis=1, keepdims=True)
        h_ref[0, h] = num / jnp.maximum(jnp.abs(den), jnp.exp(-m_t))
        b_end = jnp.where(backward, brow[:, 0:1], brow[:, L - 1:L])
        d_end = b_end - bcol + licol
        m_new = jnp.maximum(b_end + m_prev, jnp.max(d_end, axis=0, keepdims=True))
        w_end = jnp.exp(d_end - m_new)
        decay = jnp.exp(b_end + m_prev - m_new)
        wv_t = (w_end * v).T.astype(BF16)
        c_ref[h] = decay * C + jnp.dot(wv_t, kb, preferred_element_type=F32)
        n_ref[h] = decay * n + jnp.sum(w_end * k, axis=0, keepdims=True)
        m_ref[h] = jnp.broadcast_to(m_new, (1, LANE))


def _mlstm_scan_pallas(qh, kh, vh, rows, cols, n_ctx):
    B, H, T, Dh = qh.shape
    L = MLSTM_CHUNK
    nc, ncb = T // L, n_ctx // L
    assert T % L == 0 and n_ctx % L == 0 and H == N_HEADS

    def chunk(g, c):
        rev = jnp.where(c < ncb, ncb - 1 - c, nc - 1 - (c - ncb))
        return jnp.where(g % 2 == 0, c, rev)

    qkv_spec = pl.BlockSpec((1, H, L, Dh), lambda g, c: (g // 2, 0, chunk(g, c), 0))
    return pl.pallas_call(
        _mlstm_kernel,
        grid=(2 * B, nc),
        in_specs=[qkv_spec, qkv_spec, qkv_spec,
                  pl.BlockSpec((1, H, 1, 2, L), lambda g, c: (g, 0, chunk(g, c), 0, 0)),
                  pl.BlockSpec((1, H, 1, L, 2), lambda g, c: (g, 0, chunk(g, c), 0, 0))],
        out_specs=pl.BlockSpec((1, H, L, Dh), lambda g, c: (g, 0, chunk(g, c), 0)),
        out_shape=jax.ShapeDtypeStruct((2 * B, H, T, Dh), F32),
        scratch_shapes=[pltpu.VMEM((H, Dh, Dh), F32), pltpu.VMEM((H, 1, Dh), F32), pltpu.VMEM((H, 1, LANE), F32)],
        compiler_params=pltpu.CompilerParams(
            dimension_semantics=("arbitrary", "arbitrary"), vmem_limit_bytes=VMEM_LIMIT_BYTES),
        name="mlstm_scan",
    )(qh, kh, vh, rows, cols)


def _attn_kernel(*refs, n_band, t_total):
    q_ref = refs[0]
    band = refs[1:1 + 2 * n_band]
    kc_ref, vc_ref, sink_ref, o_ref = refs[1 + 2 * n_band:]
    n = pl.program_id(1)
    nt_dims = (((1,), (1,)), ((), ()))
    if n_band:
        qpos = n * ATT_BLOCK + lax.broadcasted_iota(jnp.int32, (ATT_BLOCK, n_band * ATT_BLOCK), 0)
        kpos = (n - 1) * ATT_BLOCK + lax.broadcasted_iota(jnp.int32, (ATT_BLOCK, n_band * ATT_BLOCK), 1)
        mask = (jnp.abs(qpos - kpos) <= WINDOW) & (kpos >= 0) & (kpos < t_total)
    for kvh in range(KV_HEADS):
        kc = kc_ref[0, kvh].astype(BF16)
        vc = vc_ref[0, kvh].astype(BF16)
        if n_band:
            kw = jnp.concatenate([band[j][0, kvh] for j in range(n_band)], axis=0).astype(BF16)
            vw = jnp.concatenate([band[n_band + j][0, kvh] for j in range(n_band)], axis=0).astype(BF16)
        for g in range(Q_PER_KV):
            h = kvh * Q_PER_KV + g
            q = q_ref[0, h].astype(BF16)
            sink = sink_ref[h][:, 0:1]
            s_ctx = lax.dot_general(q, kc, nt_dims, preferred_element_type=F32) * ATT_SCALE
            m = jnp.maximum(jnp.max(s_ctx, axis=1, keepdims=True), sink)
            if n_band:
                s_loc = lax.dot_general(q, kw, nt_dims, preferred_element_type=F32) * ATT_SCALE
                s_loc = jnp.where(mask, s_loc, _NEG_INF)
                m = jnp.maximum(m, jnp.max(s_loc, axis=1, keepdims=True))
            p_ctx = jnp.exp(s_ctx - m)
            den = jnp.sum(p_ctx, axis=1, keepdims=True) + jnp.exp(sink - m)
            o = jnp.dot(p_ctx.astype(BF16), vc, preferred_element_type=F32)
            if n_band:
                p_loc = jnp.exp(s_loc - m)
                den = den + jnp.sum(p_loc, axis=1, keepdims=True)
                o = o + jnp.dot(p_loc.astype(BF16), vw, preferred_element_type=F32)
            o_ref[0, h] = o / den


def _attention_pallas(q, k, v, kc, vc, sink):
    B, H, T, Dh = q.shape
    C = kc.shape[2]
    nb = T // ATT_BLOCK
    assert T % ATT_BLOCK == 0
    n_band = 0 if k is None else 3
    sink_b = jnp.broadcast_to(sink.astype(F32)[:, None, None], (H, 1, LANE))
    band_specs = [pl.BlockSpec((1, KV_HEADS, ATT_BLOCK, Dh),
                               lambda b, n, j=j: (b, 0, jnp.clip(n + j - 1, 0, nb - 1), 0)) for j in range(n_band)]
    ctx_spec = pl.BlockSpec((1, KV_HEADS, C, Dh), lambda b, n: (b, 0, 0, 0))
    kern = functools.partial(_attn_kernel, n_band=n_band, t_total=T)
    band_args = [] if k is None else [k] * 3 + [v] * 3
    return pl.pallas_call(
        kern,
        grid=(B, nb),
        in_specs=[pl.BlockSpec((1, H, ATT_BLOCK, Dh), lambda b, n: (b, 0, n, 0))] + band_specs * 2
                 + [ctx_spec, ctx_spec, pl.BlockSpec((H, 1, LANE), lambda b, n: (0, 0, 0))],
        out_specs=pl.BlockSpec((1, H, ATT_BLOCK, Dh), lambda b, n: (b, 0, n, 0)),
        out_shape=jax.ShapeDtypeStruct((B, H, T, Dh), F32),
        compiler_params=pltpu.CompilerParams(
            dimension_semantics=("arbitrary", "arbitrary"), vmem_limit_bytes=VMEM_LIMIT_BYTES),
        name="attention",
    )(q, *band_args, kc, vc, sink_b)


def rms_norm(x, g):
    xf = x.astype(F32)
    y = xf * lax.rsqrt(jnp.mean(xf * xf, axis=-1, keepdims=True) + EPS)
    return (y * g.astype(F32)).astype(x.dtype)


def heads(t):
    return t.reshape(t.shape[:-1] + (N_HEADS, HEAD_DIM))


def head_norm_merge(y, g):
    return rms_norm(y, g).reshape(y.shape[:-2] + (GROUP_W,))


def rope_2d(x, row, col):
    quarter = HEAD_DIM // 4
    inv = ROPE_BASE ** (-jnp.arange(quarter, dtype=F32) / quarter)
    xf = x.astype(F32)
    extra = (1,) * (x.ndim - 3)

    def rot(xa, pos):
        ang = pos.astype(F32)[:, None] * inv[None, :]
        ang = ang.reshape((1, ang.shape[0]) + extra + (quarter,))
        cos, sin = jnp.cos(ang), jnp.sin(ang)
        x1, x2 = xa[..., :quarter], xa[..., quarter:]
        return jnp.concatenate([x1 * cos - x2 * sin, x2 * cos + x1 * sin], axis=-1)

    half = HEAD_DIM // 2
    return jnp.concatenate([rot(xf[..., :half], row), rot(xf[..., half:], col)], axis=-1).astype(x.dtype)


def conv_mixer(hx, b_gate, c_gate, w, g):
    u = c_gate * hx
    up = jnp.pad(u, ((0, 0), (1, 1), (0, 0)))
    y = b_gate * (w[0] * up[:, :-2] + w[1] * up[:, 1:-1] + w[2] * up[:, 2:])
    return head_norm_merge(heads(y), g)


_RWKV_COL_BLOCK = IN_OFFSETS[2] // GROUP_W
assert IN_OFFSETS[2] % GROUP_W == 0 and W_LORA + A_LORA + G_LORA <= GROUP_W


def _rwkv_prep_kernel(r_ref, k_ref, v_ref, x_ref, vec_ref, w2_ref, a2_ref, g2_ref,
                      sh_ref, d0_ref, d1_ref, g_ref):
    r, k, v, x = r_ref[0], k_ref[0], v_ref[0], x_ref[0]
    lane = lax.broadcasted_iota(jnp.int32, (GROUP_W, GROUP_W), 1)
    sub = lax.broadcasted_iota(jnp.int32, (GROUP_W, GROUP_W), 0)
    seg_ones = jnp.where(lane // HEAD_DIM == sub // HEAD_DIM, 1.0, 0.0).astype(BF16)
    kkr = k * vec_ref[0:1, :]
    sq = kkr * kkr
    sq_hi = sq.astype(BF16)
    sq_lo = (sq - sq_hi.astype(F32)).astype(BF16)
    ss = (jnp.dot(sq_hi, seg_ones, preferred_element_type=F32)
          + jnp.dot(sq_lo, seg_ones, preferred_element_type=F32))
    kk = kkr * lax.rsqrt(ss + EPS)
    sh_ref[0, 0], sh_ref[1, 0], sh_ref[2, 0] = kk, r, v
    g_ref[0] = jnp.dot(jax.nn.sigmoid(x).astype(BF16), g2_ref[...], preferred_element_type=F32)
    xt = jnp.tanh(x).astype(BF16)
    xb = x.astype(BF16)
    for d, d_ref in enumerate((d0_ref, d1_ref)):
        lw = jnp.dot(xt, w2_ref[d], preferred_element_type=F32)
        la = jnp.dot(xb, a2_ref[d], preferred_element_type=F32)
        a = jax.nn.sigmoid(vec_ref[4 + d:5 + d, :] + la)
        d_ref[0, 0] = jnp.exp(-RWKV_DECAY_SCALE * jax.nn.sigmoid(vec_ref[2 + d:3 + d, :] + lw))
        d_ref[1, 0] = kk * a
        d_ref[2, 0] = k * (1 + (a - 1) * vec_ref[1:2, :])


def _rwkv_prep_pallas(p_all, w0, w2, a0, a2, g2, k_k, k_a, tb=256):
    B, T, _ = p_all.shape
    assert T % tb == 0
    C = GROUP_W
    vec = jnp.concatenate([k_k[None], k_a[None], w0, a0, jnp.zeros((2, C), F32)], axis=0)

    def pad_rows(w, row0):
        lead = w.shape[:-2]
        return jnp.zeros(lead + (C, C), BF16).at[..., row0:row0 + w.shape[-2], :].set(w.astype(BF16))

    w2p, a2p, g2p = pad_rows(w2, 0), pad_rows(a2, W_LORA), pad_rows(g2, W_LORA + A_LORA)
    col = lambda j: pl.BlockSpec((1, tb, C), lambda b, t, j=j: (b, t, _RWKV_COL_BLOCK + j))
    full = lambda a: pl.BlockSpec(a.shape, lambda b, t: (0,) * a.ndim)
    out3 = pl.BlockSpec((3, 1, tb, C), lambda b, t: (0, b, t, 0))
    return pl.pallas_call(
        _rwkv_prep_kernel,
        grid=(B, T // tb),
        in_specs=[col(0), col(1), col(2), col(3), full(vec), full(w2p), full(a2p), full(g2p)],
        out_specs=[out3, out3, out3, pl.BlockSpec((1, tb, C), lambda b, t: (b, t, 0))],
        out_shape=[jax.ShapeDtypeStruct((3, B, T, C), F32)] * 3 + [jax.ShapeDtypeStruct((B, T, C), F32)],
        compiler_params=pltpu.CompilerParams(
            dimension_semantics=("arbitrary", "arbitrary"), vmem_limit_bytes=VMEM_LIMIT_BYTES),
        name="rwkv_prep",
    )(p_all, p_all, p_all, p_all, vec, w2p, a2p, g2p)


def rwkv_mixer(p_all, n_ctx, w0, w2, a0, a2, g2, k_k, k_a, r_k, ln_g, need_ctx):
    shared, dir0, dir1, g = _rwkv_prep_pallas(p_all, w0, w2, a0, a2, g2, k_k, k_a)
    y0, y1 = _rwkv_scan_pallas(shared, dir0, dir1, n_ctx)
    y = rms_norm(heads(y0 + y1), ln_g)
    rh = heads(shared[1])
    bonus = (jnp.sum(rh * heads(dir0[2]) * r_k, axis=-1, keepdims=True)
             + jnp.sum(rh * heads(dir1[2]) * r_k, axis=-1, keepdims=True)) * heads(shared[2])
    out = (y + bonus).reshape(g.shape) * g
    return out[:, n_ctx:], (out[:, :n_ctx] if need_ctx else None)


def attn_project(q, k, v, q_g, k_g):
    B, T, _ = q.shape
    q = rms_norm(q.reshape(B, T, KV_HEADS, Q_PER_KV, HEAD_DIM), q_g)
    k = rms_norm(k.reshape(B, T, KV_HEADS, HEAD_DIM), k_g)
    v = v.reshape(B, T, KV_HEADS, HEAD_DIM)
    return q, k, v


def _head_major(t):
    B, T = t.shape[:2]
    return jnp.moveaxis(t.reshape(B, T, -1, HEAD_DIM), 2, 1)


def latent_attention(q, k, v, kc, vc, sink):
    o = _attention_pallas(_head_major(q), _head_major(k), _head_major(v), _head_major(kc), _head_major(vc), sink)
    return jnp.moveaxis(o, 1, 2)


def ctx_attention(qc, kc, vc, sink):
    o = _attention_pallas(_head_major(qc), None, None, _head_major(kc), _head_major(vc), sink)
    return jnp.moveaxis(o, 1, 2)


def mlstm_mixer(seq, n_ctx, i_b, f_b, out_g, need_ctx):
    q, k, v, o, gates = seq
    B, T, _ = q.shape
    L = MLSTM_CHUNK

    def th(t):
        return jnp.moveaxis(heads(t.astype(F32)), 2, 1)

    gates = gates.astype(F32).reshape(B, T, 2, 2, N_HEADS) + jnp.stack([i_b, f_b], axis=1).astype(F32)
    gates = jnp.moveaxis(gates, 1, -1)
    logi = gates[:, :, 0].reshape(B, 2, N_HEADS, T // L, L)
    logf = jax.nn.log_sigmoid(gates[:, :, 1]).reshape(B, 2, N_HEADS, T // L, L)
    bcum = jnp.stack([jnp.cumsum(logf[:, 0], axis=-1),
                      jnp.flip(jnp.cumsum(jnp.flip(logf[:, 1], axis=-1), axis=-1), axis=-1)], axis=1)
    rows = jnp.stack([bcum, logi], axis=-2).reshape(B * 2, N_HEADS, T // L, 2, L)
    cols = jnp.stack([bcum, logi], axis=-1).reshape(B * 2, N_HEADS, T // L, L, 2)
    h = _mlstm_scan_pallas(th(q), th(k) * (HEAD_DIM ** -0.5), th(v), rows, cols, n_ctx)
    h = h.reshape(B, 2, N_HEADS, T, HEAD_DIM)
    y = jax.nn.sigmoid(o) * head_norm_merge(jnp.moveaxis(h[:, 0] + h[:, 1], 1, 2), out_g)
    return y[:, n_ctx:], (y[:, :n_ctx] if need_ctx else None)


def kernel(x, c, ctx, c_ctx, ada_w, ada_b, norm1_g, norm2_g, w_in, w_out, conv_w, conv_g,
           rwkv_w0, rwkv_w2, rwkv_a0, rwkv_a2, rwkv_g2, rwkv_kk, rwkv_ka, rwkv_rk, rwkv_ln_g,
           att_q_g, att_k_g, att_sink, att_out_g, ml_i_b, ml_f_b, ml_out_g,
           peer_wq, peer_keys, peer_u, peer_v):
    B, T, D = x.shape
    n_ctx = ctx.shape[1]
    ROWS = T // GRID_W
    row = jnp.repeat(jnp.arange(ROWS), GRID_W)
    col = jnp.arange(ROWS * GRID_W) % GRID_W
    for l in range(DEPTH):
        need_ctx = l < DEPTH - 1
        mod = jax.nn.silu(c) @ ada_w[l] + ada_b[l]
        mod_c = jax.nn.silu(c_ctx) @ ada_w[l] + ada_b[l]
        sh1, sc1, gt1, sh2, sc2, gt2 = jnp.split(mod[:, None, :], 6, axis=-1)
        csh1, csc1, cgt1, csh2, csc2, cgt2 = jnp.split(mod_c, 6, axis=-1)

        h = rms_norm(x, norm1_g[l]) * (1 + sc1) + sh1
        hc = rms_norm(ctx, norm1_g[l]) * (1 + csc1) + csh1
        p_all = _mm3(jnp.concatenate([hc, h], axis=1), w_in[l])
        offs = (0,) + IN_OFFSETS
        S_ = [p_all[..., o_:o_ + n_] for o_, n_ in zip(offs, IN_SIZES)]
        P = [t[:, n_ctx:] for t in S_]
        Pc = [t[:, :n_ctx] for t in S_]

        y_a = conv_mixer(P[0], P[1], P[2], conv_w[l], conv_g[l])
        y_b, yc_b = rwkv_mixer(p_all, n_ctx, rwkv_w0[l], rwkv_w2[l], rwkv_a0[l], rwkv_a2[l],
                               rwkv_g2[l], rwkv_kk[l], rwkv_ka[l], rwkv_rk[l], rwkv_ln_g[l], need_ctx)
        q, k, v = attn_project(P[9], P[10], P[11], att_q_g[l], att_k_g[l])
        q, k = rope_2d(q, row, col), rope_2d(k, row, col)
        qc, kc, vc = attn_project(Pc[9], Pc[10], Pc[11], att_q_g[l], att_k_g[l])
        y_c = head_norm_merge(latent_attention(q, k, v, kc, vc, att_sink[l]), att_out_g[l])
        y_d, yc_d = mlstm_mixer(S_[12:17], n_ctx, ml_i_b[l], ml_f_b[l], ml_out_g[l], need_ctx)

        y = _mm3(jnp.concatenate([t.astype(x.dtype) for t in (y_a, y_b, y_c, y_d)], axis=-1), w_out[l])
        x = x + gt1 * y
        h2 = rms_norm(x, norm2_g[l]) * (1 + sc2) + sh2
        tok = [h2.reshape(B * T, D)]
        if need_ctx:
            yc_a = conv_mixer(Pc[0], Pc[1], Pc[2], conv_w[l], conv_g[l])
            yc_c = head_norm_merge(ctx_attention(qc, kc, vc, att_sink[l]), att_out_g[l])
            yc = _mm3(jnp.concatenate([t.astype(ctx.dtype) for t in (yc_a, yc_b, yc_c, yc_d)], axis=-1), w_out[l])
            ctx = ctx + cgt1 * yc
            hc2 = rms_norm(ctx, norm2_g[l]) * (1 + csc2) + csh2
            tok.append(hc2.reshape(-1, D))
        peer_t = _peer_dense(jnp.concatenate(tok, axis=0) if need_ctx else tok[0],
                             peer_wq[l].astype(BF16),
                             peer_keys[l].reshape(2 * PEER_HEADS, N_KEYS, PEER_HALF).astype(BF16),
                             peer_u[l].astype(BF16), _peer_v_tiles(peer_v[l]))
        x = x + gt2 * peer_t[:, :B * T].T.reshape(B, T, D)
        if need_ctx:
            ctx = ctx + cgt2 * peer_t[:, B * T:].T.reshape(ctx.shape)
    return x
```

```python
import functools
import math

import jax
import jax.numpy as jnp
import numpy as np
from jax import lax
from jax.experimental import pallas as pl
from jax.experimental.pallas import tpu as pltpu

D_MODEL = 1024
DEPTH = 2
GRID_W = 64
N_MIXERS = 4
GROUP_W = D_MODEL // N_MIXERS
HEAD_DIM = 64
N_HEADS = GROUP_W // HEAD_DIM
CONV_K = 3
W_LORA = 16
A_LORA = 16
G_LORA = 32
RWKV_DECAY_SCALE = math.exp(-0.5)
KV_HEADS = 2
Q_PER_KV = N_HEADS // KV_HEADS
KV_W = KV_HEADS * HEAD_DIM
WINDOW = 128
ATT_BLOCK = 128
ATT_SCALE = HEAD_DIM ** -0.5
ROPE_BASE = 10000.0
MLSTM_CHUNK = 128
N_GATE_COLS = 2 * 2 * N_HEADS
PEER_HEADS = 8
N_KEYS = 128
N_EXPERTS = N_KEYS * N_KEYS
PEER_TOPK = 16
PEER_QDIM = 256
PEER_HALF = PEER_QDIM // 2
PEER_BLOCK = 128
EPS = 1e-6
F32 = jnp.float32
BF16 = jnp.bfloat16
IN_SIZES = (GROUP_W, GROUP_W, GROUP_W,
            GROUP_W, GROUP_W, GROUP_W, W_LORA, A_LORA, G_LORA,
            GROUP_W, KV_W, KV_W,
            GROUP_W, GROUP_W, GROUP_W, GROUP_W, N_GATE_COLS)
D_IN = sum(IN_SIZES)
IN_OFFSETS = tuple(int(o) for o in np.cumsum(IN_SIZES)[:-1])

LANE = 128
VMEM_LIMIT_BYTES = 56 * 1024 * 1024


def _mm_kernel(x_ref, w_ref, o_ref):
    o_ref[...] = jnp.dot(x_ref[...].astype(BF16), w_ref[...], preferred_element_type=F32)


def _matmul(x, w, tm=512):
    M, K = x.shape
    N = w.shape[1]
    Np = -(-N // LANE) * LANE
    wb = w.astype(BF16)
    if Np != N:
        wb = jnp.pad(wb, ((0, 0), (0, Np - N)))
    tm = min(tm, M)
    assert M % tm == 0
    out = pl.pallas_call(
        _mm_kernel,
        grid=(M // tm,),
        in_specs=[pl.BlockSpec((tm, K), lambda i: (i, 0)),
                  pl.BlockSpec((K, Np), lambda i: (0, 0))],
        out_specs=pl.BlockSpec((tm, Np), lambda i: (i, 0)),
        out_shape=jax.ShapeDtypeStruct((M, Np), F32),
        compiler_params=pltpu.CompilerParams(
            dimension_semantics=("arbitrary",), vmem_limit_bytes=VMEM_LIMIT_BYTES),
        name="matmul",
    )(x, wb)
    return out


def _mm3(x, w):
    lead = x.shape[:-1]
    out = _matmul(x.reshape(-1, x.shape[-1]), w)
    return out.reshape(lead + (out.shape[1],))


_NEG_INF = float("-inf")


def _gelu_tanh(x):
    c = math.sqrt(2.0 / math.pi)
    return 0.5 * x * (1.0 + jnp.tanh(c * (x + 0.044715 * (x * x * x))))


_SUBLANES = 8


def _sort_network(n):
    pairs, p = [], 1
    while p < 16:
        k = p
        while k >= 1:
            for j in range(k % p, 16 - k, 2 * k):
                for i in range(min(k, 16 - j - k)):
                    if (i + j) // (2 * p) == (i + j + k) // (2 * p):
                        pairs.append((i + j, i + j + k))
            k //= 2
        p *= 2
    return [(a, b) for a, b in pairs if b < n]


def _top16_values(x):
    v = [x[_SUBLANES * k:_SUBLANES * (k + 1), :] for k in range(x.shape[0] // _SUBLANES)]
    for a, b in _sort_network(len(v)):
        v[a], v[b] = jnp.maximum(v[a], v[b]), jnp.minimum(v[a], v[b])
    sub = lax.broadcasted_iota(jnp.int32, v[0].shape, 0)
    vals = []
    for step in range(PEER_TOPK):
        mx = jnp.max(v[0], axis=0, keepdims=True)
        vals.append(mx)
        remaining = PEER_TOPK - 1 - step
        if remaining == 0:
            break
        hit = sub == jnp.min(jnp.where(v[0] == mx, sub, _SUBLANES), axis=0, keepdims=True)
        for k in range(min(remaining, len(v))):
            v[k] = jnp.where(hit, v[k + 1] if k + 1 < len(v) else _NEG_INF, v[k])
    return vals


def _peer_candidates(sv1, sv2):
    row8 = lax.broadcasted_iota(jnp.int32, (8, sv1.shape[1]), 0)
    blocks = [sv1[0:1, :] + sv2[0:8, :], sv1[0:1, :] + sv2[8:16, :]]
    for a in range(1, 8):
        n_valid = PEER_TOPK // (a + 1)
        blk = sv1[a:a + 1, :] + sv2[0:8, :]
        blocks.append(blk if n_valid >= 8 else jnp.where(row8 < n_valid, blk, _NEG_INF))
    blocks.append(sv1[8:16, :] + sv2[0:1, :])
    return jnp.concatenate(blocks, axis=0)


def _peer_kernel(h_ref, wq_ref, keys_ref, u_ref, vt_ref, o_ref,
                 hbt_ref, s_ref, sv_ref, e1_ref, cnt_ref, e2_ref, rank_ref, act_ref, wg_ref, *, tm, te):
    j = pl.program_id(1)
    n_lg = tm // LANE
    n_pair = n_lg // 2
    n_ib = te // N_KEYS

    @pl.when(j == 0)
    def _prepare():
        hb = h_ref[...].astype(BF16)
        hbt_ref[...] = h_ref[...].T.astype(BF16)
        qb = jnp.dot(hb, wq_ref[...], preferred_element_type=F32).astype(BF16)
        for hp in range(2 * PEER_HEADS):
            s_ref[hp] = lax.dot_general(keys_ref[hp], qb[:, hp * PEER_HALF:(hp + 1) * PEER_HALF],
                                        (((1,), (1,)), ((), ())), preferred_element_type=F32)

        def lane_groups(it):
            return it // n_pair, [pl.ds(pl.multiple_of(((it % n_pair) * 2 + half) * LANE, LANE), LANE)
                                  for half in range(2)]

        def top_body(it, carry):
            h, groups = lane_groups(it)
            for ls in groups:
                sv_ref[2 * h, :, ls] = jnp.concatenate(_top16_values(s_ref[2 * h, :, ls]), axis=0)
                s2 = s_ref[2 * h + 1, :, ls]
                vals = _top16_values(s2)
                sv_ref[2 * h + 1, :, ls] = jnp.concatenate(vals, axis=0)
                rank = jnp.zeros_like(s2)
                for val in vals:
                    rank = rank + jnp.where(val > s2, 1.0, 0.0)
                rank_ref[h, :, ls] = rank.astype(BF16)
            return carry

        lax.fori_loop(0, PEER_HEADS * n_pair, top_body, 0)

        def head_body(it, carry):
            h, groups = lane_groups(it)
            for ls in groups:
                sv1 = sv_ref[2 * h, :, ls]
                sv2 = sv_ref[2 * h + 1, :, ls]
                tv = _top16_values(_peer_candidates(sv1, sv2))
                thr = tv[PEER_TOPK - 1]
                z = jnp.zeros_like(thr)
                for t in tv:
                    z = z + jnp.exp(t - tv[0])
                s1 = s_ref[2 * h, :, ls]
                cnt = jnp.zeros_like(s1)
                for b in range(PEER_TOPK):
                    cnt = cnt + jnp.where(s1 + sv2[b:b + 1, :] >= thr, 1.0, 0.0)
                cnt_ref[h, :, ls] = cnt
                e1_ref[h, :, ls] = jnp.exp(s1 - sv1[0:1, :])
                e2_ref[h, :, ls] = (jnp.exp(s_ref[2 * h + 1, :, ls] - sv2[0:1, :]) / z).astype(BF16)
            return carry

        lax.fori_loop(0, PEER_HEADS * n_pair, head_body, 0)
        o_ref[...] = jnp.zeros_like(o_ref)

    i0 = pl.multiple_of(j * n_ib, 8)
    half_rows = te // 2
    act_ref[...] = jnp.dot(u_ref[...], hbt_ref[...], preferred_element_type=F32)
    for part in range(2):
        for lg in range(n_lg):
            ls = slice(lg * LANE, (lg + 1) * LANE)
            cnt8 = [cnt_ref[h, pl.ds(i0, 8), ls].astype(BF16) for h in range(PEER_HEADS)]
            e18 = [e1_ref[h, pl.ds(i0, 8), ls].astype(BF16) for h in range(PEER_HEADS)]
            for ib in range(part * n_ib // 2, (part + 1) * n_ib // 2):
                w = jnp.zeros((N_KEYS, LANE), BF16)
                for h in range(PEER_HEADS):
                    sel = rank_ref[h, :, ls] < cnt8[h][ib:ib + 1, :]
                    w = w + jnp.where(sel, e2_ref[h, :, ls] * e18[h][ib:ib + 1, :], jnp.zeros((), BF16))
                rs = slice(ib * N_KEYS, (ib + 1) * N_KEYS)
                wg_ref[rs, ls] = w * _gelu_tanh(act_ref[rs, ls].astype(BF16))
        rows = slice(part * half_rows, (part + 1) * half_rows)
        o_ref[...] += jnp.dot(vt_ref[0, :, rows], wg_ref[rows, :], preferred_element_type=F32)


_PEER_TE = 8 * N_KEYS


def _peer_v_tiles(v):
    E, D = v.shape
    return jnp.transpose(v.astype(BF16).reshape(E // _PEER_TE, _PEER_TE, D), (0, 2, 1))


def _peer_dense(hf, wq_b, keys_b, u_b, vt_b, tm=512):
    M, D = hf.shape
    E = u_b.shape[0]
    te = _PEER_TE
    tm = min(tm, M)
    assert M % tm == 0 and E % te == 0 and tm % (2 * LANE) == 0 and vt_b.shape == (E // te, D, te)
    kern = functools.partial(_peer_kernel, tm=tm, te=te)
    return pl.pallas_call(
        kern,
        grid=(M // tm, E // te),
        in_specs=[pl.BlockSpec((tm, D), lambda i, j: (i, 0)),
                  pl.BlockSpec(wq_b.shape, lambda i, j: (0, 0)),
                  pl.BlockSpec(keys_b.shape, lambda i, j: (0, 0, 0)),
                  pl.BlockSpec((te, D), lambda i, j: (j, 0)),
                  pl.BlockSpec((1, D, te), lambda i, j: (j, 0, 0))],
        out_specs=pl.BlockSpec((D, tm), lambda i, j: (0, i)),
        out_shape=jax.ShapeDtypeStruct((D, M), F32),
        scratch_shapes=[pltpu.VMEM((D, tm), BF16),
                        pltpu.VMEM((2 * PEER_HEADS, N_KEYS, tm), F32),
                        pltpu.VMEM((2 * PEER_HEADS, PEER_TOPK, tm), F32),
                        pltpu.VMEM((PEER_HEADS, N_KEYS, tm), F32),
                        pltpu.VMEM((PEER_HEADS, N_KEYS, tm), F32),
                        pltpu.VMEM((PEER_HEADS, N_KEYS, tm), BF16),
                        pltpu.VMEM((PEER_HEADS, N_KEYS, tm), BF16),
                        pltpu.VMEM((te, tm), F32),
                        pltpu.VMEM((te, tm), BF16)],
        compiler_params=pltpu.CompilerParams(
            dimension_semantics=("arbitrary", "arbitrary"), vmem_limit_bytes=VMEM_LIMIT_BYTES),
        name="peer_dense",
    )(hf, wq_b, keys_b, u_b, vt_b)


_RWKV_UNROLL = 8


def _rwkv_kernel(sh_f_ref, sh_b_ref, d0_ref, d1_ref, y0_ref, y1_ref, s_ref, sa_ref, *, tb, n_batch):
    i = pl.program_id(0)
    n_ch = 2 * n_batch
    n_tiles = tb // _RWKV_UNROLL

    @pl.when(i == 0)
    def _init():
        s_ref[...] = jnp.zeros_like(s_ref)

    lane = lax.broadcasted_iota(jnp.int32, (GROUP_W, GROUP_W), 1)
    sub = lax.broadcasted_iota(jnp.int32, (GROUP_W, GROUP_W), 0)
    seg_ones = jnp.where(lane // HEAD_DIM == sub // HEAD_DIM, 1.0, 0.0).astype(BF16)
    lane_v = lax.broadcasted_iota(jnp.int32, (HEAD_DIM, GROUP_W), 1)
    sub_v = lax.broadcasted_iota(jnp.int32, (HEAD_DIM, GROUP_W), 0)
    eye = (lane_v % HEAD_DIM == sub_v)

    def seg_sum(p):
        return jnp.dot(p.astype(BF16), seg_ones, preferred_element_type=F32)

    def chain_refs(c):
        d, b = divmod(c, n_batch)
        return d, b, (sh_f_ref, d0_ref) if d == 0 else (sh_b_ref, d1_ref)

    def tile_start(d, tt):
        return pl.multiple_of((tt if d == 0 else n_tiles - 1 - tt) * _RWKV_UNROLL, _RWKV_UNROLL)

    def first_kk(c, tt):
        d, b, (sh_ref, _) = chain_refs(c)
        kk8 = sh_ref[0, b, pl.ds(tile_start(d, tt), _RWKV_UNROLL), :]
        r0 = 0 if d == 0 else _RWKV_UNROLL - 1
        return kk8[r0:r0 + 1, :]

    sa_ref[...] = seg_sum(jnp.concatenate([s_ref[c] * first_kk(c, 0) for c in range(n_ch)], axis=0))

    def tile_body(tt, carry):
        rows, t8s, dirs = [], [], []
        for c in range(n_ch):
            d, b, (sh_ref, dr_ref) = chain_refs(c)
            t8 = tile_start(d, tt)
            rows.append([sh_ref[q, b, pl.ds(t8, _RWKV_UNROLL), :] for q in range(3)]
                        + [dr_ref[q, b, pl.ds(t8, _RWKV_UNROLL), :] for q in range(3)])
            t8s.append(t8)
            dirs.append(d)
        kk_next_tile = [first_kk(c, jnp.minimum(tt + 1, n_tiles - 1)) for c in range(n_ch)]

        def step_rows(step):
            return [step if d == 0 else _RWKV_UNROLL - 1 - step for d in dirs]

        vexp = seg_sum(jnp.concatenate([jnp.where(eye, rows[c][2][s:s + 1, :], 0.0)
                                        for c in range(n_ch) for s in range(_RWKV_UNROLL)], axis=0))
        S = [s_ref[c] for c in range(n_ch)]
        sa = [sa_ref[c * HEAD_DIM:(c + 1) * HEAD_DIM, :] for c in range(n_ch)]
        ys = [[None] * _RWKV_UNROLL for _ in range(n_ch)]
        for step in range(_RWKV_UNROLL):
            row = step_rows(step)

            def r_(c, q, rw=None):
                rw = row[c] if rw is None else rw
                return rows[c][q][rw:rw + 1, :]

            if step + 1 < _RWKV_UNROLL:
                kk_next = [r_(c, 0, row[c] + (1 if dirs[c] == 0 else -1)) for c in range(n_ch)]
            else:
                kk_next = kk_next_tile
            prods = []
            for c in range(n_ch):
                v0 = (c * _RWKV_UNROLL + row[c]) * HEAD_DIM
                ahead = S[c] * r_(c, 3) + vexp[v0:v0 + HEAD_DIM, :] * r_(c, 5)
                prods.append(ahead * kk_next[c] - sa[c] * (r_(c, 4) * kk_next[c]))
                S[c] = ahead - sa[c] * r_(c, 4)
            res = seg_sum(jnp.concatenate(prods, axis=0))
            sa = [res[c * HEAD_DIM:(c + 1) * HEAD_DIM, :] for c in range(n_ch)]
            for c in range(n_ch):
                ys[c][row[c]] = S[c] * r_(c, 1)
        ye = seg_sum(jnp.concatenate([ys[c][s] for c in range(n_ch) for s in range(_RWKV_UNROLL)], axis=0))
        for c in range(n_ch):
            for s in range(_RWKV_UNROLL):
                v0 = (c * _RWKV_UNROLL + s) * HEAD_DIM
                ys[c][s] = jnp.sum(jnp.where(eye, ye[v0:v0 + HEAD_DIM, :], 0.0), axis=0, keepdims=True)
        sa_ref[...] = jnp.concatenate(sa, axis=0)
        for c in range(n_ch):
            s_ref[c] = S[c]
            y_ref = y0_ref if dirs[c] == 0 else y1_ref
            y_ref[c % n_batch, pl.ds(t8s[c], _RWKV_UNROLL), :] = jnp.concatenate(ys[c], axis=0)
        return carry

    lax.fori_loop(0, n_tiles, tile_body, 0)


def _rwkv_scan_pallas(shared, dir0, dir1, n_ctx, tb=256):
    _, B, T, C = shared.shape
    assert T % tb == 0 and n_ctx % tb == 0 and tb % _RWKV_UNROLL == 0 and C == GROUP_W
    nblk, ncb = T // tb, n_ctx // tb

    def fwd3(i):
        return (0, 0, i, 0)

    def bwd_blk(i):
        return jnp.where(i < ncb, ncb - 1 - i, nblk - 1 - (i - ncb))

    def bwd3(i):
        return (0, 0, bwd_blk(i), 0)

    kern = functools.partial(_rwkv_kernel, tb=tb, n_batch=B)
    blk = (3, B, tb, C)
    return pl.pallas_call(
        kern,
        grid=(nblk,),
        in_specs=[pl.BlockSpec(blk, fwd3), pl.BlockSpec(blk, bwd3), pl.BlockSpec(blk, fwd3), pl.BlockSpec(blk, bwd3)],
        out_specs=[pl.BlockSpec((B, tb, C), lambda i: (0, i, 0)),
                   pl.BlockSpec((B, tb, C), lambda i: (0, bwd_blk(i), 0))],
        out_shape=[jax.ShapeDtypeStruct((B, T, C), F32)] * 2,
        scratch_shapes=[pltpu.VMEM((2 * B, HEAD_DIM, C), F32), pltpu.VMEM((2 * B * HEAD_DIM, C), F32)],
        compiler_params=pltpu.CompilerParams(
            dimension_semantics=("arbitrary",), vmem_limit_bytes=VMEM_LIMIT_BYTES),
        name="rwkv_scan",
    )(shared, shared, dir0, dir1)


def _mlstm_kernel(q_ref, k_ref, v_ref, row_ref, col_ref, h_ref, c_ref, n_ref, m_ref):
    g = pl.program_id(0)
    L = q_ref.shape[2]

    @pl.when(pl.program_id(1) == 0)
    def _init():
        c_ref[...] = jnp.zeros_like(c_ref)
        n_ref[...] = jnp.zeros_like(n_ref)
        m_ref[...] = jnp.zeros_like(m_ref)

    backward = (g % 2) == 1
    sgn = 1 - 2 * (g % 2)
    tt = lax.broadcasted_iota(jnp.int32, (L, L), 0)
    ss = lax.broadcasted_iota(jnp.int32, (L, L), 1)
    causal = (tt - ss) * sgn >= 0
    nt_dims = (((1,), (1,)), ((), ()))
    for h in range(N_HEADS):
        q = q_ref[0, h]
        k = k_ref[0, h]
        v = v_ref[0, h]
        qb, kb, vb = q.astype(BF16), k.astype(BF16), v.astype(BF16)
        brow = row_ref[0, h, 0, 0:1, :]
        lirow = row_ref[0, h, 0, 1:2, :]
        bcol = col_ref[0, h, 0, :, 0:1]
        licol = col_ref[0, h, 0, :, 1:2]
        m_prev = m_ref[h][:, 0:1]
        C = c_ref[h]
        n = n_ref[h]
        d_intra = jnp.where(causal, bcol - brow + lirow, _NEG_INF)
        d_inter = bcol + m_prev
        m_t = jnp.maximum(jnp.max(d_intra, axis=1, keepdims=True), d_inter)
        w_intra = jnp.exp(d_intra - m_t)
        w_inter = jnp.exp(d_inter - m_t)
        s = lax.dot_general(qb, kb, nt_dims, preferred_element_type=F32) * w_intra
        num = (jnp.dot(s.astype(BF16), vb, preferred_element_type=F32)
               + w_inter * lax.dot_general(qb, C.astype(BF16), nt_dims, preferred_element_type=F32))
        den = jnp.sum(s, axis=1, keepdims=True) + w_inter * jnp.sum(q * n, axis=1, keepdims=True)
        h_ref[0, h] = num / jnp.maximum(jnp.abs(den), jnp.exp(-m_t))
        b_end = jnp.where(backward, brow[:, 0:1], brow[:, L - 1:L])
        d_end = b_end - bcol + licol
        m_new = jnp.maximum(b_end + m_prev, jnp.max(d_end, axis=0, keepdims=True))
        w_end = jnp.exp(d_end - m_new)
        decay = jnp.exp(b_end + m_prev - m_new)
        wv_t = (w_end * v).T.astype(BF16)
        c_ref[h] = decay * C + jnp.dot(wv_t, kb, preferred_element_type=F32)
        n_ref[h] = decay * n + jnp.sum(w_end * k, axis=0, keepdims=True)
        m_ref[h] = jnp.broadcast_to(m_new, (1, LANE))


def _mlstm_scan_pallas(qh, kh, vh, rows, cols, n_ctx):
    B, H, T, Dh = qh.shape
    L = MLSTM_CHUNK
    nc, ncb = T // L, n_ctx // L
    assert T % L == 0 and n_ctx % L == 0 and H == N_HEADS

    def chunk(g, c):
        rev = jnp.where(c < ncb, ncb - 1 - c, nc - 1 - (c - ncb))
        return jnp.where(g % 2 == 0, c, rev)

    qkv_spec = pl.BlockSpec((1, H, L, Dh), lambda g, c: (g // 2, 0, chunk(g, c), 0))
    return pl.pallas_call(
        _mlstm_kernel,
        grid=(2 * B, nc),
        in_specs=[qkv_spec, qkv_spec, qkv_spec,
                  pl.BlockSpec((1, H, 1, 2, L), lambda g, c: (g, 0, chunk(g, c), 0, 0)),
                  pl.BlockSpec((1, H, 1, L, 2), lambda g, c: (g, 0, chunk(g, c), 0, 0))],
        out_specs=pl.BlockSpec((1, H, L, Dh), lambda g, c: (g, 0, chunk(g, c), 0)),
        out_shape=jax.ShapeDtypeStruct((2 * B, H, T, Dh), F32),
        scratch_shapes=[pltpu.VMEM((H, Dh, Dh), F32), pltpu.VMEM((H, 1, Dh), F32), pltpu.VMEM((H, 1, LANE), F32)],
        compiler_params=pltpu.CompilerParams(
            dimension_semantics=("arbitrary", "arbitrary"), vmem_limit_bytes=VMEM_LIMIT_BYTES),
        name="mlstm_scan",
    )(qh, kh, vh, rows, cols)


def _attn_kernel(*refs, n_band, t_total):
    q_ref = refs[0]
    band = refs[1:1 + 2 * n_band]
    kc_ref, vc_ref, sink_ref, o_ref = refs[1 + 2 * n_band:]
    n = pl.program_id(1)
    nt_dims = (((1,), (1,)), ((), ()))
    if n_band:
        qpos = n * ATT_BLOCK + lax.broadcasted_iota(jnp.int32, (ATT_BLOCK, n_band * ATT_BLOCK), 0)
        kpos = (n - 1) * ATT_BLOCK + lax.broadcasted_iota(jnp.int32, (ATT_BLOCK, n_band * ATT_BLOCK), 1)
        mask = (jnp.abs(qpos - kpos) <= WINDOW) & (kpos >= 0) & (kpos < t_total)
    for kvh in range(KV_HEADS):
        kc = kc_ref[0, kvh].astype(BF16)
        vc = vc_ref[0, kvh].astype(BF16)
        if n_band:
            kw = jnp.concatenate([band[j][0, kvh] for j in range(n_band)], axis=0).astype(BF16)
            vw = jnp.concatenate([band[n_band + j][0, kvh] for j in range(n_band)], axis=0).astype(BF16)
        for g in range(Q_PER_KV):
            h = kvh * Q_PER_KV + g
            q = q_ref[0, h].astype(BF16)
            sink = sink_ref[h][:, 0:1]
            s_ctx = lax.dot_general(q, kc, nt_dims, preferred_element_type=F32) * ATT_SCALE
            m = jnp.maximum(jnp.max(s_ctx, axis=1, keepdims=True), sink)
            if n_band:
                s_loc = lax.dot_general(q, kw, nt_dims, preferred_element_type=F32) * ATT_SCALE
                s_loc = jnp.where(mask, s_loc, _NEG_INF)
                m = jnp.maximum(m, jnp.max(s_loc, axis=1, keepdims=True))
            p_ctx = jnp.exp(s_ctx - m)
            den = jnp.sum(p_ctx, axis=1, keepdims=True) + jnp.exp(sink - m)
            o = jnp.dot(p_ctx.astype(BF16), vc, preferred_element_type=F32)
            if n_band:
                p_loc = jnp.exp(s_loc - m)
                den = den + jnp.sum(p_loc, axis=1, keepdims=True)
                o = o + jnp.dot(p_loc.astype(BF16), vw, preferred_element_type=F32)
            o_ref[0, h] = o / den


def _attention_pallas(q, k, v, kc, vc, sink):
    B, H, T, Dh = q.shape
    C = kc.shape[2]
    nb = T // ATT_BLOCK
    assert T % ATT_BLOCK == 0
    n_band = 0 if k is None else 3
    sink_b = jnp.broadcast_to(sink.astype(F32)[:, None, None], (H, 1, LANE))
    band_specs = [pl.BlockSpec((1, KV_HEADS, ATT_BLOCK, Dh),
                               lambda b, n, j=j: (b, 0, jnp.clip(n + j - 1, 0, nb - 1), 0)) for j in range(n_band)]
    ctx_spec = pl.BlockSpec((1, KV_HEADS, C, Dh), lambda b, n: (b, 0, 0, 0))
    kern = functools.partial(_attn_kernel, n_band=n_band, t_total=T)
    band_args = [] if k is None else [k] * 3 + [v] * 3
    return pl.pallas_call(
        kern,
        grid=(B, nb),
        in_specs=[pl.BlockSpec((1, H, ATT_BLOCK, Dh), lambda b, n: (b, 0, n, 0))] + band_specs * 2
                 + [ctx_spec, ctx_spec, pl.BlockSpec((H, 1, LANE), lambda b, n: (0, 0, 0))],
        out_specs=pl.BlockSpec((1, H, ATT_BLOCK, Dh), lambda b, n: (b, 0, n, 0)),
        out_shape=jax.ShapeDtypeStruct((B, H, T, Dh), F32),
        compiler_params=pltpu.CompilerParams(
            dimension_semantics=("arbitrary", "arbitrary"), vmem_limit_bytes=VMEM_LIMIT_BYTES),
        name="attention",
    )(q, *band_args, kc, vc, sink_b)


def rms_norm(x, g):
    xf = x.astype(F32)
    y = xf * lax.rsqrt(jnp.mean(xf * xf, axis=-1, keepdims=True) + EPS)
    return (y * g.astype(F32)).astype(x.dtype)


def heads(t):
    return t.reshape(t.shape[:-1] + (N_HEADS, HEAD_DIM))


def head_norm_merge(y, g):
    return rms_norm(y, g).reshape(y.shape[:-2] + (GROUP_W,))


def rope_2d(x, row, col):
    quarter = HEAD_DIM // 4
    inv = ROPE_BASE ** (-jnp.arange(quarter, dtype=F32) / quarter)
    xf = x.astype(F32)
    extra = (1,) * (x.ndim - 3)

    def rot(xa, pos):
        ang = pos.astype(F32)[:, None] * inv[None, :]
        ang = ang.reshape((1, ang.shape[0]) + extra + (quarter,))
        cos, sin = jnp.cos(ang), jnp.sin(ang)
        x1, x2 = xa[..., :quarter], xa[..., quarter:]
        return jnp.concatenate([x1 * cos - x2 * sin, x2 * cos + x1 * sin], axis=-1)

    half = HEAD_DIM // 2
    return jnp.concatenate([rot(xf[..., :half], row), rot(xf[..., half:], col)], axis=-1).astype(x.dtype)


def conv_mixer(hx, b_gate, c_gate, w, g):
    u = c_gate * hx
    up = jnp.pad(u, ((0, 0), (1, 1), (0, 0)))
    y = b_gate * (w[0] * up[:, :-2] + w[1] * up[:, 1:-1] + w[2] * up[:, 2:])
    return head_norm_merge(heads(y), g)


_RWKV_COL_BLOCK = IN_OFFSETS[2] // GROUP_W
assert IN_OFFSETS[2] % GROUP_W == 0 and W_LORA + A_LORA + G_LORA <= GROUP_W


def _rwkv_prep_kernel(r_ref, k_ref, v_ref, x_ref, vec_ref, w2_ref, a2_ref, g2_ref,
                      sh_ref, d0_ref, d1_ref, g_ref):
    r, k, v, x = r_ref[0], k_ref[0], v_ref[0], x_ref[0]
    lane = lax.broadcasted_iota(jnp.int32, (GROUP_W, GROUP_W), 1)
    sub = lax.broadcasted_iota(jnp.int32, (GROUP_W, GROUP_W), 0)
    seg_ones = jnp.where(lane // HEAD_DIM == sub // HEAD_DIM, 1.0, 0.0).astype(BF16)
    kkr = k * vec_ref[0:1, :]
    sq = kkr * kkr
    sq_hi = sq.astype(BF16)
    sq_lo = (sq - sq_hi.astype(F32)).astype(BF16)
    ss = (jnp.dot(sq_hi, seg_ones, preferred_element_type=F32)
          + jnp.dot(sq_lo, seg_ones, preferred_element_type=F32))
    kk = kkr * lax.rsqrt(ss + EPS)
    sh_ref[0, 0], sh_ref[1, 0], sh_ref[2, 0] = kk, r, v
    g_ref[0] = jnp.dot(jax.nn.sigmoid(x).astype(BF16), g2_ref[...], preferred_element_type=F32)
    xt = jnp.tanh(x).astype(BF16)
    xb = x.astype(BF16)
    for d, d_ref in enumerate((d0_ref, d1_ref)):
        lw = jnp.dot(xt, w2_ref[d], preferred_element_type=F32)
        la = jnp.dot(xb, a2_ref[d], preferred_element_type=F32)
        a = jax.nn.sigmoid(vec_ref[4 + d:5 + d, :] + la)
        d_ref[0, 0] = jnp.exp(-RWKV_DECAY_SCALE * jax.nn.sigmoid(vec_ref[2 + d:3 + d, :] + lw))
        d_ref[1, 0] = kk * a
        d_ref[2, 0] = k * (1 + (a - 1) * vec_ref[1:2, :])


def _rwkv_prep_pallas(p_all, w0, w2, a0, a2, g2, k_k, k_a, tb=256):
    B, T, _ = p_all.shape
    assert T % tb == 0
    C = GROUP_W
    vec = jnp.concatenate([k_k[None], k_a[None], w0, a0, jnp.zeros((2, C), F32)], axis=0)

    def pad_rows(w, row0):
        lead = w.shape[:-2]
        return jnp.zeros(lead + (C, C), BF16).at[..., row0:row0 + w.shape[-2], :].set(w.astype(BF16))

    w2p, a2p, g2p = pad_rows(w2, 0), pad_rows(a2, W_LORA), pad_rows(g2, W_LORA + A_LORA)
    col = lambda j: pl.BlockSpec((1, tb, C), lambda b, t, j=j: (b, t, _RWKV_COL_BLOCK + j))
    full = lambda a: pl.BlockSpec(a.shape, lambda b, t: (0,) * a.ndim)
    out3 = pl.BlockSpec((3, 1, tb, C), lambda b, t: (0, b, t, 0))
    return pl.pallas_call(
        _rwkv_prep_kernel,
        grid=(B, T // tb),
        in_specs=[col(0), col(1), col(2), col(3), full(vec), full(w2p), full(a2p), full(g2p)],
        out_specs=[out3, out3, out3, pl.BlockSpec((1, tb, C), lambda b, t: (b, t, 0))],
        out_shape=[jax.ShapeDtypeStruct((3, B, T, C), F32)] * 3 + [jax.ShapeDtypeStruct((B, T, C), F32)],
        compiler_params=pltpu.CompilerParams(
            dimension_semantics=("arbitrary", "arbitrary"), vmem_limit_bytes=VMEM_LIMIT_BYTES),
        name="rwkv_prep",
    )(p_all, p_all, p_all, p_all, vec, w2p, a2p, g2p)


def _rwkv_finish_kernel(y0_ref, y1_ref, r_ref, v_ref, kd0_ref, kd1_ref, g_ref, vec_ref, o_ref):
    lane = lax.broadcasted_iota(jnp.int32, (GROUP_W, GROUP_W), 1)
    sub = lax.broadcasted_iota(jnp.int32, (GROUP_W, GROUP_W), 0)
    seg_ones = jnp.where(lane // HEAD_DIM == sub // HEAD_DIM, 1.0, 0.0).astype(BF16)

    def head_sum(t):
        hi = t.astype(BF16)
        lo = (t - hi.astype(F32)).astype(BF16)
        return (jnp.dot(hi, seg_ones, preferred_element_type=F32)
                + jnp.dot(lo, seg_ones, preferred_element_type=F32))

    y = y0_ref[0] + y1_ref[0]
    yn = y * lax.rsqrt(head_sum(y * y) * (1.0 / HEAD_DIM) + EPS) * vec_ref[0:1, :]
    bonus = head_sum(r_ref[0, 0] * (kd0_ref[0, 0] + kd1_ref[0, 0]) * vec_ref[1:2, :]) * v_ref[0, 0]
    o_ref[0] = (yn + bonus) * g_ref[0]


def _rwkv_finish_pallas(y0, y1, shared, dir0, dir1, g, ln_g, r_k, tb=256):
    B, T, C = g.shape
    assert T % tb == 0
    vec = jnp.concatenate([ln_g.reshape(1, C), r_k.reshape(1, C), jnp.zeros((6, C), F32)], axis=0)
    row = pl.BlockSpec((1, tb, C), lambda b, t: (b, t, 0))
    plane = lambda q: pl.BlockSpec((1, 1, tb, C), lambda b, t, q=q: (q, b, t, 0))
    return pl.pallas_call(
        _rwkv_finish_kernel,
        grid=(B, T // tb),
        in_specs=[row, row, plane(1), plane(2), plane(2), plane(2), row, pl.BlockSpec(vec.shape, lambda b, t: (0, 0))],
        out_specs=row,
        out_shape=jax.ShapeDtypeStruct((B, T, C), F32),
        compiler_params=pltpu.CompilerParams(
            dimension_semantics=("arbitrary", "arbitrary"), vmem_limit_bytes=VMEM_LIMIT_BYTES),
        name="rwkv_finish",
    )(y0, y1, shared, shared, dir0, dir1, g, vec)


def rwkv_mixer(p_all, n_ctx, w0, w2, a0, a2, g2, k_k, k_a, r_k, ln_g, need_ctx):
    shared, dir0, dir1, g = _rwkv_prep_pallas(p_all, w0, w2, a0, a2, g2, k_k, k_a)
    y0, y1 = _rwkv_scan_pallas(shared, dir0, dir1, n_ctx)
    out = _rwkv_finish_pallas(y0, y1, shared, dir0, dir1, g, ln_g, r_k)
    return out[:, n_ctx:], (out[:, :n_ctx] if need_ctx else None)


def attn_project(q, k, v, q_g, k_g):
    B, T, _ = q.shape
    q = rms_norm(q.reshape(B, T, KV_HEADS, Q_PER_KV, HEAD_DIM), q_g)
    k = rms_norm(k.reshape(B, T, KV_HEADS, HEAD_DIM), k_g)
    v = v.reshape(B, T, KV_HEADS, HEAD_DIM)
    return q, k, v


def _head_major(t):
    B, T = t.shape[:2]
    return jnp.moveaxis(t.reshape(B, T, -1, HEAD_DIM), 2, 1)


def latent_attention(q, k, v, kc, vc, sink):
    o = _attention_pallas(_head_major(q), _head_major(k), _head_major(v), _head_major(kc), _head_major(vc), sink)
    return jnp.moveaxis(o, 1, 2)


def ctx_attention(qc, kc, vc, sink):
    o = _attention_pallas(_head_major(qc), None, None, _head_major(kc), _head_major(vc), sink)
    return jnp.moveaxis(o, 1, 2)


def mlstm_mixer(seq, n_ctx, i_b, f_b, out_g, need_ctx):
    q, k, v, o, gates = seq
    B, T, _ = q.shape
    L = MLSTM_CHUNK

    def th(t):
        return jnp.moveaxis(heads(t.astype(F32)), 2, 1)

    gates = gates.astype(F32).reshape(B, T, 2, 2, N_HEADS) + jnp.stack([i_b, f_b], axis=1).astype(F32)
    gates = jnp.moveaxis(gates, 1, -1)
    logi = gates[:, :, 0].reshape(B, 2, N_HEADS, T // L, L)
    logf = jax.nn.log_sigmoid(gates[:, :, 1]).reshape(B, 2, N_HEADS, T // L, L)
    bcum = jnp.stack([jnp.cumsum(logf[:, 0], axis=-1),
                      jnp.flip(jnp.cumsum(jnp.flip(logf[:, 1], axis=-1), axis=-1), axis=-1)], axis=1)
    rows = jnp.stack([bcum, logi], axis=-2).reshape(B * 2, N_HEADS, T // L, 2, L)
    cols = jnp.stack([bcum, logi], axis=-1).reshape(B * 2, N_HEADS, T // L, L, 2)
    h = _mlstm_scan_pallas(th(q), th(k) * (HEAD_DIM ** -0.5), th(v), rows, cols, n_ctx)
    h = h.reshape(B, 2, N_HEADS, T, HEAD_DIM)
    y = jax.nn.sigmoid(o) * head_norm_merge(jnp.moveaxis(h[:, 0] + h[:, 1], 1, 2), out_g)
    return y[:, n_ctx:], (y[:, :n_ctx] if need_ctx else None)


def kernel(x, c, ctx, c_ctx, ada_w, ada_b, norm1_g, norm2_g, w_in, w_out, conv_w, conv_g,
           rwkv_w0, rwkv_w2, rwkv_a0, rwkv_a2, rwkv_g2, rwkv_kk, rwkv_ka, rwkv_rk, rwkv_ln_g,
           att_q_g, att_k_g, att_sink, att_out_g, ml_i_b, ml_f_b, ml_out_g,
           peer_wq, peer_keys, peer_u, peer_v):
    B, T, D = x.shape
    n_ctx = ctx.shape[1]
    ROWS = T // GRID_W
    row = jnp.repeat(jnp.arange(ROWS), GRID_W)
    col = jnp.arange(ROWS * GRID_W) % GRID_W
    for l in range(DEPTH):
        need_ctx = l < DEPTH - 1
        mod = jax.nn.silu(c) @ ada_w[l] + ada_b[l]
        mod_c = jax.nn.silu(c_ctx) @ ada_w[l] + ada_b[l]
        sh1, sc1, gt1, sh2, sc2, gt2 = jnp.split(mod[:, None, :], 6, axis=-1)
        csh1, csc1, cgt1, csh2, csc2, cgt2 = jnp.split(mod_c, 6, axis=-1)

        h = rms_norm(x, norm1_g[l]) * (1 + sc1) + sh1
        hc = rms_norm(ctx, norm1_g[l]) * (1 + csc1) + csh1
        p_all = _mm3(jnp.concatenate([hc, h], axis=1), w_in[l])
        offs = (0,) + IN_OFFSETS
        S_ = [p_all[..., o_:o_ + n_] for o_, n_ in zip(offs, IN_SIZES)]
        P = [t[:, n_ctx:] for t in S_]
        Pc = [t[:, :n_ctx] for t in S_]

        y_a = conv_mixer(P[0], P[1], P[2], conv_w[l], conv_g[l])
        y_b, yc_b = rwkv_mixer(p_all, n_ctx, rwkv_w0[l], rwkv_w2[l], rwkv_a0[l], rwkv_a2[l],
                               rwkv_g2[l], rwkv_kk[l], rwkv_ka[l], rwkv_rk[l], rwkv_ln_g[l], need_ctx)
        q, k, v = attn_project(P[9], P[10], P[11], att_q_g[l], att_k_g[l])
        q, k = rope_2d(q, row, col), rope_2d(k, row, col)
        qc, kc, vc = attn_project(Pc[9], Pc[10], Pc[11], att_q_g[l], att_k_g[l])
        y_c = head_norm_merge(latent_attention(q, k, v, kc, vc, att_sink[l]), att_out_g[l])
        y_d, yc_d = mlstm_mixer(S_[12:17], n_ctx, ml_i_b[l], ml_f_b[l], ml_out_g[l], need_ctx)

        y = _mm3(jnp.concatenate([t.astype(x.dtype) for t in (y_a, y_b, y_c, y_d)], axis=-1), w_out[l])
        x = x + gt1 * y
        h2 = rms_norm(x, norm2_g[l]) * (1 + sc2) + sh2
        tok = [h2.reshape(B * T, D)]
        if need_ctx:
            yc_a = conv_mixer(Pc[0], Pc[1], Pc[2], conv_w[l], conv_g[l])
            yc_c = head_norm_merge(ctx_attention(qc, kc, vc, att_sink[l]), att_out_g[l])
            yc = _mm3(jnp.concatenate([t.astype(ctx.dtype) for t in (yc_a, yc_b, yc_c, yc_d)], axis=-1), w_out[l])
            ctx = ctx + cgt1 * yc
            hc2 = rms_norm(ctx, norm2_g[l]) * (1 + csc2) + csh2
            tok.append(hc2.reshape(-1, D))
        peer_t = _peer_dense(jnp.concatenate(tok, axis=0) if need_ctx else tok[0],
                             peer_wq[l].astype(BF16),
                             peer_keys[l].reshape(2 * PEER_HEADS, N_KEYS, PEER_HALF).astype(BF16),
                             peer_u[l].astype(BF16), _peer_v_tiles(peer_v[l]))
        x = x + gt2 * peer_t[:, :B * T].T.reshape(B, T, D)
        if need_ctx:
            ctx = ctx + cgt2 * peer_t[:, B * T:].T.reshape(ctx.shape)
    return x
```

```python
import functools
import math

import jax
import jax.numpy as jnp
import numpy as np
from jax import lax
from jax.experimental import pallas as pl
from jax.experimental.pallas import tpu as pltpu

D_MODEL = 1024
DEPTH = 2
GRID_W = 64
N_MIXERS = 4
GROUP_W = D_MODEL // N_MIXERS
HEAD_DIM = 64
N_HEADS = GROUP_W // HEAD_DIM
CONV_K = 3
W_LORA = 16
A_LORA = 16
G_LORA = 32
RWKV_DECAY_SCALE = math.exp(-0.5)
KV_HEADS = 2
Q_PER_KV = N_HEADS // KV_HEADS
KV_W = KV_HEADS * HEAD_DIM
WINDOW = 128
ATT_BLOCK = 128
ATT_SCALE = HEAD_DIM ** -0.5
ROPE_BASE = 10000.0
MLSTM_CHUNK = 128
N_GATE_COLS = 2 * 2 * N_HEADS
PEER_HEADS = 8
N_KEYS = 128
N_EXPERTS = N_KEYS * N_KEYS
PEER_TOPK = 16
PEER_QDIM = 256
PEER_HALF = PEER_QDIM // 2
PEER_BLOCK = 128
EPS = 1e-6
F32 = jnp.float32
BF16 = jnp.bfloat16
IN_SIZES = (GROUP_W, GROUP_W, GROUP_W,
            GROUP_W, GROUP_W, GROUP_W, W_LORA, A_LORA, G_LORA,
            GROUP_W, KV_W, KV_W,
            GROUP_W, GROUP_W, GROUP_W, GROUP_W, N_GATE_COLS)
D_IN = sum(IN_SIZES)
IN_OFFSETS = tuple(int(o) for o in np.cumsum(IN_SIZES)[:-1])

LANE = 128
VMEM_LIMIT_BYTES = 56 * 1024 * 1024


def _mm_kernel(x_ref, w_ref, o_ref):
    o_ref[...] = jnp.dot(x_ref[...].astype(BF16), w_ref[...], preferred_element_type=F32)


def _matmul(x, w, tm=512):
    M, K = x.shape
    N = w.shape[1]
    Np = -(-N // LANE) * LANE
    wb = w.astype(BF16)
    if Np != N:
        wb = jnp.pad(wb, ((0, 0), (0, Np - N)))
    tm = min(tm, M)
    assert M % tm == 0
    out = pl.pallas_call(
        _mm_kernel,
        grid=(M // tm,),
        in_specs=[pl.BlockSpec((tm, K), lambda i: (i, 0)),
                  pl.BlockSpec((K, Np), lambda i: (0, 0))],
        out_specs=pl.BlockSpec((tm, Np), lambda i: (i, 0)),
        out_shape=jax.ShapeDtypeStruct((M, Np), F32),
        compiler_params=pltpu.CompilerParams(
            dimension_semantics=("arbitrary",), vmem_limit_bytes=VMEM_LIMIT_BYTES),
        name="matmul",
    )(x, wb)
    return out


def _mm3(x, w):
    lead = x.shape[:-1]
    out = _matmul(x.reshape(-1, x.shape[-1]), w)
    return out.reshape(lead + (out.shape[1],))


_NEG_INF = float("-inf")


def _gelu_tanh(x):
    c = math.sqrt(2.0 / math.pi)
    return 0.5 * x * (1.0 + jnp.tanh(c * (x + 0.044715 * (x * x * x))))


_SUBLANES = 8


def _sort_network(n):
    pairs, p = [], 1
    while p < 16:
        k = p
        while k >= 1:
            for j in range(k % p, 16 - k, 2 * k):
                for i in range(min(k, 16 - j - k)):
                    if (i + j) // (2 * p) == (i + j + k) // (2 * p):
                        pairs.append((i + j, i + j + k))
            k //= 2
        p *= 2
    return [(a, b) for a, b in pairs if b < n]


def _top16_values(x):
    v = [x[_SUBLANES * k:_SUBLANES * (k + 1), :] for k in range(x.shape[0] // _SUBLANES)]
    for a, b in _sort_network(len(v)):
        v[a], v[b] = jnp.maximum(v[a], v[b]), jnp.minimum(v[a], v[b])
    sub = lax.broadcasted_iota(jnp.int32, v[0].shape, 0)
    vals = []
    for step in range(PEER_TOPK):
        mx = jnp.max(v[0], axis=0, keepdims=True)
        vals.append(mx)
        remaining = PEER_TOPK - 1 - step
        if remaining == 0:
            break
        hit = sub == jnp.min(jnp.where(v[0] == mx, sub, _SUBLANES), axis=0, keepdims=True)
        for k in range(min(remaining, len(v))):
            v[k] = jnp.where(hit, v[k + 1] if k + 1 < len(v) else _NEG_INF, v[k])
    return vals


def _peer_candidates(sv1, sv2):
    row8 = lax.broadcasted_iota(jnp.int32, (8, sv1.shape[1]), 0)
    blocks = [sv1[0:1, :] + sv2[0:8, :], sv1[0:1, :] + sv2[8:16, :]]
    for a in range(1, 8):
        n_valid = PEER_TOPK // (a + 1)
        blk = sv1[a:a + 1, :] + sv2[0:8, :]
        blocks.append(blk if n_valid >= 8 else jnp.where(row8 < n_valid, blk, _NEG_INF))
    blocks.append(sv1[8:16, :] + sv2[0:1, :])
    return jnp.concatenate(blocks, axis=0)


def _peer_kernel(h_ref, wq_ref, keys_ref, u_ref, vt_ref, res_ref, gate_ref, o_ref,
                 hbt_ref, s_ref, sv_ref, e1_ref, cnt_ref, e2_ref, rank_ref, act_ref, wg_ref, acc_ref, *, tm, te):
    j = pl.program_id(1)
    n_lg = tm // LANE
    n_pair = n_lg // 2
    n_ib = te // N_KEYS

    @pl.when(j == 0)
    def _prepare():
        hb = h_ref[...].astype(BF16)
        hbt_ref[...] = h_ref[...].T.astype(BF16)
        qb = jnp.dot(hb, wq_ref[...], preferred_element_type=F32).astype(BF16)
        for hp in range(2 * PEER_HEADS):
            s_ref[hp] = lax.dot_general(keys_ref[hp], qb[:, hp * PEER_HALF:(hp + 1) * PEER_HALF],
                                        (((1,), (1,)), ((), ())), preferred_element_type=F32)

        def lane_groups(it):
            return it // n_pair, [pl.ds(pl.multiple_of(((it % n_pair) * 2 + half) * LANE, LANE), LANE)
                                  for half in range(2)]

        def top_body(it, carry):
            h, groups = lane_groups(it)
            for ls in groups:
                sv_ref[2 * h, :, ls] = jnp.concatenate(_top16_values(s_ref[2 * h, :, ls]), axis=0)
                s2 = s_ref[2 * h + 1, :, ls]
                vals = _top16_values(s2)
                sv_ref[2 * h + 1, :, ls] = jnp.concatenate(vals, axis=0)
                rank = jnp.zeros_like(s2)
                for val in vals:
                    rank = rank + jnp.where(val > s2, 1.0, 0.0)
                rank_ref[h, :, ls] = rank.astype(BF16)
            return carry

        lax.fori_loop(0, PEER_HEADS * n_pair, top_body, 0)

        def head_body(it, carry):
            h, groups = lane_groups(it)
            for ls in groups:
                sv1 = sv_ref[2 * h, :, ls]
                sv2 = sv_ref[2 * h + 1, :, ls]
                tv = _top16_values(_peer_candidates(sv1, sv2))
                thr = tv[PEER_TOPK - 1]
                z = jnp.zeros_like(thr)
                for t in tv:
                    z = z + jnp.exp(t - tv[0])
                s1 = s_ref[2 * h, :, ls]
                cnt = jnp.zeros_like(s1)
                for b in range(PEER_TOPK):
                    cnt = cnt + jnp.where(s1 + sv2[b:b + 1, :] >= thr, 1.0, 0.0)
                cnt_ref[h, :, ls] = cnt
                e1_ref[h, :, ls] = jnp.exp(s1 - sv1[0:1, :])
                e2_ref[h, :, ls] = (jnp.exp(s_ref[2 * h + 1, :, ls] - sv2[0:1, :]) / z).astype(BF16)
            return carry

        lax.fori_loop(0, PEER_HEADS * n_pair, head_body, 0)
        acc_ref[...] = jnp.zeros_like(acc_ref)

    i0 = pl.multiple_of(j * n_ib, 8)
    half_rows = te // 2
    act_ref[...] = jnp.dot(u_ref[...], hbt_ref[...], preferred_element_type=F32)
    for part in range(2):
        for lg in range(n_lg):
            ls = slice(lg * LANE, (lg + 1) * LANE)
            cnt8 = [cnt_ref[h, pl.ds(i0, 8), ls].astype(BF16) for h in range(PEER_HEADS)]
            e18 = [e1_ref[h, pl.ds(i0, 8), ls].astype(BF16) for h in range(PEER_HEADS)]
            for ib in range(part * n_ib // 2, (part + 1) * n_ib // 2):
                w = jnp.zeros((N_KEYS, LANE), BF16)
                for h in range(PEER_HEADS):
                    sel = rank_ref[h, :, ls] < cnt8[h][ib:ib + 1, :]
                    w = w + jnp.where(sel, e2_ref[h, :, ls] * e18[h][ib:ib + 1, :], jnp.zeros((), BF16))
                rs = slice(ib * N_KEYS, (ib + 1) * N_KEYS)
                wg_ref[rs, ls] = w * _gelu_tanh(act_ref[rs, ls].astype(BF16))
        rows = slice(part * half_rows, (part + 1) * half_rows)
        acc_ref[...] += jnp.dot(vt_ref[0, :, rows], wg_ref[rows, :], preferred_element_type=F32)

    @pl.when(j == pl.num_programs(1) - 1)
    def _residual():
        o_ref[...] = res_ref[...] + gate_ref[0] * acc_ref[...].T


_PEER_TE = 8 * N_KEYS
PEER_TM = 512


def _peer_v_tiles(v):
    E, D = v.shape
    return jnp.transpose(v.astype(BF16).reshape(E // _PEER_TE, _PEER_TE, D), (0, 2, 1))


def _peer_dense(hf, wq_b, keys_b, u_b, vt_b, resid, gate_rows, tm):
    M, D = hf.shape
    E = u_b.shape[0]
    te = _PEER_TE
    assert M % tm == 0 and E % te == 0 and tm % (2 * LANE) == 0 and vt_b.shape == (E // te, D, te)
    assert resid.shape == (M, D) and gate_rows.shape == (M // tm, 1, D)
    kern = functools.partial(_peer_kernel, tm=tm, te=te)
    return pl.pallas_call(
        kern,
        grid=(M // tm, E // te),
        in_specs=[pl.BlockSpec((tm, D), lambda i, j: (i, 0)),
                  pl.BlockSpec(wq_b.shape, lambda i, j: (0, 0)),
                  pl.BlockSpec(keys_b.shape, lambda i, j: (0, 0, 0)),
                  pl.BlockSpec((te, D), lambda i, j: (j, 0)),
                  pl.BlockSpec((1, D, te), lambda i, j: (j, 0, 0)),
                  pl.BlockSpec((tm, D), lambda i, j: (i, 0)),
                  pl.BlockSpec((1, 1, D), lambda i, j: (i, 0, 0))],
        out_specs=pl.BlockSpec((tm, D), lambda i, j: (i, 0)),
        out_shape=jax.ShapeDtypeStruct((M, D), F32),
        scratch_shapes=[pltpu.VMEM((D, tm), BF16),
                        pltpu.VMEM((2 * PEER_HEADS, N_KEYS, tm), F32),
                        pltpu.VMEM((2 * PEER_HEADS, PEER_TOPK, tm), F32),
                        pltpu.VMEM((PEER_HEADS, N_KEYS, tm), F32),
                        pltpu.VMEM((PEER_HEADS, N_KEYS, tm), F32),
                        pltpu.VMEM((PEER_HEADS, N_KEYS, tm), BF16),
                        pltpu.VMEM((PEER_HEADS, N_KEYS, tm), BF16),
                        pltpu.VMEM((te, tm), F32),
                        pltpu.VMEM((te, tm), BF16),
                        pltpu.VMEM((D, tm), F32)],
        compiler_params=pltpu.CompilerParams(
            dimension_semantics=("arbitrary", "arbitrary"), vmem_limit_bytes=VMEM_LIMIT_BYTES),
        name="peer_dense",
    )(hf, wq_b, keys_b, u_b, vt_b, resid, gate_rows)


_RWKV_UNROLL = 8


def _rwkv_kernel(sh_f_ref, sh_b_ref, d0_ref, d1_ref, y0_ref, y1_ref, s_ref, sa_ref, *, tb, n_batch):
    i = pl.program_id(0)
    n_ch = 2 * n_batch
    n_tiles = tb // _RWKV_UNROLL

    @pl.when(i == 0)
    def _init():
        s_ref[...] = jnp.zeros_like(s_ref)

    lane = lax.broadcasted_iota(jnp.int32, (GROUP_W, GROUP_W), 1)
    sub = lax.broadcasted_iota(jnp.int32, (GROUP_W, GROUP_W), 0)
    seg_ones = jnp.where(lane // HEAD_DIM == sub // HEAD_DIM, 1.0, 0.0).astype(BF16)
    lane_v = lax.broadcasted_iota(jnp.int32, (HEAD_DIM, GROUP_W), 1)
    sub_v = lax.broadcasted_iota(jnp.int32, (HEAD_DIM, GROUP_W), 0)
    eye = (lane_v % HEAD_DIM == sub_v)

    def seg_sum(p):
        return jnp.dot(p.astype(BF16), seg_ones, preferred_element_type=F32)

    def chain_refs(c):
        d, b = divmod(c, n_batch)
        return d, b, (sh_f_ref, d0_ref) if d == 0 else (sh_b_ref, d1_ref)

    def tile_start(d, tt):
        return pl.multiple_of((tt if d == 0 else n_tiles - 1 - tt) * _RWKV_UNROLL, _RWKV_UNROLL)

    def first_kk(c, tt):
        d, b, (sh_ref, _) = chain_refs(c)
        kk8 = sh_ref[0, b, pl.ds(tile_start(d, tt), _RWKV_UNROLL), :]
        r0 = 0 if d == 0 else _RWKV_UNROLL - 1
        return kk8[r0:r0 + 1, :]

    sa_ref[...] = seg_sum(jnp.concatenate([s_ref[c] * first_kk(c, 0) for c in range(n_ch)], axis=0))

    def tile_body(tt, carry):
        rows, t8s, dirs = [], [], []
        for c in range(n_ch):
            d, b, (sh_ref, dr_ref) = chain_refs(c)
            t8 = tile_start(d, tt)
            rows.append([sh_ref[q, b, pl.ds(t8, _RWKV_UNROLL), :] for q in range(3)]
                        + [dr_ref[q, b, pl.ds(t8, _RWKV_UNROLL), :] for q in range(3)])
            t8s.append(t8)
            dirs.append(d)
        kk_next_tile = [first_kk(c, jnp.minimum(tt + 1, n_tiles - 1)) for c in range(n_ch)]

        def step_rows(step):
            return [step if d == 0 else _RWKV_UNROLL - 1 - step for d in dirs]

        vexp = seg_sum(jnp.concatenate([jnp.where(eye, rows[c][2][s:s + 1, :], 0.0)
                                        for c in range(n_ch) for s in range(_RWKV_UNROLL)], axis=0))
        S = [s_ref[c] for c in range(n_ch)]
        sa = [sa_ref[c * HEAD_DIM:(c + 1) * HEAD_DIM, :] for c in range(n_ch)]
        ys = [[None] * _RWKV_UNROLL for _ in range(n_ch)]
        for step in range(_RWKV_UNROLL):
            row = step_rows(step)

            def r_(c, q, rw=None):
                rw = row[c] if rw is None else rw
                return rows[c][q][rw:rw + 1, :]

            if step + 1 < _RWKV_UNROLL:
                kk_next = [r_(c, 0, row[c] + (1 if dirs[c] == 0 else -1)) for c in range(n_ch)]
            else:
                kk_next = kk_next_tile
            prods = []
            for c in range(n_ch):
                v0 = (c * _RWKV_UNROLL + row[c]) * HEAD_DIM
                ahead = S[c] * r_(c, 3) + vexp[v0:v0 + HEAD_DIM, :] * r_(c, 5)
                prods.append(ahead * kk_next[c] - sa[c] * (r_(c, 4) * kk_next[c]))
                S[c] = ahead - sa[c] * r_(c, 4)
            res = seg_sum(jnp.concatenate(prods, axis=0))
            sa = [res[c * HEAD_DIM:(c + 1) * HEAD_DIM, :] for c in range(n_ch)]
            for c in range(n_ch):
                ys[c][row[c]] = S[c] * r_(c, 1)
        ye = seg_sum(jnp.concatenate([ys[c][s] for c in range(n_ch) for s in range(_RWKV_UNROLL)], axis=0))
        for c in range(n_ch):
            for s in range(_RWKV_UNROLL):
                v0 = (c * _RWKV_UNROLL + s) * HEAD_DIM
                ys[c][s] = jnp.sum(jnp.where(eye, ye[v0:v0 + HEAD_DIM, :], 0.0), axis=0, keepdims=True)
        sa_ref[...] = jnp.concatenate(sa, axis=0)
        for c in range(n_ch):
            s_ref[c] = S[c]
            y_ref = y0_ref if dirs[c] == 0 else y1_ref
            y_ref[c % n_batch, pl.ds(t8s[c], _RWKV_UNROLL), :] = jnp.concatenate(ys[c], axis=0)
        return carry

    lax.fori_loop(0, n_tiles, tile_body, 0)


def _rwkv_scan_pallas(shared, dir0, dir1, n_ctx, tb=256):
    _, B, T, C = shared.shape
    assert T % tb == 0 and n_ctx % tb == 0 and tb % _RWKV_UNROLL == 0 and C == GROUP_W
    nblk, ncb = T // tb, n_ctx // tb

    def fwd3(i):
        return (0, 0, i, 0)

    def bwd_blk(i):
        return jnp.where(i < ncb, ncb - 1 - i, nblk - 1 - (i - ncb))

    def bwd3(i):
        return (0, 0, bwd_blk(i), 0)

    kern = functools.partial(_rwkv_kernel, tb=tb, n_batch=B)
    blk = (3, B, tb, C)
    return pl.pallas_call(
        kern,
        grid=(nblk,),
        in_specs=[pl.BlockSpec(blk, fwd3), pl.BlockSpec(blk, bwd3), pl.BlockSpec(blk, fwd3), pl.BlockSpec(blk, bwd3)],
        out_specs=[pl.BlockSpec((B, tb, C), lambda i: (0, i, 0)),
                   pl.BlockSpec((B, tb, C), lambda i: (0, bwd_blk(i), 0))],
        out_shape=[jax.ShapeDtypeStruct((B, T, C), F32)] * 2,
        scratch_shapes=[pltpu.VMEM((2 * B, HEAD_DIM, C), F32), pltpu.VMEM((2 * B * HEAD_DIM, C), F32)],
        compiler_params=pltpu.CompilerParams(
            dimension_semantics=("arbitrary",), vmem_limit_bytes=VMEM_LIMIT_BYTES),
        name="rwkv_scan",
    )(shared, shared, dir0, dir1)


def _mlstm_kernel(q_ref, k_ref, v_ref, row_ref, col_ref, h_ref, c_ref, n_ref, m_ref):
    g = pl.program_id(0)
    L = q_ref.shape[2]

    @pl.when(pl.program_id(1) == 0)
    def _init():
        c_ref[...] = jnp.zeros_like(c_ref)
        n_ref[...] = jnp.zeros_like(n_ref)
        m_ref[...] = jnp.zeros_like(m_ref)

    backward = (g % 2) == 1
    sgn = 1 - 2 * (g % 2)
    tt = lax.broadcasted_iota(jnp.int32, (L, L), 0)
    ss = lax.broadcasted_iota(jnp.int32, (L, L), 1)
    causal = (tt - ss) * sgn >= 0
    nt_dims = (((1,), (1,)), ((), ()))
    for h in range(N_HEADS):
        q = q_ref[0, h]
        k = k_ref[0, h]
        v = v_ref[0, h]
        qb, kb, vb = q.astype(BF16), k.astype(BF16), v.astype(BF16)
        brow = row_ref[0, h, 0, 0:1, :]
        lirow = row_ref[0, h, 0, 1:2, :]
        bcol = col_ref[0, h, 0, :, 0:1]
        licol = col_ref[0, h, 0, :, 1:2]
        m_prev = m_ref[h][:, 0:1]
        C = c_ref[h]
        n = n_ref[h]
        d_intra = jnp.where(causal, bcol - brow + lirow, _NEG_INF)
        d_inter = bcol + m_prev
        m_t = jnp.maximum(jnp.max(d_intra, axis=1, keepdims=True), d_inter)
        w_intra = jnp.exp(d_intra - m_t)
        w_inter = jnp.exp(d_inter - m_t)
        s = lax.dot_general(qb, kb, nt_dims, preferred_element_type=F32) * w_intra
        num = (jnp.dot(s.astype(BF16), vb, preferred_element_type=F32)
               + w_inter * lax.dot_general(qb, C.astype(BF16), nt_dims, preferred_element_type=F32))
        den = jnp.sum(s, axis=1, keepdims=True) + w_inter * jnp.sum(q * n, axis=1, keepdims=True)
        h_ref[0, h] = num / jnp.maximum(jnp.abs(den), jnp.exp(-m_t))
        b_end = jnp.where(backward, brow[:, 0:1], brow[:, L - 1:L])
        d_end = b_end - bcol + licol
        m_new = jnp.maximum(b_end + m_prev, jnp.max(d_end, axis=0, keepdims=True))
        w_end = jnp.exp(d_end - m_new)
        decay = jnp.exp(b_end + m_prev - m_new)
        wv_t = (w_end * v).T.astype(BF16)
        c_ref[h] = decay * C + jnp.dot(wv_t, kb, preferred_element_type=F32)
        n_ref[h] = decay * n + jnp.sum(w_end * k, axis=0, keepdims=True)
        m_ref[h] = jnp.broadcast_to(m_new, (1, LANE))


def _mlstm_scan_pallas(qh, kh, vh, rows, cols, n_ctx):
    B, H, T, Dh = qh.shape
    L = MLSTM_CHUNK
    nc, ncb = T // L, n_ctx // L
    assert T % L == 0 and n_ctx % L == 0 and H == N_HEADS

    def chunk(g, c):
        rev = jnp.where(c < ncb, ncb - 1 - c, nc - 1 - (c - ncb))
        return jnp.where(g % 2 == 0, c, rev)

    qkv_spec = pl.BlockSpec((1, H, L, Dh), lambda g, c: (g // 2, 0, chunk(g, c), 0))
    return pl.pallas_call(
        _mlstm_kernel,
        grid=(2 * B, nc),
        in_specs=[qkv_spec, qkv_spec, qkv_spec,
                  pl.BlockSpec((1, H, 1, 2, L), lambda g, c: (g, 0, chunk(g, c), 0, 0)),
                  pl.BlockSpec((1, H, 1, L, 2), lambda g, c: (g, 0, chunk(g, c), 0, 0))],
        out_specs=pl.BlockSpec((1, H, L, Dh), lambda g, c: (g, 0, chunk(g, c), 0)),
        out_shape=jax.ShapeDtypeStruct((2 * B, H, T, Dh), F32),
        scratch_shapes=[pltpu.VMEM((H, Dh, Dh), F32), pltpu.VMEM((H, 1, Dh), F32), pltpu.VMEM((H, 1, LANE), F32)],
        compiler_params=pltpu.CompilerParams(
            dimension_semantics=("arbitrary", "arbitrary"), vmem_limit_bytes=VMEM_LIMIT_BYTES),
        name="mlstm_scan",
    )(qh, kh, vh, rows, cols)


def _attn_kernel(*refs, n_band, t_total):
    q_ref = refs[0]
    band = refs[1:1 + 2 * n_band]
    kc_ref, vc_ref, sink_ref, o_ref = refs[1 + 2 * n_band:]
    n = pl.program_id(1)
    nt_dims = (((1,), (1,)), ((), ()))
    if n_band:
        qpos = n * ATT_BLOCK + lax.broadcasted_iota(jnp.int32, (ATT_BLOCK, n_band * ATT_BLOCK), 0)
        kpos = (n - 1) * ATT_BLOCK + lax.broadcasted_iota(jnp.int32, (ATT_BLOCK, n_band * ATT_BLOCK), 1)
        mask = (jnp.abs(qpos - kpos) <= WINDOW) & (kpos >= 0) & (kpos < t_total)
    for kvh in range(KV_HEADS):
        kc = kc_ref[0, kvh].astype(BF16)
        vc = vc_ref[0, kvh].astype(BF16)
        if n_band:
            kw = jnp.concatenate([band[j][0, kvh] for j in range(n_band)], axis=0).astype(BF16)
            vw = jnp.concatenate([band[n_band + j][0, kvh] for j in range(n_band)], axis=0).astype(BF16)
        for g in range(Q_PER_KV):
            h = kvh * Q_PER_KV + g
            q = q_ref[0, h].astype(BF16)
            sink = sink_ref[h][:, 0:1]
            s_ctx = lax.dot_general(q, kc, nt_dims, preferred_element_type=F32) * ATT_SCALE
            m = jnp.maximum(jnp.max(s_ctx, axis=1, keepdims=True), sink)
            if n_band:
                s_loc = lax.dot_general(q, kw, nt_dims, preferred_element_type=F32) * ATT_SCALE
                s_loc = jnp.where(mask, s_loc, _NEG_INF)
                m = jnp.maximum(m, jnp.max(s_loc, axis=1, keepdims=True))
            p_ctx = jnp.exp(s_ctx - m)
            den = jnp.sum(p_ctx, axis=1, keepdims=True) + jnp.exp(sink - m)
            o = jnp.dot(p_ctx.astype(BF16), vc, preferred_element_type=F32)
            if n_band:
                p_loc = jnp.exp(s_loc - m)
                den = den + jnp.sum(p_loc, axis=1, keepdims=True)
                o = o + jnp.dot(p_loc.astype(BF16), vw, preferred_element_type=F32)
            o_ref[0, h] = o / den


def _attention_pallas(q, k, v, kc, vc, sink):
    B, H, T, Dh = q.shape
    C = kc.shape[2]
    nb = T // ATT_BLOCK
    assert T % ATT_BLOCK == 0
    n_band = 0 if k is None else 3
    sink_b = jnp.broadcast_to(sink.astype(F32)[:, None, None], (H, 1, LANE))
    band_specs = [pl.BlockSpec((1, KV_HEADS, ATT_BLOCK, Dh),
                               lambda b, n, j=j: (b, 0, jnp.clip(n + j - 1, 0, nb - 1), 0)) for j in range(n_band)]
    ctx_spec = pl.BlockSpec((1, KV_HEADS, C, Dh), lambda b, n: (b, 0, 0, 0))
    kern = functools.partial(_attn_kernel, n_band=n_band, t_total=T)
    band_args = [] if k is None else [k] * 3 + [v] * 3
    return pl.pallas_call(
        kern,
        grid=(B, nb),
        in_specs=[pl.BlockSpec((1, H, ATT_BLOCK, Dh), lambda b, n: (b, 0, n, 0))] + band_specs * 2
                 + [ctx_spec, ctx_spec, pl.BlockSpec((H, 1, LANE), lambda b, n: (0, 0, 0))],
        out_specs=pl.BlockSpec((1, H, ATT_BLOCK, Dh), lambda b, n: (b, 0, n, 0)),
        out_shape=jax.ShapeDtypeStruct((B, H, T, Dh), F32),
        compiler_params=pltpu.CompilerParams(
            dimension_semantics=("arbitrary", "arbitrary"), vmem_limit_bytes=VMEM_LIMIT_BYTES),
        name="attention",
    )(q, *band_args, kc, vc, sink_b)


def rms_norm(x, g):
    xf = x.astype(F32)
    y = xf * lax.rsqrt(jnp.mean(xf * xf, axis=-1, keepdims=True) + EPS)
    return (y * g.astype(F32)).astype(x.dtype)


def heads(t):
    return t.reshape(t.shape[:-1] + (N_HEADS, HEAD_DIM))


def head_norm_merge(y, g):
    return rms_norm(y, g).reshape(y.shape[:-2] + (GROUP_W,))


def rope_2d(x, row, col):
    quarter = HEAD_DIM // 4
    inv = ROPE_BASE ** (-jnp.arange(quarter, dtype=F32) / quarter)
    xf = x.astype(F32)
    extra = (1,) * (x.ndim - 3)

    def rot(xa, pos):
        ang = pos.astype(F32)[:, None] * inv[None, :]
        ang = ang.reshape((1, ang.shape[0]) + extra + (quarter,))
        cos, sin = jnp.cos(ang), jnp.sin(ang)
        x1, x2 = xa[..., :quarter], xa[..., quarter:]
        return jnp.concatenate([x1 * cos - x2 * sin, x2 * cos + x1 * sin], axis=-1)

    half = HEAD_DIM // 2
    return jnp.concatenate([rot(xf[..., :half], row), rot(xf[..., half:], col)], axis=-1).astype(x.dtype)


def conv_mixer(hx, b_gate, c_gate, w, g):
    u = c_gate * hx
    up = jnp.pad(u, ((0, 0), (1, 1), (0, 0)))
    y = b_gate * (w[0] * up[:, :-2] + w[1] * up[:, 1:-1] + w[2] * up[:, 2:])
    return head_norm_merge(heads(y), g)


_RWKV_COL_BLOCK = IN_OFFSETS[2] // GROUP_W
assert IN_OFFSETS[2] % GROUP_W == 0 and W_LORA + A_LORA + G_LORA <= GROUP_W


def _rwkv_prep_kernel(r_ref, k_ref, v_ref, x_ref, vec_ref, w2_ref, a2_ref, g2_ref,
                      sh_ref, d0_ref, d1_ref, g_ref):
    r, k, v, x = r_ref[0], k_ref[0], v_ref[0], x_ref[0]
    lane = lax.broadcasted_iota(jnp.int32, (GROUP_W, GROUP_W), 1)
    sub = lax.broadcasted_iota(jnp.int32, (GROUP_W, GROUP_W), 0)
    seg_ones = jnp.where(lane // HEAD_DIM == sub // HEAD_DIM, 1.0, 0.0).astype(BF16)
    kkr = k * vec_ref[0:1, :]
    sq = kkr * kkr
    sq_hi = sq.astype(BF16)
    sq_lo = (sq - sq_hi.astype(F32)).astype(BF16)
    ss = (jnp.dot(sq_hi, seg_ones, preferred_element_type=F32)
          + jnp.dot(sq_lo, seg_ones, preferred_element_type=F32))
    kk = kkr * lax.rsqrt(ss + EPS)
    sh_ref[0, 0], sh_ref[1, 0], sh_ref[2, 0] = kk, r, v
    g_ref[0] = jnp.dot(jax.nn.sigmoid(x).astype(BF16), g2_ref[...], preferred_element_type=F32)
    xt = jnp.tanh(x).astype(BF16)
    xb = x.astype(BF16)
    for d, d_ref in enumerate((d0_ref, d1_ref)):
        lw = jnp.dot(xt, w2_ref[d], preferred_element_type=F32)
        la = jnp.dot(xb, a2_ref[d], preferred_element_type=F32)
        a = jax.nn.sigmoid(vec_ref[4 + d:5 + d, :] + la)
        d_ref[0, 0] = jnp.exp(-RWKV_DECAY_SCALE * jax.nn.sigmoid(vec_ref[2 + d:3 + d, :] + lw))
        d_ref[1, 0] = kk * a
        d_ref[2, 0] = k * (1 + (a - 1) * vec_ref[1:2, :])


def _rwkv_prep_pallas(p_all, w0, w2, a0, a2, g2, k_k, k_a, tb=256):
    B, T, _ = p_all.shape
    assert T % tb == 0
    C = GROUP_W
    vec = jnp.concatenate([k_k[None], k_a[None], w0, a0, jnp.zeros((2, C), F32)], axis=0)

    def pad_rows(w, row0):
        lead = w.shape[:-2]
        return jnp.zeros(lead + (C, C), BF16).at[..., row0:row0 + w.shape[-2], :].set(w.astype(BF16))

    w2p, a2p, g2p = pad_rows(w2, 0), pad_rows(a2, W_LORA), pad_rows(g2, W_LORA + A_LORA)
    col = lambda j: pl.BlockSpec((1, tb, C), lambda b, t, j=j: (b, t, _RWKV_COL_BLOCK + j))
    full = lambda a: pl.BlockSpec(a.shape, lambda b, t: (0,) * a.ndim)
    out3 = pl.BlockSpec((3, 1, tb, C), lambda b, t: (0, b, t, 0))
    return pl.pallas_call(
        _rwkv_prep_kernel,
        grid=(B, T // tb),
        in_specs=[col(0), col(1), col(2), col(3), full(vec), full(w2p), full(a2p), full(g2p)],
        out_specs=[out3, out3, out3, pl.BlockSpec((1, tb, C), lambda b, t: (b, t, 0))],
        out_shape=[jax.ShapeDtypeStruct((3, B, T, C), F32)] * 3 + [jax.ShapeDtypeStruct((B, T, C), F32)],
        compiler_params=pltpu.CompilerParams(
            dimension_semantics=("arbitrary", "arbitrary"), vmem_limit_bytes=VMEM_LIMIT_BYTES),
        name="rwkv_prep",
    )(p_all, p_all, p_all, p_all, vec, w2p, a2p, g2p)


def _rwkv_finish_kernel(y0_ref, y1_ref, r_ref, v_ref, kd0_ref, kd1_ref, g_ref, vec_ref, o_ref):
    lane = lax.broadcasted_iota(jnp.int32, (GROUP_W, GROUP_W), 1)
    sub = lax.broadcasted_iota(jnp.int32, (GROUP_W, GROUP_W), 0)
    seg_ones = jnp.where(lane // HEAD_DIM == sub // HEAD_DIM, 1.0, 0.0).astype(BF16)

    def head_sum(t):
        hi = t.astype(BF16)
        lo = (t - hi.astype(F32)).astype(BF16)
        return (jnp.dot(hi, seg_ones, preferred_element_type=F32)
                + jnp.dot(lo, seg_ones, preferred_element_type=F32))

    y = y0_ref[0] + y1_ref[0]
    yn = y * lax.rsqrt(head_sum(y * y) * (1.0 / HEAD_DIM) + EPS) * vec_ref[0:1, :]
    bonus = head_sum(r_ref[0, 0] * (kd0_ref[0, 0] + kd1_ref[0, 0]) * vec_ref[1:2, :]) * v_ref[0, 0]
    o_ref[0] = (yn + bonus) * g_ref[0]


def _rwkv_finish_pallas(y0, y1, shared, dir0, dir1, g, ln_g, r_k, tb=256):
    B, T, C = g.shape
    assert T % tb == 0
    vec = jnp.concatenate([ln_g.reshape(1, C), r_k.reshape(1, C), jnp.zeros((6, C), F32)], axis=0)
    row = pl.BlockSpec((1, tb, C), lambda b, t: (b, t, 0))
    plane = lambda q: pl.BlockSpec((1, 1, tb, C), lambda b, t, q=q: (q, b, t, 0))
    return pl.pallas_call(
        _rwkv_finish_kernel,
        grid=(B, T // tb),
        in_specs=[row, row, plane(1), plane(2), plane(2), plane(2), row, pl.BlockSpec(vec.shape, lambda b, t: (0, 0))],
        out_specs=row,
        out_shape=jax.ShapeDtypeStruct((B, T, C), F32),
        compiler_params=pltpu.CompilerParams(
            dimension_semantics=("arbitrary", "arbitrary"), vmem_limit_bytes=VMEM_LIMIT_BYTES),
        name="rwkv_finish",
    )(y0, y1, shared, shared, dir0, dir1, g, vec)


def rwkv_mixer(p_all, n_ctx, w0, w2, a0, a2, g2, k_k, k_a, r_k, ln_g, need_ctx):
    shared, dir0, dir1, g = _rwkv_prep_pallas(p_all, w0, w2, a0, a2, g2, k_k, k_a)
    y0, y1 = _rwkv_scan_pallas(shared, dir0, dir1, n_ctx)
    out = _rwkv_finish_pallas(y0, y1, shared, dir0, dir1, g, ln_g, r_k)
    return out[:, n_ctx:], (out[:, :n_ctx] if need_ctx else None)


def attn_project(q, k, v, q_g, k_g):
    B, T, _ = q.shape
    q = rms_norm(q.reshape(B, T, KV_HEADS, Q_PER_KV, HEAD_DIM), q_g)
    k = rms_norm(k.reshape(B, T, KV_HEADS, HEAD_DIM), k_g)
    v = v.reshape(B, T, KV_HEADS, HEAD_DIM)
    return q, k, v


def _head_major(t):
    B, T = t.shape[:2]
    return jnp.moveaxis(t.reshape(B, T, -1, HEAD_DIM), 2, 1)


def latent_attention(q, k, v, kc, vc, sink):
    o = _attention_pallas(_head_major(q), _head_major(k), _head_major(v), _head_major(kc), _head_major(vc), sink)
    return jnp.moveaxis(o, 1, 2)


def ctx_attention(qc, kc, vc, sink):
    o = _attention_pallas(_head_major(qc), None, None, _head_major(kc), _head_major(vc), sink)
    return jnp.moveaxis(o, 1, 2)


def mlstm_mixer(seq, n_ctx, i_b, f_b, out_g, need_ctx):
    q, k, v, o, gates = seq
    B, T, _ = q.shape
    L = MLSTM_CHUNK

    def th(t):
        return jnp.moveaxis(heads(t.astype(F32)), 2, 1)

    gates = gates.astype(F32).reshape(B, T, 2, 2, N_HEADS) + jnp.stack([i_b, f_b], axis=1).astype(F32)
    gates = jnp.moveaxis(gates, 1, -1)
    logi = gates[:, :, 0].reshape(B, 2, N_HEADS, T // L, L)
    logf = jax.nn.log_sigmoid(gates[:, :, 1]).reshape(B, 2, N_HEADS, T // L, L)
    bcum = jnp.stack([jnp.cumsum(logf[:, 0], axis=-1),
                      jnp.flip(jnp.cumsum(jnp.flip(logf[:, 1], axis=-1), axis=-1), axis=-1)], axis=1)
    rows = jnp.stack([bcum, logi], axis=-2).reshape(B * 2, N_HEADS, T // L, 2, L)
    cols = jnp.stack([bcum, logi], axis=-1).reshape(B * 2, N_HEADS, T // L, L, 2)
    h = _mlstm_scan_pallas(th(q), th(k) * (HEAD_DIM ** -0.5), th(v), rows, cols, n_ctx)
    h = h.reshape(B, 2, N_HEADS, T, HEAD_DIM)
    y = jax.nn.sigmoid(o) * head_norm_merge(jnp.moveaxis(h[:, 0] + h[:, 1], 1, 2), out_g)
    return y[:, n_ctx:], (y[:, :n_ctx] if need_ctx else None)


def kernel(x, c, ctx, c_ctx, ada_w, ada_b, norm1_g, norm2_g, w_in, w_out, conv_w, conv_g,
           rwkv_w0, rwkv_w2, rwkv_a0, rwkv_a2, rwkv_g2, rwkv_kk, rwkv_ka, rwkv_rk, rwkv_ln_g,
           att_q_g, att_k_g, att_sink, att_out_g, ml_i_b, ml_f_b, ml_out_g,
           peer_wq, peer_keys, peer_u, peer_v):
    B, T, D = x.shape
    n_ctx = ctx.shape[1]
    assert T % PEER_TM == 0 and (B * n_ctx) % PEER_TM == 0
    ROWS = T // GRID_W
    row = jnp.repeat(jnp.arange(ROWS), GRID_W)
    col = jnp.arange(ROWS * GRID_W) % GRID_W
    for l in range(DEPTH):
        need_ctx = l < DEPTH - 1
        mod = jax.nn.silu(c) @ ada_w[l] + ada_b[l]
        mod_c = jax.nn.silu(c_ctx) @ ada_w[l] + ada_b[l]
        sh1, sc1, gt1, sh2, sc2, gt2 = jnp.split(mod[:, None, :], 6, axis=-1)
        csh1, csc1, cgt1, csh2, csc2, cgt2 = jnp.split(mod_c, 6, axis=-1)

        h = rms_norm(x, norm1_g[l]) * (1 + sc1) + sh1
        hc = rms_norm(ctx, norm1_g[l]) * (1 + csc1) + csh1
        p_all = _mm3(jnp.concatenate([hc, h], axis=1), w_in[l])
        offs = (0,) + IN_OFFSETS
        S_ = [p_all[..., o_:o_ + n_] for o_, n_ in zip(offs, IN_SIZES)]
        P = [t[:, n_ctx:] for t in S_]
        Pc = [t[:, :n_ctx] for t in S_]

        y_a = conv_mixer(P[0], P[1], P[2], conv_w[l], conv_g[l])
        y_b, yc_b = rwkv_mixer(p_all, n_ctx, rwkv_w0[l], rwkv_w2[l], rwkv_a0[l], rwkv_a2[l],
                               rwkv_g2[l], rwkv_kk[l], rwkv_ka[l], rwkv_rk[l], rwkv_ln_g[l], need_ctx)
        q, k, v = attn_project(P[9], P[10], P[11], att_q_g[l], att_k_g[l])
        q, k = rope_2d(q, row, col), rope_2d(k, row, col)
        qc, kc, vc = attn_project(Pc[9], Pc[10], Pc[11], att_q_g[l], att_k_g[l])
        y_c = head_norm_merge(latent_attention(q, k, v, kc, vc, att_sink[l]), att_out_g[l])
        y_d, yc_d = mlstm_mixer(S_[12:17], n_ctx, ml_i_b[l], ml_f_b[l], ml_out_g[l], need_ctx)

        y = _mm3(jnp.concatenate([t.astype(x.dtype) for t in (y_a, y_b, y_c, y_d)], axis=-1), w_out[l])
        x = x + gt1 * y
        h2 = rms_norm(x, norm2_g[l]) * (1 + sc2) + sh2
        tok, resid = [h2.reshape(B * T, D)], [x.reshape(B * T, D)]
        gate_rows = [jnp.repeat(gt2.reshape(B, 1, D), T // PEER_TM, axis=0)]
        if need_ctx:
            yc_a = conv_mixer(Pc[0], Pc[1], Pc[2], conv_w[l], conv_g[l])
            yc_c = head_norm_merge(ctx_attention(qc, kc, vc, att_sink[l]), att_out_g[l])
            yc = _mm3(jnp.concatenate([t.astype(ctx.dtype) for t in (yc_a, yc_b, yc_c, yc_d)], axis=-1), w_out[l])
            ctx = ctx + cgt1 * yc
            hc2 = rms_norm(ctx, norm2_g[l]) * (1 + csc2) + csh2
            tok.append(hc2.reshape(-1, D))
            resid.append(ctx.reshape(-1, D))
            gate_rows.append(jnp.broadcast_to(cgt2.reshape(1, 1, D), (B * n_ctx // PEER_TM, 1, D)))
        new = _peer_dense(jnp.concatenate(tok, axis=0), peer_wq[l].astype(BF16),
                          peer_keys[l].reshape(2 * PEER_HEADS, N_KEYS, PEER_HALF).astype(BF16),
                          peer_u[l].astype(BF16), _peer_v_tiles(peer_v[l]),
                          jnp.concatenate(resid, axis=0), jnp.concatenate(gate_rows, axis=0), PEER_TM)
        x = new[:B * T].reshape(B, T, D)
        if need_ctx:
            ctx = new[B * T:].reshape(ctx.shape)
    return x
```

```python
import functools
import math

import jax
import jax.numpy as jnp
import numpy as np
from jax import lax
from jax.experimental import pallas as pl
from jax.experimental.pallas import tpu as pltpu

D_MODEL = 1024
DEPTH = 2
GRID_W = 64
N_MIXERS = 4
GROUP_W = D_MODEL // N_MIXERS
HEAD_DIM = 64
N_HEADS = GROUP_W // HEAD_DIM
W_LORA = 16
A_LORA = 16
G_LORA = 32
RWKV_DECAY_SCALE = math.exp(-0.5)
KV_HEADS = 2
Q_PER_KV = N_HEADS // KV_HEADS
KV_W = KV_HEADS * HEAD_DIM
WINDOW = 128
ATT_BLOCK = 128
ATT_SCALE = HEAD_DIM ** -0.5
ROPE_BASE = 10000.0
MLSTM_CHUNK = 128
N_GATE_COLS = 2 * 2 * N_HEADS
PEER_HEADS = 8
N_KEYS = 128
PEER_TOPK = 16
PEER_QDIM = 256
PEER_HALF = PEER_QDIM // 2
EPS = 1e-6
F32 = jnp.float32
BF16 = jnp.bfloat16
IN_SIZES = (GROUP_W, GROUP_W, GROUP_W,
            GROUP_W, GROUP_W, GROUP_W, W_LORA, A_LORA, G_LORA,
            GROUP_W, KV_W, KV_W,
            GROUP_W, GROUP_W, GROUP_W, GROUP_W, N_GATE_COLS)
IN_OFFSETS = tuple(int(o) for o in np.cumsum(IN_SIZES)[:-1])

LANE = 128
VMEM_LIMIT_BYTES = 56 * 1024 * 1024


def _mm_kernel(x_ref, w_ref, o_ref):
    o_ref[...] = jnp.dot(x_ref[...].astype(BF16), w_ref[...], preferred_element_type=F32)


def _matmul(x, w, tm=512):
    M, K = x.shape
    N = w.shape[1]
    Np = -(-N // LANE) * LANE
    wb = w.astype(BF16)
    if Np != N:
        wb = jnp.pad(wb, ((0, 0), (0, Np - N)))
    tm = min(tm, M)
    assert M % tm == 0
    out = pl.pallas_call(
        _mm_kernel,
        grid=(M // tm,),
        in_specs=[pl.BlockSpec((tm, K), lambda i: (i, 0)),
                  pl.BlockSpec((K, Np), lambda i: (0, 0))],
        out_specs=pl.BlockSpec((tm, Np), lambda i: (i, 0)),
        out_shape=jax.ShapeDtypeStruct((M, Np), F32),
        compiler_params=pltpu.CompilerParams(
            dimension_semantics=("arbitrary",), vmem_limit_bytes=VMEM_LIMIT_BYTES),
        name="matmul",
    )(x, wb)
    return out


def _mm3(x, w):
    lead = x.shape[:-1]
    out = _matmul(x.reshape(-1, x.shape[-1]), w)
    return out.reshape(lead + (out.shape[1],))


_NEG_INF = float("-inf")


def _gelu_tanh(x):
    c = math.sqrt(2.0 / math.pi)
    return 0.5 * x * (1.0 + jnp.tanh(c * (x + 0.044715 * (x * x * x))))


_SUBLANES = 8


def _sort_network(n):
    pairs, p = [], 1
    while p < 16:
        k = p
        while k >= 1:
            for j in range(k % p, 16 - k, 2 * k):
                for i in range(min(k, 16 - j - k)):
                    if (i + j) // (2 * p) == (i + j + k) // (2 * p):
                        pairs.append((i + j, i + j + k))
            k //= 2
        p *= 2
    return [(a, b) for a, b in pairs if b < n]


def _top16_values(x):
    v = [x[_SUBLANES * k:_SUBLANES * (k + 1), :] for k in range(x.shape[0] // _SUBLANES)]
    for a, b in _sort_network(len(v)):
        v[a], v[b] = jnp.maximum(v[a], v[b]), jnp.minimum(v[a], v[b])
    sub = lax.broadcasted_iota(jnp.int32, v[0].shape, 0)
    vals = []
    for step in range(PEER_TOPK):
        mx = jnp.max(v[0], axis=0, keepdims=True)
        vals.append(mx)
        remaining = PEER_TOPK - 1 - step
        if remaining == 0:
            break
        hit = sub == jnp.min(jnp.where(v[0] == mx, sub, _SUBLANES), axis=0, keepdims=True)
        for k in range(min(remaining, len(v))):
            v[k] = jnp.where(hit, v[k + 1] if k + 1 < len(v) else _NEG_INF, v[k])
    return vals


def _peer_candidates(sv1, sv2):
    row8 = lax.broadcasted_iota(jnp.int32, (8, sv1.shape[1]), 0)
    blocks = [sv1[0:1, :] + sv2[0:8, :], sv1[0:1, :] + sv2[8:16, :]]
    for a in range(1, 8):
        n_valid = PEER_TOPK // (a + 1)
        blk = sv1[a:a + 1, :] + sv2[0:8, :]
        blocks.append(blk if n_valid >= 8 else jnp.where(row8 < n_valid, blk, _NEG_INF))
    blocks.append(sv1[8:16, :] + sv2[0:1, :])
    return jnp.concatenate(blocks, axis=0)


def _peer_kernel(h_ref, wq_ref, keys_ref, u_ref, vt_ref, res_ref, gate_ref, o_ref,
                 hbt_ref, s_ref, sv_ref, e1_ref, cnt_ref, e2_ref, rank_ref, act_ref, wg_ref, acc_ref, *, tm, te):
    j = pl.program_id(1)
    n_lg = tm // LANE
    n_pair = n_lg // 2
    n_ib = te // N_KEYS

    @pl.when(j == 0)
    def _prepare():
        hb = h_ref[...].astype(BF16)
        hbt_ref[...] = h_ref[...].T.astype(BF16)
        qb = jnp.dot(hb, wq_ref[...], preferred_element_type=F32).astype(BF16)
        for hp in range(2 * PEER_HEADS):
            s_ref[hp] = lax.dot_general(keys_ref[hp], qb[:, hp * PEER_HALF:(hp + 1) * PEER_HALF],
                                        (((1,), (1,)), ((), ())), preferred_element_type=F32)

        def lane_groups(it):
            return it // n_pair, [pl.ds(pl.multiple_of(((it % n_pair) * 2 + half) * LANE, LANE), LANE)
                                  for half in range(2)]

        def top_body(it, carry):
            h, groups = lane_groups(it)
            for ls in groups:
                sv_ref[2 * h, :, ls] = jnp.concatenate(_top16_values(s_ref[2 * h, :, ls]), axis=0)
                s2 = s_ref[2 * h + 1, :, ls]
                vals = _top16_values(s2)
                sv_ref[2 * h + 1, :, ls] = jnp.concatenate(vals, axis=0)
                rank = jnp.zeros_like(s2)
                for val in vals:
                    rank = rank + jnp.where(val > s2, 1.0, 0.0)
                rank_ref[h, :, ls] = rank.astype(BF16)
            return carry

        lax.fori_loop(0, PEER_HEADS * n_pair, top_body, 0)

        def head_body(it, carry):
            h, groups = lane_groups(it)
            for ls in groups:
                sv1 = sv_ref[2 * h, :, ls]
                sv2 = sv_ref[2 * h + 1, :, ls]
                tv = _top16_values(_peer_candidates(sv1, sv2))
                thr = tv[PEER_TOPK - 1]
                z = jnp.zeros_like(thr)
                for t in tv:
                    z = z + jnp.exp(t - tv[0])
                s1 = s_ref[2 * h, :, ls]
                cnt = jnp.zeros_like(s1)
                for b in range(PEER_TOPK):
                    cnt = cnt + jnp.where(s1 + sv2[b:b + 1, :] >= thr, 1.0, 0.0)
                cnt_ref[h, :, ls] = cnt
                e1_ref[h, :, ls] = jnp.exp(s1 - sv1[0:1, :])
                e2_ref[h, :, ls] = (jnp.exp(s_ref[2 * h + 1, :, ls] - sv2[0:1, :]) / z).astype(BF16)
            return carry

        lax.fori_loop(0, PEER_HEADS * n_pair, head_body, 0)
        acc_ref[...] = jnp.zeros_like(acc_ref)

    i0 = pl.multiple_of(j * n_ib, 8)
    half_rows = te // 2
    act_ref[...] = jnp.dot(u_ref[...], hbt_ref[...], preferred_element_type=F32)
    for part in range(2):
        for lg in range(n_lg):
            ls = slice(lg * LANE, (lg + 1) * LANE)
            cnt8 = [cnt_ref[h, pl.ds(i0, 8), ls].astype(BF16) for h in range(PEER_HEADS)]
            e18 = [e1_ref[h, pl.ds(i0, 8), ls].astype(BF16) for h in range(PEER_HEADS)]
            for ib in range(part * n_ib // 2, (part + 1) * n_ib // 2):
                w = jnp.zeros((N_KEYS, LANE), BF16)
                for h in range(PEER_HEADS):
                    sel = rank_ref[h, :, ls] < cnt8[h][ib:ib + 1, :]
                    w = w + jnp.where(sel, e2_ref[h, :, ls] * e18[h][ib:ib + 1, :], jnp.zeros((), BF16))
                rs = slice(ib * N_KEYS, (ib + 1) * N_KEYS)
                wg_ref[rs, ls] = w * _gelu_tanh(act_ref[rs, ls].astype(BF16))
        rows = slice(part * half_rows, (part + 1) * half_rows)
        acc_ref[...] += jnp.dot(vt_ref[0, :, rows], wg_ref[rows, :], preferred_element_type=F32)

    @pl.when(j == pl.num_programs(1) - 1)
    def _residual():
        o_ref[...] = res_ref[...] + gate_ref[0] * acc_ref[...].T


_PEER_TE = 8 * N_KEYS
PEER_TM = 512


def _peer_v_tiles(v):
    E, D = v.shape
    return jnp.transpose(v.astype(BF16).reshape(E // _PEER_TE, _PEER_TE, D), (0, 2, 1))


def _peer_dense(hf, wq_b, keys_b, u_b, vt_b, resid, gate_rows, tm):
    M, D = hf.shape
    E = u_b.shape[0]
    te = _PEER_TE
    assert M % tm == 0 and E % te == 0 and tm % (2 * LANE) == 0 and vt_b.shape == (E // te, D, te)
    assert resid.shape == (M, D) and gate_rows.shape == (M // tm, 1, D)
    kern = functools.partial(_peer_kernel, tm=tm, te=te)
    return pl.pallas_call(
        kern,
        grid=(M // tm, E // te),
        in_specs=[pl.BlockSpec((tm, D), lambda i, j: (i, 0)),
                  pl.BlockSpec(wq_b.shape, lambda i, j: (0, 0)),
                  pl.BlockSpec(keys_b.shape, lambda i, j: (0, 0, 0)),
                  pl.BlockSpec((te, D), lambda i, j: (j, 0)),
                  pl.BlockSpec((1, D, te), lambda i, j: (j, 0, 0)),
                  pl.BlockSpec((tm, D), lambda i, j: (i, 0)),
                  pl.BlockSpec((1, 1, D), lambda i, j: (i, 0, 0))],
        out_specs=pl.BlockSpec((tm, D), lambda i, j: (i, 0)),
        out_shape=jax.ShapeDtypeStruct((M, D), F32),
        scratch_shapes=[pltpu.VMEM((D, tm), BF16),
                        pltpu.VMEM((2 * PEER_HEADS, N_KEYS, tm), F32),
                        pltpu.VMEM((2 * PEER_HEADS, PEER_TOPK, tm), F32),
                        pltpu.VMEM((PEER_HEADS, N_KEYS, tm), F32),
                        pltpu.VMEM((PEER_HEADS, N_KEYS, tm), F32),
                        pltpu.VMEM((PEER_HEADS, N_KEYS, tm), BF16),
                        pltpu.VMEM((PEER_HEADS, N_KEYS, tm), BF16),
                        pltpu.VMEM((te, tm), F32),
                        pltpu.VMEM((te, tm), BF16),
                        pltpu.VMEM((D, tm), F32)],
        compiler_params=pltpu.CompilerParams(
            dimension_semantics=("arbitrary", "arbitrary"), vmem_limit_bytes=VMEM_LIMIT_BYTES),
        name="peer_dense",
    )(hf, wq_b, keys_b, u_b, vt_b, resid, gate_rows)


_RWKV_UNROLL = 8


def _rwkv_kernel(sh_f_ref, sh_b_ref, d0_ref, d1_ref, y0_ref, y1_ref, s_ref, sa_ref, *, tb, n_batch):
    i = pl.program_id(0)
    n_ch = 2 * n_batch
    n_tiles = tb // _RWKV_UNROLL

    @pl.when(i == 0)
    def _init():
        s_ref[...] = jnp.zeros_like(s_ref)

    lane = lax.broadcasted_iota(jnp.int32, (GROUP_W, GROUP_W), 1)
    sub = lax.broadcasted_iota(jnp.int32, (GROUP_W, GROUP_W), 0)
    seg_ones = jnp.where(lane // HEAD_DIM == sub // HEAD_DIM, 1.0, 0.0).astype(BF16)
    lane_v = lax.broadcasted_iota(jnp.int32, (HEAD_DIM, GROUP_W), 1)
    sub_v = lax.broadcasted_iota(jnp.int32, (HEAD_DIM, GROUP_W), 0)
    eye = (lane_v % HEAD_DIM == sub_v)

    def seg_sum(p):
        return jnp.dot(p.astype(BF16), seg_ones, preferred_element_type=F32)

    def chain_refs(c):
        d, b = divmod(c, n_batch)
        return d, b, (sh_f_ref, d0_ref) if d == 0 else (sh_b_ref, d1_ref)

    def tile_start(d, tt):
        return pl.multiple_of((tt if d == 0 else n_tiles - 1 - tt) * _RWKV_UNROLL, _RWKV_UNROLL)

    def first_kk(c, tt):
        d, b, (sh_ref, _) = chain_refs(c)
        kk8 = sh_ref[0, b, pl.ds(tile_start(d, tt), _RWKV_UNROLL), :]
        r0 = 0 if d == 0 else _RWKV_UNROLL - 1
        return kk8[r0:r0 + 1, :]

    sa_ref[...] = seg_sum(jnp.concatenate([s_ref[c] * first_kk(c, 0) for c in range(n_ch)], axis=0))

    def tile_body(tt, carry):
        rows, t8s, dirs = [], [], []
        for c in range(n_ch):
            d, b, (sh_ref, dr_ref) = chain_refs(c)
            t8 = tile_start(d, tt)
            rows.append([sh_ref[q, b, pl.ds(t8, _RWKV_UNROLL), :] for q in range(3)]
                        + [dr_ref[q, b, pl.ds(t8, _RWKV_UNROLL), :] for q in range(3)])
            t8s.append(t8)
            dirs.append(d)
        kk_next_tile = [first_kk(c, jnp.minimum(tt + 1, n_tiles - 1)) for c in range(n_ch)]

        def step_rows(step):
            return [step if d == 0 else _RWKV_UNROLL - 1 - step for d in dirs]

        vexp = seg_sum(jnp.concatenate([jnp.where(eye, rows[c][2][s:s + 1, :], 0.0)
                                        for c in range(n_ch) for s in range(_RWKV_UNROLL)], axis=0))
        S = [s_ref[c] for c in range(n_ch)]
        sa = [sa_ref[c * HEAD_DIM:(c + 1) * HEAD_DIM, :] for c in range(n_ch)]
        ys = [[None] * _RWKV_UNROLL for _ in range(n_ch)]
        for step in range(_RWKV_UNROLL):
            row = step_rows(step)

            def r_(c, q, rw=None):
                rw = row[c] if rw is None else rw
                return rows[c][q][rw:rw + 1, :]

            if step + 1 < _RWKV_UNROLL:
                kk_next = [r_(c, 0, row[c] + (1 if dirs[c] == 0 else -1)) for c in range(n_ch)]
            else:
                kk_next = kk_next_tile
            prods = []
            for c in range(n_ch):
                v0 = (c * _RWKV_UNROLL + row[c]) * HEAD_DIM
                ahead = S[c] * r_(c, 3) + vexp[v0:v0 + HEAD_DIM, :] * r_(c, 5)
                prods.append(ahead * kk_next[c] - sa[c] * (r_(c, 4) * kk_next[c]))
                S[c] = ahead - sa[c] * r_(c, 4)
            res = seg_sum(jnp.concatenate(prods, axis=0))
            sa = [res[c * HEAD_DIM:(c + 1) * HEAD_DIM, :] for c in range(n_ch)]
            for c in range(n_ch):
                ys[c][row[c]] = S[c] * r_(c, 1)
        ye = seg_sum(jnp.concatenate([ys[c][s] for c in range(n_ch) for s in range(_RWKV_UNROLL)], axis=0))
        for c in range(n_ch):
            for s in range(_RWKV_UNROLL):
                v0 = (c * _RWKV_UNROLL + s) * HEAD_DIM
                ys[c][s] = jnp.sum(jnp.where(eye, ye[v0:v0 + HEAD_DIM, :], 0.0), axis=0, keepdims=True)
        sa_ref[...] = jnp.concatenate(sa, axis=0)
        for c in range(n_ch):
            s_ref[c] = S[c]
            y_ref = y0_ref if dirs[c] == 0 else y1_ref
            y_ref[c % n_batch, pl.ds(t8s[c], _RWKV_UNROLL), :] = jnp.concatenate(ys[c], axis=0)
        return carry

    lax.fori_loop(0, n_tiles, tile_body, 0)


def _rwkv_scan_pallas(shared, dir0, dir1, n_ctx, tb=256):
    _, B, T, C = shared.shape
    assert T % tb == 0 and n_ctx % tb == 0 and tb % _RWKV_UNROLL == 0 and C == GROUP_W
    nblk, ncb = T // tb, n_ctx // tb

    def fwd3(i):
        return (0, 0, i, 0)

    def bwd_blk(i):
        return jnp.where(i < ncb, ncb - 1 - i, nblk - 1 - (i - ncb))

    def bwd3(i):
        return (0, 0, bwd_blk(i), 0)

    kern = functools.partial(_rwkv_kernel, tb=tb, n_batch=B)
    blk = (3, B, tb, C)
    return pl.pallas_call(
        kern,
        grid=(nblk,),
        in_specs=[pl.BlockSpec(blk, fwd3), pl.BlockSpec(blk, bwd3), pl.BlockSpec(blk, fwd3), pl.BlockSpec(blk, bwd3)],
        out_specs=[pl.BlockSpec((B, tb, C), lambda i: (0, i, 0)),
                   pl.BlockSpec((B, tb, C), lambda i: (0, bwd_blk(i), 0))],
        out_shape=[jax.ShapeDtypeStruct((B, T, C), F32)] * 2,
        scratch_shapes=[pltpu.VMEM((2 * B, HEAD_DIM, C), F32), pltpu.VMEM((2 * B * HEAD_DIM, C), F32)],
        compiler_params=pltpu.CompilerParams(
            dimension_semantics=("arbitrary",), vmem_limit_bytes=VMEM_LIMIT_BYTES),
        name="rwkv_scan",
    )(shared, shared, dir0, dir1)


def _mlstm_kernel(q_ref, k_ref, v_ref, row_ref, col_ref, h_ref, c_ref, n_ref, m_ref):
    g = pl.program_id(0)
    L = q_ref.shape[2]

    @pl.when(pl.program_id(1) == 0)
    def _init():
        c_ref[...] = jnp.zeros_like(c_ref)
        n_ref[...] = jnp.zeros_like(n_ref)
        m_ref[...] = jnp.zeros_like(m_ref)

    backward = (g % 2) == 1
    sgn = 1 - 2 * (g % 2)
    tt = lax.broadcasted_iota(jnp.int32, (L, L), 0)
    ss = lax.broadcasted_iota(jnp.int32, (L, L), 1)
    causal = (tt - ss) * sgn >= 0
    nt_dims = (((1,), (1,)), ((), ()))
    for h in range(N_HEADS):
        q = q_ref[0, h]
        k = k_ref[0, h]
        v = v_ref[0, h]
        qb, kb, vb = q.astype(BF16), k.astype(BF16), v.astype(BF16)
        brow = row_ref[0, h, 0, 0:1, :]
        lirow = row_ref[0, h, 0, 1:2, :]
        bcol = col_ref[0, h, 0, :, 0:1]
        licol = col_ref[0, h, 0, :, 1:2]
        m_prev = m_ref[h][:, 0:1]
        C = c_ref[h]
        n = n_ref[h]
        d_intra = jnp.where(causal, bcol - brow + lirow, _NEG_INF)
        d_inter = bcol + m_prev
        m_t = jnp.maximum(jnp.max(d_intra, axis=1, keepdims=True), d_inter)
        w_intra = jnp.exp(d_intra - m_t)
        w_inter = jnp.exp(d_inter - m_t)
        s = lax.dot_general(qb, kb, nt_dims, preferred_element_type=F32) * w_intra
        num = (jnp.dot(s.astype(BF16), vb, preferred_element_type=F32)
               + w_inter * lax.dot_general(qb, C.astype(BF16), nt_dims, preferred_element_type=F32))
        den = jnp.sum(s, axis=1, keepdims=True) + w_inter * jnp.sum(q * n, axis=1, keepdims=True)
        h_ref[0, h] = num / jnp.maximum(jnp.abs(den), jnp.exp(-m_t))
        b_end = jnp.where(backward, brow[:, 0:1], brow[:, L - 1:L])
        d_end = b_end - bcol + licol
        m_new = jnp.maximum(b_end + m_prev, jnp.max(d_end, axis=0, keepdims=True))
        w_end = jnp.exp(d_end - m_new)
        decay = jnp.exp(b_end + m_prev - m_new)
        wv_t = (w_end * v).T.astype(BF16)
        c_ref[h] = decay * C + jnp.dot(wv_t, kb, preferred_element_type=F32)
        n_ref[h] = decay * n + jnp.sum(w_end * k, axis=0, keepdims=True)
        m_ref[h] = jnp.broadcast_to(m_new, (1, LANE))


def _mlstm_scan_pallas(qh, kh, vh, rows, cols, n_ctx):
    B, H, T, Dh = qh.shape
    L = MLSTM_CHUNK
    nc, ncb = T // L, n_ctx // L
    assert T % L == 0 and n_ctx % L == 0 and H == N_HEADS

    def chunk(g, c):
        rev = jnp.where(c < ncb, ncb - 1 - c, nc - 1 - (c - ncb))
        return jnp.where(g % 2 == 0, c, rev)

    qkv_spec = pl.BlockSpec((1, H, L, Dh), lambda g, c: (g // 2, 0, chunk(g, c), 0))
    return pl.pallas_call(
        _mlstm_kernel,
        grid=(2 * B, nc),
        in_specs=[qkv_spec, qkv_spec, qkv_spec,
                  pl.BlockSpec((1, H, 1, 2, L), lambda g, c: (g, 0, chunk(g, c), 0, 0)),
                  pl.BlockSpec((1, H, 1, L, 2), lambda g, c: (g, 0, chunk(g, c), 0, 0))],
        out_specs=pl.BlockSpec((1, H, L, Dh), lambda g, c: (g, 0, chunk(g, c), 0)),
        out_shape=jax.ShapeDtypeStruct((2 * B, H, T, Dh), F32),
        scratch_shapes=[pltpu.VMEM((H, Dh, Dh), F32), pltpu.VMEM((H, 1, Dh), F32), pltpu.VMEM((H, 1, LANE), F32)],
        compiler_params=pltpu.CompilerParams(
            dimension_semantics=("arbitrary", "arbitrary"), vmem_limit_bytes=VMEM_LIMIT_BYTES),
        name="mlstm_scan",
    )(qh, kh, vh, rows, cols)


def _attn_kernel(*refs, n_band, t_total):
    q_ref = refs[0]
    band = refs[1:1 + 2 * n_band]
    kc_ref, vc_ref, sink_ref, o_ref = refs[1 + 2 * n_band:]
    n = pl.program_id(1)
    nt_dims = (((1,), (1,)), ((), ()))
    if n_band:
        qpos = n * ATT_BLOCK + lax.broadcasted_iota(jnp.int32, (ATT_BLOCK, n_band * ATT_BLOCK), 0)
        kpos = (n - 1) * ATT_BLOCK + lax.broadcasted_iota(jnp.int32, (ATT_BLOCK, n_band * ATT_BLOCK), 1)
        mask = (jnp.abs(qpos - kpos) <= WINDOW) & (kpos >= 0) & (kpos < t_total)
    for kvh in range(KV_HEADS):
        kc = kc_ref[0, kvh].astype(BF16)
        vc = vc_ref[0, kvh].astype(BF16)
        if n_band:
            kw = jnp.concatenate([band[j][0, kvh] for j in range(n_band)], axis=0).astype(BF16)
            vw = jnp.concatenate([band[n_band + j][0, kvh] for j in range(n_band)], axis=0).astype(BF16)
        for g in range(Q_PER_KV):
            h = kvh * Q_PER_KV + g
            q = q_ref[0, h].astype(BF16)
            sink = sink_ref[h][:, 0:1]
            s_ctx = lax.dot_general(q, kc, nt_dims, preferred_element_type=F32) * ATT_SCALE
            m = jnp.maximum(jnp.max(s_ctx, axis=1, keepdims=True), sink)
            if n_band:
                s_loc = lax.dot_general(q, kw, nt_dims, preferred_element_type=F32) * ATT_SCALE
                s_loc = jnp.where(mask, s_loc, _NEG_INF)
                m = jnp.maximum(m, jnp.max(s_loc, axis=1, keepdims=True))
            p_ctx = jnp.exp(s_ctx - m)
            den = jnp.sum(p_ctx, axis=1, keepdims=True) + jnp.exp(sink - m)
            o = jnp.dot(p_ctx.astype(BF16), vc, preferred_element_type=F32)
            if n_band:
                p_loc = jnp.exp(s_loc - m)
                den = den + jnp.sum(p_loc, axis=1, keepdims=True)
                o = o + jnp.dot(p_loc.astype(BF16), vw, preferred_element_type=F32)
            o_ref[0, h] = o / den


def _attention_pallas(q, k, v, kc, vc, sink):
    B, H, T, Dh = q.shape
    C = kc.shape[2]
    nb = T // ATT_BLOCK
    assert T % ATT_BLOCK == 0
    n_band = 0 if k is None else 3
    sink_b = jnp.broadcast_to(sink.astype(F32)[:, None, None], (H, 1, LANE))
    band_specs = [pl.BlockSpec((1, KV_HEADS, ATT_BLOCK, Dh),
                               lambda b, n, j=j: (b, 0, jnp.clip(n + j - 1, 0, nb - 1), 0)) for j in range(n_band)]
    ctx_spec = pl.BlockSpec((1, KV_HEADS, C, Dh), lambda b, n: (b, 0, 0, 0))
    kern = functools.partial(_attn_kernel, n_band=n_band, t_total=T)
    band_args = [] if k is None else [k] * 3 + [v] * 3
    return pl.pallas_call(
        kern,
        grid=(B, nb),
        in_specs=[pl.BlockSpec((1, H, ATT_BLOCK, Dh), lambda b, n: (b, 0, n, 0))] + band_specs * 2
                 + [ctx_spec, ctx_spec, pl.BlockSpec((H, 1, LANE), lambda b, n: (0, 0, 0))],
        out_specs=pl.BlockSpec((1, H, ATT_BLOCK, Dh), lambda b, n: (b, 0, n, 0)),
        out_shape=jax.ShapeDtypeStruct((B, H, T, Dh), F32),
        compiler_params=pltpu.CompilerParams(
            dimension_semantics=("arbitrary", "arbitrary"), vmem_limit_bytes=VMEM_LIMIT_BYTES),
        name="attention",
    )(q, *band_args, kc, vc, sink_b)


def rms_norm(x, g):
    xf = x.astype(F32)
    y = xf * lax.rsqrt(jnp.mean(xf * xf, axis=-1, keepdims=True) + EPS)
    return (y * g.astype(F32)).astype(x.dtype)


def heads(t):
    return t.reshape(t.shape[:-1] + (N_HEADS, HEAD_DIM))


def head_norm_merge(y, g):
    return rms_norm(y, g).reshape(y.shape[:-2] + (GROUP_W,))


def rope_2d(x, row, col):
    quarter = HEAD_DIM // 4
    inv = ROPE_BASE ** (-jnp.arange(quarter, dtype=F32) / quarter)
    xf = x.astype(F32)
    extra = (1,) * (x.ndim - 3)

    def rot(xa, pos):
        ang = pos.astype(F32)[:, None] * inv[None, :]
        ang = ang.reshape((1, ang.shape[0]) + extra + (quarter,))
        cos, sin = jnp.cos(ang), jnp.sin(ang)
        x1, x2 = xa[..., :quarter], xa[..., quarter:]
        return jnp.concatenate([x1 * cos - x2 * sin, x2 * cos + x1 * sin], axis=-1)

    half = HEAD_DIM // 2
    return jnp.concatenate([rot(xf[..., :half], row), rot(xf[..., half:], col)], axis=-1).astype(x.dtype)


def conv_mixer(hx, b_gate, c_gate, w, g):
    u = c_gate * hx
    up = jnp.pad(u, ((0, 0), (1, 1), (0, 0)))
    y = b_gate * (w[0] * up[:, :-2] + w[1] * up[:, 1:-1] + w[2] * up[:, 2:])
    return head_norm_merge(heads(y), g)


_RWKV_COL_BLOCK = IN_OFFSETS[2] // GROUP_W
assert IN_OFFSETS[2] % GROUP_W == 0 and W_LORA + A_LORA + G_LORA <= GROUP_W


def _rwkv_prep_kernel(r_ref, k_ref, v_ref, x_ref, vec_ref, w2_ref, a2_ref, g2_ref,
                      sh_ref, d0_ref, d1_ref, g_ref):
    r, k, v, x = r_ref[0], k_ref[0], v_ref[0], x_ref[0]
    lane = lax.broadcasted_iota(jnp.int32, (GROUP_W, GROUP_W), 1)
    sub = lax.broadcasted_iota(jnp.int32, (GROUP_W, GROUP_W), 0)
    seg_ones = jnp.where(lane // HEAD_DIM == sub // HEAD_DIM, 1.0, 0.0).astype(BF16)
    kkr = k * vec_ref[0:1, :]
    sq = kkr * kkr
    sq_hi = sq.astype(BF16)
    sq_lo = (sq - sq_hi.astype(F32)).astype(BF16)
    ss = (jnp.dot(sq_hi, seg_ones, preferred_element_type=F32)
          + jnp.dot(sq_lo, seg_ones, preferred_element_type=F32))
    kk = kkr * lax.rsqrt(ss + EPS)
    sh_ref[0, 0], sh_ref[1, 0], sh_ref[2, 0] = kk, r, v
    g_ref[0] = jnp.dot(jax.nn.sigmoid(x).astype(BF16), g2_ref[...], preferred_element_type=F32)
    xt = jnp.tanh(x).astype(BF16)
    xb = x.astype(BF16)
    for d, d_ref in enumerate((d0_ref, d1_ref)):
        lw = jnp.dot(xt, w2_ref[d], preferred_element_type=F32)
        la = jnp.dot(xb, a2_ref[d], preferred_element_type=F32)
        a = jax.nn.sigmoid(vec_ref[4 + d:5 + d, :] + la)
        d_ref[0, 0] = jnp.exp(-RWKV_DECAY_SCALE * jax.nn.sigmoid(vec_ref[2 + d:3 + d, :] + lw))
        d_ref[1, 0] = kk * a
        d_ref[2, 0] = k * (1 + (a - 1) * vec_ref[1:2, :])


def _rwkv_prep_pallas(p_all, w0, w2, a0, a2, g2, k_k, k_a, tb=768):
    B, T, _ = p_all.shape
    assert T % tb == 0
    C = GROUP_W
    vec = jnp.concatenate([k_k[None], k_a[None], w0, a0, jnp.zeros((2, C), F32)], axis=0)

    def pad_rows(w, row0):
        lead = w.shape[:-2]
        return jnp.zeros(lead + (C, C), BF16).at[..., row0:row0 + w.shape[-2], :].set(w.astype(BF16))

    w2p, a2p, g2p = pad_rows(w2, 0), pad_rows(a2, W_LORA), pad_rows(g2, W_LORA + A_LORA)
    col = lambda j: pl.BlockSpec((1, tb, C), lambda b, t, j=j: (b, t, _RWKV_COL_BLOCK + j))
    full = lambda a: pl.BlockSpec(a.shape, lambda b, t: (0,) * a.ndim)
    out3 = pl.BlockSpec((3, 1, tb, C), lambda b, t: (0, b, t, 0))
    return pl.pallas_call(
        _rwkv_prep_kernel,
        grid=(B, T // tb),
        in_specs=[col(0), col(1), col(2), col(3), full(vec), full(w2p), full(a2p), full(g2p)],
        out_specs=[out3, out3, out3, pl.BlockSpec((1, tb, C), lambda b, t: (b, t, 0))],
        out_shape=[jax.ShapeDtypeStruct((3, B, T, C), F32)] * 3 + [jax.ShapeDtypeStruct((B, T, C), F32)],
        compiler_params=pltpu.CompilerParams(
            dimension_semantics=("arbitrary", "arbitrary"), vmem_limit_bytes=VMEM_LIMIT_BYTES),
        name="rwkv_prep",
    )(p_all, p_all, p_all, p_all, vec, w2p, a2p, g2p)


def _rwkv_finish_kernel(y0_ref, y1_ref, r_ref, v_ref, kd0_ref, kd1_ref, g_ref, vec_ref, o_ref):
    lane = lax.broadcasted_iota(jnp.int32, (GROUP_W, GROUP_W), 1)
    sub = lax.broadcasted_iota(jnp.int32, (GROUP_W, GROUP_W), 0)
    seg_ones = jnp.where(lane // HEAD_DIM == sub // HEAD_DIM, 1.0, 0.0).astype(BF16)

    def head_sum(t):
        hi = t.astype(BF16)
        lo = (t - hi.astype(F32)).astype(BF16)
        return (jnp.dot(hi, seg_ones, preferred_element_type=F32)
                + jnp.dot(lo, seg_ones, preferred_element_type=F32))

    y = y0_ref[0] + y1_ref[0]
    yn = y * lax.rsqrt(head_sum(y * y) * (1.0 / HEAD_DIM) + EPS) * vec_ref[0:1, :]
    bonus = head_sum(r_ref[0, 0] * (kd0_ref[0, 0] + kd1_ref[0, 0]) * vec_ref[1:2, :]) * v_ref[0, 0]
    o_ref[0] = (yn + bonus) * g_ref[0]


def _rwkv_finish_pallas(y0, y1, shared, dir0, dir1, g, ln_g, r_k, tb=768):
    B, T, C = g.shape
    assert T % tb == 0
    vec = jnp.concatenate([ln_g.reshape(1, C), r_k.reshape(1, C), jnp.zeros((6, C), F32)], axis=0)
    row = pl.BlockSpec((1, tb, C), lambda b, t: (b, t, 0))
    plane = lambda q: pl.BlockSpec((1, 1, tb, C), lambda b, t, q=q: (q, b, t, 0))
    return pl.pallas_call(
        _rwkv_finish_kernel,
        grid=(B, T // tb),
        in_specs=[row, row, plane(1), plane(2), plane(2), plane(2), row, pl.BlockSpec(vec.shape, lambda b, t: (0, 0))],
        out_specs=row,
        out_shape=jax.ShapeDtypeStruct((B, T, C), F32),
        compiler_params=pltpu.CompilerParams(
            dimension_semantics=("arbitrary", "arbitrary"), vmem_limit_bytes=VMEM_LIMIT_BYTES),
        name="rwkv_finish",
    )(y0, y1, shared, shared, dir0, dir1, g, vec)


def rwkv_mixer(p_all, n_ctx, w0, w2, a0, a2, g2, k_k, k_a, r_k, ln_g, need_ctx):
    shared, dir0, dir1, g = _rwkv_prep_pallas(p_all, w0, w2, a0, a2, g2, k_k, k_a)
    y0, y1 = _rwkv_scan_pallas(shared, dir0, dir1, n_ctx)
    out = _rwkv_finish_pallas(y0, y1, shared, dir0, dir1, g, ln_g, r_k)
    return out[:, n_ctx:], (out[:, :n_ctx] if need_ctx else None)


def attn_project(q, k, v, q_g, k_g):
    B, T, _ = q.shape
    q = rms_norm(q.reshape(B, T, KV_HEADS, Q_PER_KV, HEAD_DIM), q_g)
    k = rms_norm(k.reshape(B, T, KV_HEADS, HEAD_DIM), k_g)
    v = v.reshape(B, T, KV_HEADS, HEAD_DIM)
    return q, k, v


def _head_major(t):
    B, T = t.shape[:2]
    return jnp.moveaxis(t.reshape(B, T, -1, HEAD_DIM), 2, 1)


def latent_attention(q, k, v, kc, vc, sink):
    o = _attention_pallas(_head_major(q), _head_major(k), _head_major(v), _head_major(kc), _head_major(vc), sink)
    return jnp.moveaxis(o, 1, 2)


def ctx_attention(qc, kc, vc, sink):
    o = _attention_pallas(_head_major(qc), None, None, _head_major(kc), _head_major(vc), sink)
    return jnp.moveaxis(o, 1, 2)


def mlstm_mixer(seq, n_ctx, i_b, f_b, out_g, need_ctx):
    q, k, v, o, gates = seq
    B, T, _ = q.shape
    L = MLSTM_CHUNK

    def th(t):
        return jnp.moveaxis(heads(t.astype(F32)), 2, 1)

    gates = gates.astype(F32).reshape(B, T, 2, 2, N_HEADS) + jnp.stack([i_b, f_b], axis=1).astype(F32)
    gates = jnp.moveaxis(gates, 1, -1)
    logi = gates[:, :, 0].reshape(B, 2, N_HEADS, T // L, L)
    logf = jax.nn.log_sigmoid(gates[:, :, 1]).reshape(B, 2, N_HEADS, T // L, L)
    bcum = jnp.stack([jnp.cumsum(logf[:, 0], axis=-1),
                      jnp.flip(jnp.cumsum(jnp.flip(logf[:, 1], axis=-1), axis=-1), axis=-1)], axis=1)
    rows = jnp.stack([bcum, logi], axis=-2).reshape(B * 2, N_HEADS, T // L, 2, L)
    cols = jnp.stack([bcum, logi], axis=-1).reshape(B * 2, N_HEADS, T // L, L, 2)
    h = _mlstm_scan_pallas(th(q), th(k) * (HEAD_DIM ** -0.5), th(v), rows, cols, n_ctx)
    h = h.reshape(B, 2, N_HEADS, T, HEAD_DIM)
    y = jax.nn.sigmoid(o) * head_norm_merge(jnp.moveaxis(h[:, 0] + h[:, 1], 1, 2), out_g)
    return y[:, n_ctx:], (y[:, :n_ctx] if need_ctx else None)


def kernel(x, c, ctx, c_ctx, ada_w, ada_b, norm1_g, norm2_g, w_in, w_out, conv_w, conv_g,
           rwkv_w0, rwkv_w2, rwkv_a0, rwkv_a2, rwkv_g2, rwkv_kk, rwkv_ka, rwkv_rk, rwkv_ln_g,
           att_q_g, att_k_g, att_sink, att_out_g, ml_i_b, ml_f_b, ml_out_g,
           peer_wq, peer_keys, peer_u, peer_v):
    B, T, D = x.shape
    n_ctx = ctx.shape[1]
    assert T % PEER_TM == 0 and (B * n_ctx) % PEER_TM == 0
    ROWS = T // GRID_W
    row = jnp.repeat(jnp.arange(ROWS), GRID_W)
    col = jnp.arange(ROWS * GRID_W) % GRID_W
    for l in range(DEPTH):
        need_ctx = l < DEPTH - 1
        mod = jax.nn.silu(c) @ ada_w[l] + ada_b[l]
        mod_c = jax.nn.silu(c_ctx) @ ada_w[l] + ada_b[l]
        sh1, sc1, gt1, sh2, sc2, gt2 = jnp.split(mod[:, None, :], 6, axis=-1)
        csh1, csc1, cgt1, csh2, csc2, cgt2 = jnp.split(mod_c, 6, axis=-1)

        h = rms_norm(x, norm1_g[l]) * (1 + sc1) + sh1
        hc = rms_norm(ctx, norm1_g[l]) * (1 + csc1) + csh1
        p_all = _mm3(jnp.concatenate([hc, h], axis=1), w_in[l])
        offs = (0,) + IN_OFFSETS
        S_ = [p_all[..., o_:o_ + n_] for o_, n_ in zip(offs, IN_SIZES)]
        P = [t[:, n_ctx:] for t in S_]
        Pc = [t[:, :n_ctx] for t in S_]

        y_a = conv_mixer(P[0], P[1], P[2], conv_w[l], conv_g[l])
        y_b, yc_b = rwkv_mixer(p_all, n_ctx, rwkv_w0[l], rwkv_w2[l], rwkv_a0[l], rwkv_a2[l],
                               rwkv_g2[l], rwkv_kk[l], rwkv_ka[l], rwkv_rk[l], rwkv_ln_g[l], need_ctx)
        q, k, v = attn_project(P[9], P[10], P[11], att_q_g[l], att_k_g[l])
        q, k = rope_2d(q, row, col), rope_2d(k, row, col)
        qc, kc, vc = attn_project(Pc[9], Pc[10], Pc[11], att_q_g[l], att_k_g[l])
        y_c = head_norm_merge(latent_attention(q, k, v, kc, vc, att_sink[l]), att_out_g[l])
        y_d, yc_d = mlstm_mixer(S_[12:17], n_ctx, ml_i_b[l], ml_f_b[l], ml_out_g[l], need_ctx)

        y = _mm3(jnp.concatenate([t.astype(x.dtype) for t in (y_a, y_b, y_c, y_d)], axis=-1), w_out[l])
        x = x + gt1 * y
        h2 = rms_norm(x, norm2_g[l]) * (1 + sc2) + sh2
        tok, resid = [h2.reshape(B * T, D)], [x.reshape(B * T, D)]
        gate_rows = [jnp.repeat(gt2.reshape(B, 1, D), T // PEER_TM, axis=0)]
        if need_ctx:
            yc_a = conv_mixer(Pc[0], Pc[1], Pc[2], conv_w[l], conv_g[l])
            yc_c = head_norm_merge(ctx_attention(qc, kc, vc, att_sink[l]), att_out_g[l])
            yc = _mm3(jnp.concatenate([t.astype(ctx.dtype) for t in (yc_a, yc_b, yc_c, yc_d)], axis=-1), w_out[l])
            ctx = ctx + cgt1 * yc
            hc2 = rms_norm(ctx, norm2_g[l]) * (1 + csc2) + csh2
            tok.append(hc2.reshape(-1, D))
            resid.append(ctx.reshape(-1, D))
            gate_rows.append(jnp.broadcast_to(cgt2.reshape(1, 1, D), (B * n_ctx // PEER_TM, 1, D)))
        new = _peer_dense(jnp.concatenate(tok, axis=0), peer_wq[l].astype(BF16),
                          peer_keys[l].reshape(2 * PEER_HEADS, N_KEYS, PEER_HALF).astype(BF16),
                          peer_u[l].astype(BF16), _peer_v_tiles(peer_v[l]),
                          jnp.concatenate(resid, axis=0), jnp.concatenate(gate_rows, axis=0), PEER_TM)
        x = new[:B * T].reshape(B, T, D)
        if need_ctx:
            ctx = new[B * T:].reshape(ctx.shape)
    return x
```
